```python
import math
import jax, jax.numpy as jnp
from jax import lax
import numpy as np

D_MODEL = 2048
BATCH = 4
SEQ = 4096
DEPTH = 2
DEC_BATCH = 8
DEC_SEQ = 16
PAST_LEN = 1024

CHUNK = 64
N_A_LAYERS = DEPTH // 2
N_B_LAYERS = DEPTH - N_A_LAYERS
MIX_WIDTH = (3 * D_MODEL) // 4
MEM_WIDTH = D_MODEL // 4
MEM_HEADS = 4
MEM_HEAD_DIM = MEM_WIDTH // MEM_HEADS
N_MEM = 256
SSM_GROUP = 16
SSM_GROUPS = MIX_WIDTH // SSM_GROUP
SSM_STATE = 64
SB_HEAD_DIM = 128
SB_HEADS = MIX_WIDTH // SB_HEAD_DIM
SB_BLOCK = 128
IN_WIDTH = 2 * MIX_WIDTH + 2 * MEM_WIDTH
OUT_WIDTH = MIX_WIDTH + MEM_WIDTH
EPS = 1e-6
DT_MIN = 1e-3
DT_MAX = 1e-1

kernel_name = 'yoco_s5_stickbreaking_stream_step'


def rms_norm(x, g):
    x32 = x.astype(jnp.float32)
    y = x32 * lax.rsqrt(jnp.mean(x32 * x32, axis=-1, keepdims=True) + EPS) * g.astype(jnp.float32)
    return y.astype(x.dtype)


def mem_attention(q, mk, mv):
    bsz, l = q.shape[:2]
    s = jnp.einsum('bqhd,bmhd->bhqm', q, mk.astype(q.dtype)).astype(jnp.float32) * (1.0 / math.sqrt(MEM_HEAD_DIM))
    p = jax.nn.softmax(s, axis=-1)
    o = jnp.einsum('bhqm,bmhd->bqhd', p.astype(q.dtype), mv.astype(q.dtype))
    return o.reshape(bsz, l, MEM_WIDTH)


def _ssm_combine(e1, e2):
    a1, b1 = e1
    a2, b2 = e2
    return a1 * a2, a2 * b1 + b2


def s5_scan(u, h0, lam_re, lam_im, log_dt, b_re, b_im, c_re, c_im, d):
    f32 = jnp.float32
    lam = lax.complex(lam_re.astype(f32), lam_im.astype(f32))
    dt = jnp.exp(log_dt.astype(f32))[:, None]
    lam_dt = lam * dt
    a_bar = jnp.exp(lam_dt)
    bmat = lax.complex(b_re.astype(f32), b_im.astype(f32))
    b_bar = ((a_bar - 1.0) / lam)[..., None] * bmat
    cmat = lax.complex(c_re.astype(f32), c_im.astype(f32))
    d32 = d.astype(f32)
    bsz, l, g, h = u.shape
    t = CHUNK if l % CHUNK == 0 else l
    n = l // t
    pows = jnp.exp(lam_dt[None] * jnp.arange(1, t + 1, dtype=f32)[:, None, None])
    u_blocks = u.astype(f32).reshape(bsz, n, t, g, h).swapaxes(0, 1)

    def step(h_prev, u_blk):
        bu = jnp.einsum('gph,btgh->btgp', b_bar, u_blk.astype(jnp.complex64))
        a = jnp.broadcast_to(a_bar, bu.shape)
        _, h_loc = lax.associative_scan(_ssm_combine, (a, bu), axis=1)
        hs = h_loc + pows[None] * h_prev[:, None]
        y = jnp.real(jnp.einsum('ghp,btgp->btgh', cmat, hs)) + d32 * u_blk
        return hs[:, -1], y

    h_last, y = lax.scan(step, h0, u_blocks)
    y = y.swapaxes(0, 1).reshape(bsz, l, g, h)
    return y, h_last


def stick_breaking_attention(q, k, v, q_offset):
    bsz, lq, nh, hd = q.shape
    lk = k.shape[1]
    blk = SB_BLOCK if lq % SB_BLOCK == 0 else lq
    nblk = lq // blk
    scale = 1.0 / math.sqrt(hd)
    k_pos = jnp.arange(lk)
    q_blocks = q.reshape(bsz, nblk, blk, nh, hd).swapaxes(0, 1)
    k = k.astype(q.dtype)
    v = v.astype(q.dtype)

    def one_block(args):
        q_blk, b_idx = args
        q_pos = q_offset + b_idx * blk + jnp.arange(blk)
        z = jnp.einsum('bqhd,bkhd->bhqk', q_blk, k).astype(jnp.float32) * scale
        causal = k_pos[None, :] < q_pos[:, None]
        log_keep = jnp.where(causal, jax.nn.log_sigmoid(-z), 0.0)
        later = lax.cumsum(log_keep, axis=3, reverse=True) - log_keep
        w = jnp.where(causal, jnp.exp(jax.nn.log_sigmoid(z) + later), 0.0)
        return jnp.einsum('bhqk,bkhd->bqhd', w.astype(v.dtype), v)

    out = lax.map(one_block, (q_blocks, jnp.arange(nblk)))
    return out.swapaxes(0, 1).reshape(bsz, lq, nh, hd)


def _split_proj(proj):
    return jnp.split(proj, [MIX_WIDTH, 2 * MIX_WIDTH, 2 * MIX_WIDTH + MEM_WIDTH], axis=-1)


def layer_a(x, h0, mk, mv, w_in, w_out, g_pre, g_post, lam_re, lam_im, log_dt,
            b_re, b_im, c_re, c_im, d, w_glu, b_glu):
    bsz, l, _ = x.shape
    hn = rms_norm(x, g_pre)
    u, gate, mq, mg = _split_proj(hn @ w_in)
    y, h_last = s5_scan(u.reshape(bsz, l, SSM_GROUPS, SSM_GROUP), h0, lam_re, lam_im, log_dt,
                        b_re, b_im, c_re, c_im, d)
    y = jax.nn.gelu(y.reshape(bsz, l, MIX_WIDTH).astype(x.dtype))
    y = y * jax.nn.sigmoid(y @ w_glu + b_glu)
    a_out = y * jax.nn.silu(gate)
    m_out = mem_attention(mq.reshape(bsz, l, MEM_HEADS, MEM_HEAD_DIM), mk, mv) * jax.nn.silu(mg)
    out = jnp.concatenate([a_out, m_out], axis=-1) @ w_out
    return x + rms_norm(out, g_post), h_last


def layer_b(x, k_all, v_all, q_offset, mk, mv, w_in, w_out, g_pre, g_post):
    bsz, l, _ = x.shape
    hn = rms_norm(x, g_pre)
    q, gate, mq, mg = _split_proj(hn @ w_in)
    o = stick_breaking_attention(q.reshape(bsz, l, SB_HEADS, SB_HEAD_DIM), k_all, v_all, q_offset)
    b_out = o.reshape(bsz, l, MIX_WIDTH) * jax.nn.silu(gate)
    m_out = mem_attention(mq.reshape(bsz, l, MEM_HEADS, MEM_HEAD_DIM), mk, mv) * jax.nn.silu(mg)
    out = jnp.concatenate([b_out, m_out], axis=-1) @ w_out
    return x + rms_norm(out, g_post)


def shared_kv(x, g_kv, w_kv):
    bsz, l, _ = x.shape
    kv = rms_norm(x, g_kv) @ w_kv
    k, v = jnp.split(kv, 2, axis=-1)
    return (k.reshape(bsz, l, SB_HEADS, SB_HEAD_DIM), v.reshape(bsz, l, SB_HEADS, SB_HEAD_DIM))


def trunk(x, mem_k, mem_v, ssm_h0, past_k, past_v, q_offset, p):
    new_h = []
    k_new = v_new = k_all = v_all = None
    for layer in range(DEPTH):
        if layer == N_A_LAYERS:
            k_new, v_new = shared_kv(x, p['g_kv'], p['w_kv'])
            if past_k is None:
                k_all, v_all = k_new, v_new
            else:
                k_all = jnp.concatenate([past_k.astype(k_new.dtype), k_new], axis=1)
                v_all = jnp.concatenate([past_v.astype(v_new.dtype), v_new], axis=1)
        if layer < N_A_LAYERS:
            i = layer
            x, h_last = layer_a(x, ssm_h0[i], mem_k[layer], mem_v[layer], p['w_in_a'][i], p['w_out_a'][i],
                                p['g_pre_a'][i], p['g_post_a'][i], p['ssm_lam_re'][i], p['ssm_lam_im'][i],
                                p['ssm_log_dt'][i], p['ssm_b_re'][i], p['ssm_b_im'][i], p['ssm_c_re'][i],
                                p['ssm_c_im'][i], p['ssm_d'][i], p['w_glu'][i], p['b_glu'][i])
            new_h.append(h_last)
        else:
            j = layer - N_A_LAYERS
            x = layer_b(x, k_all, v_all, q_offset, mem_k[layer], mem_v[layer], p['w_in_b'][j],
                        p['w_out_b'][j], p['g_pre_b'][j], p['g_post_b'][j])
    return x, jnp.stack(new_h), k_new, v_new


def setup_inputs(seed: int = 0) -> dict:
    key = jax.random.key(seed)
    ks = jax.random.split(key, 32)
    f32 = jnp.float32
    nrm = lambda k, shape, s=1.0: (jax.random.normal(k, shape, f32) * s).astype(f32)
    lam_im0 = jnp.pi * jnp.arange(SSM_STATE, dtype=f32)
    return {
        'x_prompt': nrm(ks[0], (BATCH, SEQ, D_MODEL)),
        'x_sample': nrm(ks[1], (DEC_BATCH, DEC_SEQ, D_MODEL)),
        'cache_k': nrm(ks[2], (DEC_BATCH, PAST_LEN, SB_HEADS, SB_HEAD_DIM)),
        'cache_v': nrm(ks[3], (DEC_BATCH, PAST_LEN, SB_HEADS, SB_HEAD_DIM)),
        'cache_mem_k': nrm(ks[4], (DEPTH, DEC_BATCH, N_MEM, MEM_HEADS, MEM_HEAD_DIM)),
        'cache_mem_v': nrm(ks[5], (DEPTH, DEC_BATCH, N_MEM, MEM_HEADS, MEM_HEAD_DIM)),
        'state_ssm': nrm(ks[6], (N_A_LAYERS, DEC_BATCH, SSM_GROUPS, SSM_STATE, 2), 0.1),
        'mem_prompt': nrm(ks[7], (BATCH, N_MEM, D_MODEL)),
        'w_in_a': nrm(ks[8], (N_A_LAYERS, D_MODEL, IN_WIDTH), D_MODEL ** -0.5),
        'w_out_a': nrm(ks[9], (N_A_LAYERS, OUT_WIDTH, D_MODEL), OUT_WIDTH ** -0.5),
        'g_pre_a': 1.0 + nrm(ks[10], (N_A_LAYERS, D_MODEL), 0.02),
        'g_post_a': 1.0 + nrm(ks[11], (N_A_LAYERS, D_MODEL), 0.02),
        'ssm_lam_re': -0.5 + nrm(ks[12], (N_A_LAYERS, SSM_GROUPS, SSM_STATE), 0.01),
        'ssm_lam_im': lam_im0 + nrm(ks[13], (N_A_LAYERS, SSM_GROUPS, SSM_STATE), 0.01),
        'ssm_log_dt': jax.random.uniform(ks[14], (N_A_LAYERS, SSM_GROUPS), f32, math.log(DT_MIN), math.log(DT_MAX)),
        'ssm_b_re': nrm(ks[15], (N_A_LAYERS, SSM_GROUPS, SSM_STATE, SSM_GROUP), (2 * SSM_GROUP) ** -0.5),
        'ssm_b_im': nrm(ks[16], (N_A_LAYERS, SSM_GROUPS, SSM_STATE, SSM_GROUP), (2 * SSM_GROUP) ** -0.5),
        'ssm_c_re': nrm(ks[17], (N_A_LAYERS, SSM_GROUPS, SSM_GROUP, SSM_STATE), (2 * SSM_STATE) ** -0.5),
        'ssm_c_im': nrm(ks[18], (N_A_LAYERS, SSM_GROUPS, SSM_GROUP, SSM_STATE), (2 * SSM_STATE) ** -0.5),
        'ssm_d': nrm(ks[19], (N_A_LAYERS, SSM_GROUPS, SSM_GROUP)),
        'w_glu': nrm(ks[20], (N_A_LAYERS, MIX_WIDTH, MIX_WIDTH), MIX_WIDTH ** -0.5),
        'b_glu': nrm(ks[21], (N_A_LAYERS, MIX_WIDTH), 0.01),
        'g_kv': 1.0 + nrm(ks[22], (D_MODEL,), 0.02),
        'w_kv': nrm(ks[23], (D_MODEL, 2 * MIX_WIDTH), D_MODEL ** -0.5),
        'w_in_b': nrm(ks[24], (N_B_LAYERS, D_MODEL, IN_WIDTH), D_MODEL ** -0.5),
        'w_out_b': nrm(ks[25], (N_B_LAYERS, OUT_WIDTH, D_MODEL), OUT_WIDTH ** -0.5),
        'g_pre_b': 1.0 + nrm(ks[26], (N_B_LAYERS, D_MODEL), 0.02),
        'g_post_b': 1.0 + nrm(ks[27], (N_B_LAYERS, D_MODEL), 0.02),
        'w_mem_k': nrm(ks[28], (DEPTH, D_MODEL, MEM_WIDTH), D_MODEL ** -0.5),
        'w_mem_v': nrm(ks[29], (DEPTH, D_MODEL, MEM_WIDTH), D_MODEL ** -0.5),
    }


def reference(x_prompt, x_sample, cache_k, cache_v, cache_mem_k, cache_mem_v, state_ssm, mem_prompt,
              w_in_a, w_out_a, g_pre_a, g_post_a, ssm_lam_re, ssm_lam_im, ssm_log_dt, ssm_b_re, ssm_b_im,
              ssm_c_re, ssm_c_im, ssm_d, w_glu, b_glu, g_kv, w_kv, w_in_b, w_out_b, g_pre_b, g_post_b,
              w_mem_k, w_mem_v):
    p = dict(w_in_a=w_in_a, w_out_a=w_out_a, g_pre_a=g_pre_a, g_post_a=g_post_a, ssm_lam_re=ssm_lam_re,
             ssm_lam_im=ssm_lam_im, ssm_log_dt=ssm_log_dt, ssm_b_re=ssm_b_re, ssm_b_im=ssm_b_im,
             ssm_c_re=ssm_c_re, ssm_c_im=ssm_c_im, ssm_d=ssm_d, w_glu=w_glu, b_glu=b_glu, g_kv=g_kv,
             w_kv=w_kv, w_in_b=w_in_b, w_out_b=w_out_b, g_pre_b=g_pre_b, g_post_b=g_post_b)
    bp = x_prompt.shape[0]
    mem_k_prompt = jnp.einsum('bmd,lde->lbme', mem_prompt, w_mem_k).reshape(DEPTH, bp, N_MEM, MEM_HEADS, MEM_HEAD_DIM)
    mem_v_prompt = jnp.einsum('bmd,lde->lbme', mem_prompt, w_mem_v).reshape(DEPTH, bp, N_MEM, MEM_HEADS, MEM_HEAD_DIM)
    h0_prompt = jnp.zeros((N_A_LAYERS, bp, SSM_GROUPS, SSM_STATE), jnp.complex64)
    y_prompt, h_prompt, k_prompt, v_prompt = trunk(x_prompt, mem_k_prompt, mem_v_prompt, h0_prompt,
                                                   None, None, 0, p)
    h0_sample = lax.complex(state_ssm[..., 0].astype(jnp.float32), state_ssm[..., 1].astype(jnp.float32))
    y_sample, h_sample, k_sample, v_sample = trunk(x_sample, cache_mem_k, cache_mem_v, h0_sample,
                                                   cache_k, cache_v, cache_k.shape[1], p)
    ssm_prompt = jnp.stack([jnp.real(h_prompt), jnp.imag(h_prompt)], axis=-1).astype(x_prompt.dtype)
    ssm_sample = jnp.stack([jnp.real(h_sample), jnp.imag(h_sample)], axis=-1).astype(state_ssm.dtype)
    return (y_prompt, y_sample, k_prompt, v_prompt, k_sample, v_sample, ssm_prompt, ssm_sample,
            mem_k_prompt, mem_v_prompt)
```

```python
import functools
import math

import jax
import jax.numpy as jnp
from jax import lax
from jax.experimental import pallas as pl
from jax.experimental.pallas import tpu as pltpu

EPS = 1e-6
CHUNK = 64
SSM_GROUP = 16
SSM_STATE = 64
SB_HEAD_DIM = 128
MEM_HEADS = 4
SUB = 8
TABLE_T = 64
LANES = 128
ATTN_BLOCK = 256
MIB = 1024 * 1024

bf16 = jnp.bfloat16
f32 = jnp.float32

_NT = (((1,), (1,)), ((), ()))


def _params(vmem_mib, semantics):
    return pltpu.CompilerParams(vmem_limit_bytes=vmem_mib * MIB, dimension_semantics=semantics)


def _resident(shape):
    zeros = (0,) * len(shape)
    return pl.BlockSpec(shape, lambda *_: zeros, pipeline_mode=pl.Buffered(1))


def _rms_scale(x):
    return x * lax.rsqrt(jnp.mean(x * x, axis=-1, keepdims=True) + EPS)


def _sigmoid(x):
    return 1.0 / (1.0 + jnp.exp(-x))


def _gelu_tanh(x):
    c = math.sqrt(2.0 / math.pi)
    return 0.5 * x * (1.0 + jnp.tanh(c * (x + 0.044715 * (x * x * x))))


def _memkv_kernel(x_ref, w_ref, of_ref, ob_ref):
    acc = jnp.dot(x_ref[...].astype(bf16), w_ref[...], preferred_element_type=f32)
    width = of_ref.shape[-1]
    for j in range(of_ref.shape[0]):
        blk = acc[:, j * width:(j + 1) * width]
        of_ref[j] = blk
        ob_ref[j] = blk.astype(bf16)


def _memkv(mem, w_cat, width):
    rows, d = mem.shape
    nout = w_cat.shape[1] // width
    tm = 256
    return pl.pallas_call(
        _memkv_kernel,
        grid=(rows // tm,),
        in_specs=[pl.BlockSpec((tm, d), lambda i: (i, 0)), _resident(w_cat.shape)],
        out_specs=[pl.BlockSpec((nout, tm, width), lambda i: (0, i, 0)),
                   pl.BlockSpec((nout, tm, width), lambda i: (0, i, 0))],
        out_shape=[jax.ShapeDtypeStruct((nout, rows, width), f32),
                   jax.ShapeDtypeStruct((nout, rows, width), bf16)],
        compiler_params=_params(40, ("parallel",)),
        name="memkv",
    )(mem, w_cat)


def _inproj_a_kernel(x_ref, g_ref, wut_ref, wr_ref, ut_ref, p_ref, hn_ref):
    ns, bc, _ = x_ref.shape
    for s in range(ns):
        hn = (_rms_scale(x_ref[s]) * g_ref[...]).astype(bf16)
        hn_ref[s * bc:(s + 1) * bc, :] = hn
        ut = lax.dot_general(wut_ref[...], hn, _NT, preferred_element_type=f32)
        ut_ref[s] = ut.reshape(ut_ref.shape[1:]).astype(bf16)
    step = 512
    for c in range(0, wr_ref.shape[1], step):
        r = jnp.dot(hn_ref[...], wr_ref[:, c:c + step], preferred_element_type=f32)
        p_ref[:, :, c:c + step] = r.astype(bf16).reshape(ns, bc, step)


def _inproj_a(xt, g, w_ut, w_rest, ns):
    t, bc, d = xt.shape
    mix = w_ut.shape[0]
    rest = w_rest.shape[1]
    groups = mix // SSM_GROUP
    return pl.pallas_call(
        _inproj_a_kernel,
        grid=(t // ns,),
        in_specs=[pl.BlockSpec((ns, bc, d), lambda i: (i, 0, 0)),
                  _resident(g.shape), _resident(w_ut.shape), _resident(w_rest.shape)],
        out_specs=[pl.BlockSpec((ns, groups, SSM_GROUP, bc), lambda i: (i, 0, 0, 0)),
                   pl.BlockSpec((ns, bc, rest), lambda i: (i, 0, 0))],
        out_shape=[jax.ShapeDtypeStruct((t, groups, SSM_GROUP, bc), bf16),
                   jax.ShapeDtypeStruct((t, bc, rest), bf16)],
        scratch_shapes=[pltpu.VMEM((ns * bc, d), bf16)],
        compiler_params=_params(52, ("parallel",)),
        name="inproj_a",
    )(xt, g, w_ut, w_rest)


def _inproj_plain_kernel(x_ref, g_ref, w_ref, u_ref, p_ref):
    hn = (_rms_scale(x_ref[...]) * g_ref[...]).astype(bf16)
    mix = u_ref.shape[1]
    u_ref[...] = jnp.dot(hn, w_ref[:, :mix], preferred_element_type=f32).astype(bf16)
    p_ref[...] = jnp.dot(hn, w_ref[:, mix:], preferred_element_type=f32).astype(bf16)


def _inproj_plain(x, g, w, mix):
    rows, d = x.shape
    rest = w.shape[1] - mix
    return pl.pallas_call(
        _inproj_plain_kernel,
        grid=(1,),
        in_specs=[pl.BlockSpec((rows, d), lambda i: (0, 0)), _resident(g.shape), _resident(w.shape)],
        out_specs=[pl.BlockSpec((rows, mix), lambda i: (0, 0)), pl.BlockSpec((rows, rest), lambda i: (0, 0))],
        out_shape=[jax.ShapeDtypeStruct((rows, mix), bf16), jax.ShapeDtypeStruct((rows, rest), bf16)],
        compiler_params=_params(40, ("arbitrary",)),
        name="inproj_plain",
    )(x, g, w)


def _s5_tables_kernel(lre_ref, lim_ref, caa_ref, cab_ref, ba_ref, bb_ref, dd_ref,
                      tt_ref, ca_ref, wb_ref):
    t_len = TABLE_T
    rows = t_len * SSM_GROUP
    lre = lre_ref[...]
    lim = lim_ref[...]
    kk = lax.broadcasted_iota(jnp.int32, (t_len, LANES), 0).astype(f32)

    def powers(k):
        mag = jnp.exp(lre * k)
        th = lim * k
        return mag * jnp.cos(th), mag * jnp.sin(th)

    pr1, pi1 = powers(kk + 1.0)
    pr0, pi0 = powers((t_len - 1.0) - kk)
    p4 = jnp.concatenate([pr1, pi1, pr0, pi0], axis=1)
    r_i = lax.broadcasted_iota(jnp.int32, (rows, t_len), 0) // SSM_GROUP
    c_i = lax.broadcasted_iota(jnp.int32, (rows, t_len), 1)
    rep = (r_i == c_i).astype(f32)
    pw = jnp.dot(rep, p4, preferred_element_type=f32, precision=lax.Precision.HIGHEST)

    def tile_rows(v, n):
        return jnp.broadcast_to(v[None], (n,) + v.shape).reshape(n * v.shape[0], v.shape[1])

    caa = caa_ref[...]
    cab = cab_ref[...]
    ba = ba_ref[...]
    bb = bb_ref[...]
    ca = pw[:, 0:128] * tile_rows(caa, t_len) + pw[:, 128:256] * tile_rows(cab, t_len)
    wbt = pw[:, 256:384] * tile_rows(ba, t_len) + pw[:, 384:512] * tile_rows(bb, t_len)
    ca_ref[...] = ca.astype(bf16)
    wb_ref[...] = wbt.T.astype(bf16)

    blk = SUB * SSM_GROUP
    nblk = rows // blk
    rt = wbt[rows - blk:, :]
    m = [None] * nblk
    for d in range(1, nblk):
        m[d] = lax.dot_general(ca[(d - 1) * blk:d * blk, :], rt, _NT, preferred_element_type=f32,
                               precision=lax.Precision.HIGHEST)
    ca0 = jnp.concatenate([caa, ca[:blk - SSM_GROUP, :]], axis=0)
    kj = lax.dot_general(ca0, tile_rows(ba, SUB), _NT, preferred_element_type=f32,
                         precision=lax.Precision.HIGHEST)
    lane = lax.broadcasted_iota(jnp.int32, (SSM_GROUP, LANES), 1)
    hrow = lax.broadcasted_iota(jnp.int32, (SSM_GROUP, LANES), 0)
    s0_lane = lane // SSM_GROUP
    skip = jnp.where(lane % SSM_GROUP == hrow, dd_ref[...], 0.0)
    kjs = [kj[j * SSM_GROUP:(j + 1) * SSM_GROUP, :] for j in range(SUB)]
    kjs[0] = kjs[0] + skip
    drows = []
    for t0 in range(SUB):
        acc = jnp.zeros((SSM_GROUP, LANES), f32)
        for j in range(t0 + 1):
            acc = acc + jnp.where(s0_lane == t0 - j, kjs[j], 0.0)
        drows.append(acc)
    m[0] = jnp.concatenate(drows, axis=0)
    zero = jnp.zeros((blk, blk), f32)
    r1 = jnp.concatenate([m[d] for d in range(nblk - 1, -1, -1)], axis=1)
    r0 = jnp.concatenate([m[d] for d in range(nblk - 2, -1, -1)] + [zero], axis=1)
    tt_ref[...] = jnp.concatenate([r0, r1], axis=0).astype(bf16)


def _s5_tables(lre2, lim2, caa, cab, ba, bb, dd):
    groups = lre2.shape[0]
    rows = TABLE_T * SSM_GROUP

    def gspec(shape):
        return pl.BlockSpec((None,) + shape, lambda g: (g,) + (0,) * len(shape))

    return pl.pallas_call(
        _s5_tables_kernel,
        grid=(groups,),
        in_specs=[gspec((1, LANES)), gspec((1, LANES)), gspec((SSM_GROUP, LANES)), gspec((SSM_GROUP, LANES)),
                  gspec((SSM_GROUP, LANES)), gspec((SSM_GROUP, LANES)), gspec((SSM_GROUP, 1))],
        out_specs=[gspec((2 * SUB * SSM_GROUP, rows)), gspec((rows, LANES)), gspec((LANES, rows))],
        out_shape=[jax.ShapeDtypeStruct((groups, 2 * SUB * SSM_GROUP, rows), bf16),
                   jax.ShapeDtypeStruct((groups, rows, LANES), bf16),
                   jax.ShapeDtypeStruct((groups, LANES, rows), bf16)],
        compiler_params=_params(40, ("parallel",)),
        name="s5_tables",
    )(lre2, lim2, caa, cab, ba, bb, dd)


def _cmul(ar, ai, x):
    half = x.shape[0] // 2
    xr = x[:half]
    xi = x[half:]
    return jnp.concatenate([ar * xr - ai * xi, ar * xi + ai * xr], axis=0)


def _s5_kernel(ut_ref, tt_ref, ca_ref, wb_ref, ap_ref, h0_ref, y_ref, hfin_ref, yt_ref, *, nc):
    g8 = pl.program_id(1)
    t_len, _, bc = ut_ref.shape
    rows = t_len * SSM_GROUP
    pair = 2 * SUB * SSM_GROUP
    table_rows = tt_ref.shape[1]
    z = ut_ref[...].reshape(rows, bc)
    ys = []
    for t2 in range(rows // pair):
        kk = pair * (t2 + 1)
        ys.append(jnp.dot(tt_ref[:, table_rows - kk:], z[:kk], preferred_element_type=f32))
    y = jnp.concatenate(ys, axis=0) if len(ys) > 1 else ys[0]
    state = jnp.dot(wb_ref[:, table_rows - rows:], z, preferred_element_type=f32)
    ap = ap_ref[...]
    h0 = h0_ref[...]
    state = state + _cmul(ap[:, 0:1], ap[:, 1:2], h0)
    lane = lax.broadcasted_iota(jnp.int32, state.shape, 1) % nc
    step = 0
    while (1 << step) < nc:
        sh = 1 << step
        shifted = jnp.where(lane >= sh, pltpu.roll(state, sh, axis=1), 0.0)
        state = state + _cmul(ap[:, 2 + 2 * step:3 + 2 * step], ap[:, 3 + 2 * step:4 + 2 * step], shifted)
        step += 1
    hfin_ref[...] = state
    if nc > 1:
        h_in = jnp.where(lane >= 1, pltpu.roll(state, 1, axis=1), 0.0) + h0
    else:
        h_in = h0
    y = y + jnp.dot(ca_ref[:rows, :], h_in.astype(bf16), preferred_element_type=f32)
    off = pl.multiple_of(g8 * SSM_GROUP, SSM_GROUP)
    yt_ref[:, pl.ds(off, SSM_GROUP), :] = y.reshape(t_len, SSM_GROUP, bc)

    @pl.when(g8 == pl.num_programs(1) - 1)
    def _():
        for t in range(t_len):
            y_ref[t] = yt_ref[t].T.astype(bf16)


def _s5(ut4, tt, ca, wb, apow, h0, nc):
    t_len, groups, _, bc = ut4.shape
    per = LANES // SSM_GROUP
    rows = TABLE_T * SSM_GROUP
    mix = groups * SSM_GROUP
    ncol = apow.shape[-1]
    return pl.pallas_call(
        functools.partial(_s5_kernel, nc=nc),
        grid=(groups // per, per),
        in_specs=[pl.BlockSpec((t_len, None, SSM_GROUP, bc), lambda G, g: (0, G * per + g, 0, 0)),
                  pl.BlockSpec((None, 2 * SUB * SSM_GROUP, rows), lambda G, g: (G * per + g, 0, 0)),
                  pl.BlockSpec((None, rows, LANES), lambda G, g: (G * per + g, 0, 0)),
                  pl.BlockSpec((None, LANES, rows), lambda G, g: (G * per + g, 0, 0)),
                  pl.BlockSpec((None, SSM_STATE, ncol), lambda G, g: (G * per + g, 0, 0)),
                  pl.BlockSpec((None, 2 * SSM_STATE, bc), lambda G, g: (G * per + g, 0, 0))],
        out_specs=[pl.BlockSpec((t_len, bc, LANES), lambda G, g: (0, 0, G)),
                   pl.BlockSpec((None, 2 * SSM_STATE, bc), lambda G, g: (G * per + g, 0, 0))],
        out_shape=[jax.ShapeDtypeStruct((t_len, bc, mix), bf16),
                   jax.ShapeDtypeStruct((groups, 2 * SSM_STATE, bc), f32)],
        scratch_shapes=[pltpu.VMEM((t_len, LANES, bc), f32)],
        compiler_params=_params(48, ("parallel", "arbitrary")),
        name="s5",
    )(ut4, tt, ca, wb, apow, h0)


def _ld(ref, start, n, c0, c1):
    if len(ref.shape) == 3:
        s, r = divmod(start, ref.shape[1])
        return ref[s, r:r + n, c0:c1]
    return ref[start:start + n, c0:c1]


def _post_kernel(y_ref, p_ref, x_ref, mk_ref, mv_ref, wglu_ref, bglu_ref, wout_ref, gpost_ref,
                 o_ref, cat_ref, *, glu, segs, mem_scale):
    mix = y_ref.shape[-1]
    memw = mk_ref.shape[-1]
    hd = memw // MEM_HEADS
    rows = cat_ref.shape[0]
    y = y_ref[...].reshape(rows, mix).astype(f32)
    gate = p_ref[:, :mix] if len(p_ref.shape) == 2 else p_ref[:, :, :mix].reshape(rows, mix)
    gate = gate.astype(f32)
    if glu:
        y = _gelu_tanh(y)
        zz = jnp.dot(y.astype(bf16), wglu_ref[...], preferred_element_type=f32) + bglu_ref[...]
        y = y * _sigmoid(zz)
    cat_ref[:, :mix] = (y * (gate * _sigmoid(gate))).astype(bf16)
    for b, pieces in enumerate(segs):
        for h in range(MEM_HEADS):
            cq = mix + h * hd
            cg = mix + memw + h * hd
            q = jnp.concatenate([_ld(p_ref, st, n, cq, cq + hd) for st, n in pieces], axis=0)
            mg = jnp.concatenate([_ld(p_ref, st, n, cg, cg + hd) for st, n in pieces], axis=0).astype(f32)
            k = mk_ref[b, :, h * hd:(h + 1) * hd]
            v = mv_ref[b, :, h * hd:(h + 1) * hd]
            s = lax.dot_general(q, k, _NT, preferred_element_type=f32) * mem_scale
            e = jnp.exp(s - jnp.max(s, axis=-1, keepdims=True))
            p = e / jnp.sum(e, axis=-1, keepdims=True)
            o = jnp.dot(p.astype(bf16), v, preferred_element_type=f32)
            om = (o * (mg * _sigmoid(mg))).astype(bf16)
            off = 0
            for st, n in pieces:
                cat_ref[st:st + n, cq:cq + hd] = om[off:off + n]
                off += n
    out = jnp.dot(cat_ref[...], wout_ref[...], preferred_element_type=f32)
    d = out.shape[-1]
    o_ref[...] = (x_ref[...].reshape(rows, d) + _rms_scale(out) * gpost_ref[...]).reshape(o_ref.shape)


def _post(y, p, x, mk, mv, wglu, bglu, wout, gpost, *, glu, segs, grid, y_spec, p_spec, x_spec, mem_spec, rows):
    mem_scale = 1.0 / math.sqrt(mk.shape[-1] // MEM_HEADS)
    return pl.pallas_call(
        functools.partial(_post_kernel, glu=glu, segs=segs, mem_scale=mem_scale),
        grid=grid,
        in_specs=[y_spec, p_spec, x_spec, mem_spec, mem_spec,
                  _resident(wglu.shape), _resident(bglu.shape), _resident(wout.shape), _resident(gpost.shape)],
        out_specs=x_spec,
        out_shape=jax.ShapeDtypeStruct(x.shape, f32),
        scratch_shapes=[pltpu.VMEM((rows, wout.shape[0]), bf16)],
        compiler_params=_params(48, ("parallel",)),
        name="post_glu" if glu else "post",
    )(y, p, x, mk, mv, wglu, bglu, wout, gpost)


def _kvb_kernel(x_ref, gkv_ref, gb_ref, wkv_ref, wb_ref, k_ref, v_ref, kvb_ref, q_ref, pr_ref):
    xs = _rms_scale(x_ref[...])
    hkv = (xs * gkv_ref[...]).astype(bf16)
    hb = (xs * gb_ref[...]).astype(bf16)
    mix = k_ref.shape[1]
    step = 512
    for c in range(0, wkv_ref.shape[1], step):
        r = jnp.dot(hkv, wkv_ref[:, c:c + step], preferred_element_type=f32)
        kvb_ref[:, c:c + step] = r.astype(bf16)
        if c < mix:
            k_ref[:, c:c + step] = r
        else:
            v_ref[:, c - mix:c - mix + step] = r
    for c in range(0, wb_ref.shape[1], step):
        r = jnp.dot(hb, wb_ref[:, c:c + step], preferred_element_type=f32).astype(bf16)
        if c < mix:
            q_ref[:, c:c + step] = r
        else:
            pr_ref[:, c - mix:c - mix + step] = r


def _kvb(x, gkv, gb, wkv, wb, mix):
    n, d = x.shape
    tm = min(256, n)
    rest = wb.shape[1] - mix
    row = lambda w: pl.BlockSpec((tm, w), lambda i: (i, 0))
    return pl.pallas_call(
        _kvb_kernel,
        grid=(n // tm,),
        in_specs=[row(d), _resident(gkv.shape), _resident(gb.shape), _resident(wkv.shape), _resident(wb.shape)],
        out_specs=[row(mix), row(mix), row(2 * mix), row(mix), row(rest)],
        out_shape=[jax.ShapeDtypeStruct((n, mix), f32), jax.ShapeDtypeStruct((n, mix), f32),
                   jax.ShapeDtypeStruct((n, 2 * mix), bf16), jax.ShapeDtypeStruct((n, mix), bf16),
                   jax.ShapeDtypeStruct((n, rest), bf16)],
        compiler_params=_params(56, ("parallel",)),
        name="kvb",
    )(x, gkv, gb, wkv, wb)


def _sb_block(q, k, v, tri, carry, acc, mask):
    z = lax.dot_general(q, k, _NT, preferred_element_type=f32)
    lk = -(jnp.maximum(z, 0.0) + jnp.log(1.0 + jnp.exp(-jnp.abs(z))))
    if mask is not None:
        lk = jnp.where(mask, lk, 0.0)
    inner = jnp.dot(lk.astype(bf16), tri, preferred_element_type=f32)
    w = jnp.exp(z + lk + inner + carry)
    if mask is not None:
        w = jnp.where(mask, w, 0.0)
    acc = acc + jnp.dot(w.astype(bf16), v, preferred_element_type=f32)
    carry = carry + inner[:, 0:1] + lk[:, 0:1]
    return acc, carry


def _tri_and_mask(n):
    r = lax.broadcasted_iota(jnp.int32, (n, n), 0)
    c = lax.broadcasted_iota(jnp.int32, (n, n), 1)
    return (r > c).astype(bf16), c < r


def _sb_attn_kernel(q_ref, k_ref, v_ref, o_ref):
    i = pl.program_id(2)
    tq = q_ref.shape[0]
    q = q_ref[...]
    tri, causal = _tri_and_mask(tq)
    d0 = pl.multiple_of(i * tq, tq)
    acc = jnp.zeros((tq, q_ref.shape[1]), f32)
    carry = jnp.zeros((tq, 1), f32)
    acc, carry = _sb_block(q, k_ref[pl.ds(d0, tq), :], v_ref[pl.ds(d0, tq), :], tri, carry, acc, causal)

    def body(jj, st):
        off = pl.multiple_of((i - 1 - jj) * tq, tq)
        return _sb_block(q, k_ref[pl.ds(off, tq), :], v_ref[pl.ds(off, tq), :], tri, st[1], st[0], None)

    acc, carry = lax.fori_loop(0, i, body, (acc, carry))
    o_ref[...] = acc.astype(bf16)


def _sb_attn(q, kvb, bsz, seq, heads):
    tq = ATTN_BLOCK
    hd = SB_HEAD_DIM
    nq = seq // tq
    return pl.pallas_call(
        _sb_attn_kernel,
        grid=(bsz, heads, nq),
        in_specs=[pl.BlockSpec((tq, hd), lambda b, h, i: (b * nq + i, h)),
                  pl.BlockSpec((seq, hd), lambda b, h, i: (b, h)),
                  pl.BlockSpec((seq, hd), lambda b, h, i: (b, heads + h))],
        out_specs=pl.BlockSpec((tq, hd), lambda b, h, i: (b * nq + i, h)),
        out_shape=jax.ShapeDtypeStruct(q.shape, bf16),
        compiler_params=_params(40, ("parallel", "parallel", "arbitrary")),
        name="sb_attn",
    )(q, kvb, kvb)


def _sb_attn_sample_kernel(q_ref, kn_ref, vn_ref, kc_ref, vc_ref, o_ref):
    tq = q_ref.shape[0]
    past = kc_ref.shape[0]
    blk = min(ATTN_BLOCK, past)
    q = q_ref[...]
    tri_n, causal = _tri_and_mask(tq)
    acc = jnp.zeros((tq, q_ref.shape[1]), f32)
    carry = jnp.zeros((tq, 1), f32)
    acc, carry = _sb_block(q, kn_ref[...], vn_ref[...], tri_n, carry, acc, causal)
    tri_p, _ = _tri_and_mask(blk)
    for j in range(past // blk - 1, -1, -1):
        kb = kc_ref[j * blk:(j + 1) * blk, :].astype(bf16)
        vb = vc_ref[j * blk:(j + 1) * blk, :].astype(bf16)
        acc, carry = _sb_block(q, kb, vb, tri_p, carry, acc, None)
    o_ref[...] = acc.astype(bf16)


def _sb_attn_sample(q, kvb, cache_k3, cache_v3, bsz, tq, heads):
    hd = SB_HEAD_DIM
    past = cache_k3.shape[1]
    return pl.pallas_call(
        _sb_attn_sample_kernel,
        grid=(bsz, heads),
        in_specs=[pl.BlockSpec((tq, hd), lambda b, h: (b, h)),
                  pl.BlockSpec((tq, hd), lambda b, h: (b, h)),
                  pl.BlockSpec((tq, hd), lambda b, h: (b, heads + h)),
                  pl.BlockSpec((None, past, hd), lambda b, h: (b, 0, h)),
                  pl.BlockSpec((None, past, hd), lambda b, h: (b, 0, h))],
        out_specs=pl.BlockSpec((tq, hd), lambda b, h: (b, h)),
        out_shape=jax.ShapeDtypeStruct(q.shape, bf16),
        compiler_params=_params(40, ("parallel", "parallel")),
        name="sb_attn_sample",
    )(q, kvb, kvb, cache_k3, cache_v3)


def _ssm_param_tables(lam_re, lam_im, log_dt, b_re, b_im, c_re, c_im, dvec):
    dt = jnp.exp(log_dt.astype(f32))[:, None]
    lre = lam_re.astype(f32) * dt
    lim = lam_im.astype(f32) * dt
    lam = lax.complex(lam_re.astype(f32), lam_im.astype(f32))
    a_bar = jnp.exp(lax.complex(lre, lim))
    b_bar = ((a_bar - 1.0) / lam)[..., None] * lax.complex(b_re.astype(f32), b_im.astype(f32))
    br = jnp.swapaxes(jnp.real(b_bar), 1, 2)
    bi = jnp.swapaxes(jnp.imag(b_bar), 1, 2)
    cr = c_re.astype(f32)
    ci = c_im.astype(f32)
    dup = lambda v: jnp.concatenate([v, v], axis=-1)
    lre2 = dup(lre)[:, None, :]
    lim2 = dup(lim)[:, None, :]
    caa = jnp.concatenate([cr, -ci], axis=-1)
    cab = jnp.concatenate([-ci, -cr], axis=-1)
    ba = jnp.concatenate([br, bi], axis=-1)
    bb = jnp.concatenate([-bi, br], axis=-1)
    dd = dvec.astype(f32)[:, :, None]
    return lre, lim, (lre2, lim2, caa, cab, ba, bb, dd)


def _chunk_powers(lre, lim, t_len, nc):
    cols = []
    exps = [t_len]
    j = 0
    while (1 << j) < nc:
        exps.append(t_len * (1 << j))
        j += 1
    for e in exps:
        mag = jnp.exp(lre * e)
        cols += [mag * jnp.cos(lim * e), mag * jnp.sin(lim * e)]
    return jnp.stack(cols, axis=-1)


def _layer_a(x, t_len, nc, h0_lanes, mk, mv, wa, tables, lre, lim, prompt):
    n, d = x.shape
    mix = wa["w_glu"].shape[0]
    groups = mix // SSM_GROUP
    tt, ca, wb = tables
    apow = _chunk_powers(lre, lim, t_len, nc)
    if prompt:
        bc = n // t_len
        ns = 2
        xt = jnp.transpose(x.reshape(bc, t_len, d), (1, 0, 2))
        ut4, p = _inproj_a(xt, wa["g_pre"], wa["w_ut"], wa["w_rest"], ns)
        y, hfin = _s5(ut4, tt, ca, wb, apow, h0_lanes, nc)
        streams = bc // nc
        rest = p.shape[-1]
        segs = tuple(tuple((s * bc + b * nc, nc) for s in range(ns)) for b in range(streams))
        blk = lambda w: pl.BlockSpec((ns, bc, w), lambda i: (i, 0, 0))
        x1t = _post(y, p, xt, mk, mv, wa["w_glu"], wa["b_glu"], wa["w_out"], wa["g_post"],
                    glu=True, segs=segs, grid=(t_len // ns,), rows=ns * bc,
                    y_spec=blk(mix), p_spec=blk(rest), x_spec=blk(d), mem_spec=_resident(mk.shape))
        return jnp.transpose(x1t, (1, 0, 2)).reshape(n, d), hfin
    streams = n // t_len
    lanes = h0_lanes.shape[-1]
    u, p = _inproj_plain(x, wa["g_pre"], wa["w_in"], mix)
    ut4 = jnp.transpose(u.reshape(streams, t_len, groups, SSM_GROUP), (1, 2, 3, 0))
    ut4 = jnp.pad(ut4, ((0, 0), (0, 0), (0, 0), (0, lanes - streams)))
    y3, hfin = _s5(ut4, tt, ca, wb, apow, h0_lanes, nc)
    y = jnp.transpose(y3[:, :streams, :], (1, 0, 2)).reshape(n, mix)
    rest = p.shape[-1]
    whole = lambda w: pl.BlockSpec((n, w), lambda i: (0, 0))
    x1 = _post(y, p, x, mk, mv, wa["w_glu"], wa["b_glu"], wa["w_out"], wa["g_post"],
               glu=True, segs=tuple(((b * t_len, t_len),) for b in range(streams)), grid=(1,), rows=n,
               y_spec=whole(mix), p_spec=whole(rest), x_spec=whole(d), mem_spec=_resident(mk.shape))
    return x1, hfin


def kernel(x_prompt, x_sample, cache_k, cache_v, cache_mem_k, cache_mem_v, state_ssm, mem_prompt, w_in_a, w_out_a, g_pre_a, g_post_a, ssm_lam_re, ssm_lam_im, ssm_log_dt, ssm_b_re, ssm_b_im, ssm_c_re, ssm_c_im, ssm_d, w_glu, b_glu, g_kv, w_kv, w_in_b, w_out_b, g_pre_b, g_post_b, w_mem_k, w_mem_v):
    bsz, seq, d = x_prompt.shape
    dbsz, dseq, _ = x_sample.shape
    mix = w_glu.shape[-1]
    memw = w_mem_k.shape[-1]
    heads = mix // SB_HEAD_DIM
    groups = mix // SSM_GROUP
    n_mem = mem_prompt.shape[1]
    depth = w_mem_k.shape[0]
    assert depth == 2 and w_in_a.shape[0] == 1 and w_in_b.shape[0] == 1
    assert seq % CHUNK == 0 and (bsz * seq // CHUNK) % LANES == 0 and seq % ATTN_BLOCK == 0
    assert dseq % (2 * SUB) == 0 and dseq <= TABLE_T and dbsz <= LANES
    nc = seq // CHUNK
    assert nc & (nc - 1) == 0

    row = lambda v: v.astype(f32).reshape(1, -1)
    wa = dict(
        w_in=w_in_a[0].astype(bf16),
        w_ut=w_in_a[0][:, :mix].T.astype(bf16),
        w_rest=w_in_a[0][:, mix:].astype(bf16),
        w_glu=w_glu[0].astype(bf16), b_glu=row(b_glu[0]),
        w_out=w_out_a[0].astype(bf16), g_pre=row(g_pre_a[0]), g_post=row(g_post_a[0]))
    qscale = 1.0 / math.sqrt(SB_HEAD_DIM)
    w_b = jnp.concatenate([w_in_b[0][:, :mix] * qscale, w_in_b[0][:, mix:]], axis=1).astype(bf16)
    w_kv_b = w_kv.astype(bf16)
    w_out_bb = w_out_b[0].astype(bf16)
    w_mem = jnp.concatenate([w_mem_k[0], w_mem_k[1], w_mem_v[0], w_mem_v[1]], axis=1).astype(bf16)

    memf, memb = _memkv(mem_prompt.reshape(bsz * n_mem, d), w_mem, memw)
    mem_k_prompt = memf[:depth].reshape(depth, bsz, n_mem, MEM_HEADS, memw // MEM_HEADS)
    mem_v_prompt = memf[depth:].reshape(depth, bsz, n_mem, MEM_HEADS, memw // MEM_HEADS)
    mkp = memb[:depth].reshape(depth, bsz, n_mem, memw)
    mvp = memb[depth:].reshape(depth, bsz, n_mem, memw)
    mks = cache_mem_k.reshape(depth, dbsz, n_mem, memw).astype(bf16)
    mvs = cache_mem_v.reshape(depth, dbsz, n_mem, memw).astype(bf16)

    lre, lim, tab_in = _ssm_param_tables(ssm_lam_re[0], ssm_lam_im[0], ssm_log_dt[0], ssm_b_re[0], ssm_b_im[0],
                                         ssm_c_re[0], ssm_c_im[0], ssm_d[0])
    tables = _s5_tables(*tab_in)

    n_p = bsz * seq
    bc = n_p // CHUNK
    h0_p = jnp.zeros((groups, 2 * SSM_STATE, bc), f32)
    x1_p, hfin_p = _layer_a(x_prompt.reshape(n_p, d), CHUNK, nc, h0_p, mkp[0], mvp[0], wa, tables, lre, lim, True)
    k_p, v_p, kvb_p, q_p, pr_p = _kvb(x1_p, row(g_kv), row(g_pre_b[0]), w_kv_b, w_b, mix)
    o_p = _sb_attn(q_p, kvb_p, bsz, seq, heads)
    rows_b = ATTN_BLOCK
    per_b = seq // rows_b
    tile = lambda w: pl.BlockSpec((rows_b, w), lambda i: (i, 0))
    y_p = _post(o_p, pr_p, x1_p, mkp[1], mvp[1], wa["w_glu"], wa["b_glu"], w_out_bb, row(g_post_b[0]),
                glu=False, segs=(((0, rows_b),),), grid=(n_p // rows_b,), rows=rows_b,
                y_spec=tile(mix), p_spec=tile(pr_p.shape[-1]), x_spec=tile(d),
                mem_spec=pl.BlockSpec((1, n_mem, memw), lambda i: (i // per_b, 0, 0)))

    n_s = dbsz * dseq
    st = state_ssm[0].astype(f32)
    h0_s = jnp.transpose(jnp.concatenate([st[..., 0], st[..., 1]], axis=-1), (1, 2, 0))
    h0_s = jnp.pad(h0_s, ((0, 0), (0, 0), (0, LANES - dbsz)))
    x1_s, hfin_s = _layer_a(x_sample.reshape(n_s, d), dseq, 1, h0_s, mks[0], mvs[0], wa, tables, lre, lim, False)
    k_s, v_s, kvb_s, q_s, pr_s = _kvb(x1_s, row(g_kv), row(g_pre_b[0]), w_kv_b, w_b, mix)
    past = cache_k.shape[1]
    o_s = _sb_attn_sample(q_s, kvb_s, cache_k.reshape(dbsz, past, mix), cache_v.reshape(dbsz, past, mix),
                          dbsz, dseq, heads)
    whole = lambda w: pl.BlockSpec((n_s, w), lambda i: (0, 0))
    y_s = _post(o_s, pr_s, x1_s, mks[1], mvs[1], wa["w_glu"], wa["b_glu"], w_out_bb, row(g_post_b[0]),
                glu=False, segs=tuple(((b * dseq, dseq),) for b in range(dbsz)), grid=(1,), rows=n_s,
                y_spec=whole(mix), p_spec=whole(pr_s.shape[-1]), x_spec=whole(d),
                mem_spec=_resident(mks[1].shape))

    def ssm_out(hfin, lanes_idx):
        h = hfin[:, :, lanes_idx]
        h = jnp.transpose(h, (2, 0, 1))
        return jnp.stack([h[..., :SSM_STATE], h[..., SSM_STATE:]], axis=-1)[None]

    ssm_prompt = ssm_out(hfin_p, jnp.arange(bsz) * nc + (nc - 1)).astype(x_prompt.dtype)
    ssm_sample = ssm_out(hfin_s, jnp.arange(dbsz)).astype(state_ssm.dtype)
    shp = (bsz, seq, heads, SB_HEAD_DIM)
    shs = (dbsz, dseq, heads, SB_HEAD_DIM)
    return (y_p.reshape(bsz, seq, d), y_s.reshape(dbsz, dseq, d),
            k_p.reshape(shp), v_p.reshape(shp), k_s.reshape(shs), v_s.reshape(shs),
            ssm_prompt, ssm_sample, mem_k_prompt, mem_v_prompt)
```

```python
import functools
import math

import jax
import jax.numpy as jnp
from jax import lax
from jax.experimental import pallas as pl
from jax.experimental.pallas import tpu as pltpu

EPS = 1e-6
CHUNK = 64
SSM_GROUP = 16
SSM_STATE = 64
SB_HEAD_DIM = 128
MEM_HEADS = 4
SUB = 8
TABLE_T = 64
LANES = 128
ATTN_BLOCK = 256
MIB = 1024 * 1024

bf16 = jnp.bfloat16
f32 = jnp.float32

_NT = (((1,), (1,)), ((), ()))


def _params(vmem_mib, semantics):
    return pltpu.CompilerParams(vmem_limit_bytes=vmem_mib * MIB, dimension_semantics=semantics)


def _resident(shape):
    zeros = (0,) * len(shape)
    return pl.BlockSpec(shape, lambda *_: zeros, pipeline_mode=pl.Buffered(1))


def _rms_scale(x):
    return x * lax.rsqrt(jnp.mean(x * x, axis=-1, keepdims=True) + EPS)


def _sigmoid(x):
    return 1.0 / (1.0 + jnp.exp(-x))


def _gelu_tanh(x):
    c = math.sqrt(2.0 / math.pi)
    return 0.5 * x * (1.0 + jnp.tanh(c * (x + 0.044715 * (x * x * x))))


def _memkv_kernel(x_ref, w_ref, of_ref, ob_ref):
    acc = jnp.dot(x_ref[...].astype(bf16), w_ref[...], preferred_element_type=f32)
    width = of_ref.shape[-1]
    for j in range(of_ref.shape[0]):
        blk = acc[:, j * width:(j + 1) * width]
        of_ref[j] = blk
        ob_ref[j] = blk.astype(bf16)


def _memkv(mem, w_cat, width):
    rows, d = mem.shape
    nout = w_cat.shape[1] // width
    tm = 256
    return pl.pallas_call(
        _memkv_kernel,
        grid=(rows // tm,),
        in_specs=[pl.BlockSpec((tm, d), lambda i: (i, 0)), _resident(w_cat.shape)],
        out_specs=[pl.BlockSpec((nout, tm, width), lambda i: (0, i, 0)),
                   pl.BlockSpec((nout, tm, width), lambda i: (0, i, 0))],
        out_shape=[jax.ShapeDtypeStruct((nout, rows, width), f32),
                   jax.ShapeDtypeStruct((nout, rows, width), bf16)],
        compiler_params=_params(40, ("parallel",)),
        name="memkv",
    )(mem, w_cat)


def _inproj_a_kernel(x_ref, g_ref, wut_ref, wr_ref, ut_ref, p_ref, hn_ref):
    ns, bc, _ = x_ref.shape
    for s in range(ns):
        hn = (_rms_scale(x_ref[s]) * g_ref[...]).astype(bf16)
        hn_ref[s * bc:(s + 1) * bc, :] = hn
        ut = lax.dot_general(wut_ref[...], hn, _NT, preferred_element_type=f32)
        ut_ref[s] = ut.reshape(ut_ref.shape[1:]).astype(bf16)
    step = 512
    for c in range(0, wr_ref.shape[1], step):
        r = jnp.dot(hn_ref[...], wr_ref[:, c:c + step], preferred_element_type=f32)
        p_ref[:, :, c:c + step] = r.astype(bf16).reshape(ns, bc, step)


def _inproj_a(xt, g, w_ut, w_rest, ns):
    t, bc, d = xt.shape
    mix = w_ut.shape[0]
    rest = w_rest.shape[1]
    groups = mix // SSM_GROUP
    return pl.pallas_call(
        _inproj_a_kernel,
        grid=(t // ns,),
        in_specs=[pl.BlockSpec((ns, bc, d), lambda i: (i, 0, 0)),
                  _resident(g.shape), _resident(w_ut.shape), _resident(w_rest.shape)],
        out_specs=[pl.BlockSpec((ns, groups, SSM_GROUP, bc), lambda i: (i, 0, 0, 0)),
                   pl.BlockSpec((ns, bc, rest), lambda i: (i, 0, 0))],
        out_shape=[jax.ShapeDtypeStruct((t, groups, SSM_GROUP, bc), bf16),
                   jax.ShapeDtypeStruct((t, bc, rest), bf16)],
        scratch_shapes=[pltpu.VMEM((ns * bc, d), bf16)],
        compiler_params=_params(52, ("parallel",)),
        name="inproj_a",
    )(xt, g, w_ut, w_rest)


def _inproj_plain_kernel(x_ref, g_ref, w_ref, u_ref, p_ref):
    hn = (_rms_scale(x_ref[...]) * g_ref[...]).astype(bf16)
    mix = u_ref.shape[1]
    u_ref[...] = jnp.dot(hn, w_ref[:, :mix], preferred_element_type=f32).astype(bf16)
    p_ref[...] = jnp.dot(hn, w_ref[:, mix:], preferred_element_type=f32).astype(bf16)


def _inproj_plain(x, g, w, mix):
    rows, d = x.shape
    rest = w.shape[1] - mix
    return pl.pallas_call(
        _inproj_plain_kernel,
        grid=(1,),
        in_specs=[pl.BlockSpec((rows, d), lambda i: (0, 0)), _resident(g.shape), _resident(w.shape)],
        out_specs=[pl.BlockSpec((rows, mix), lambda i: (0, 0)), pl.BlockSpec((rows, rest), lambda i: (0, 0))],
        out_shape=[jax.ShapeDtypeStruct((rows, mix), bf16), jax.ShapeDtypeStruct((rows, rest), bf16)],
        compiler_params=_params(40, ("arbitrary",)),
        name="inproj_plain",
    )(x, g, w)


def _s5_tables_kernel(lre_ref, lim_ref, caa_ref, cab_ref, ba_ref, bb_ref, dd_ref,
                      tt_ref, ca_ref, wb_ref, pw_ref, cas_ref, wbs_ref):
    t_len = TABLE_T
    rows = t_len * SSM_GROUP
    lre = lre_ref[...]
    lim = lim_ref[...]
    kk = lax.broadcasted_iota(jnp.int32, (t_len, LANES), 0).astype(f32)

    def powers(k):
        mag = jnp.exp(lre * k)
        th = lim * k
        return mag * jnp.cos(th), mag * jnp.sin(th)

    pr1, pi1 = powers(kk + 1.0)
    pr0, pi0 = powers((t_len - 1.0) - kk)
    pw_ref[...] = jnp.concatenate([pr1, pi1, pr0, pi0], axis=1)

    def tile_rows(v, n):
        return jnp.broadcast_to(v[None], (n,) + v.shape).reshape(n * v.shape[0], v.shape[1])

    caa = caa_ref[...]
    cab = cab_ref[...]
    ba = ba_ref[...]
    bb = bb_ref[...]
    for t in range(t_len):
        pw = jnp.broadcast_to(pw_ref[t:t + 1, :], (SSM_GROUP, 4 * LANES))
        r0 = t * SSM_GROUP
        cas_ref[r0:r0 + SSM_GROUP, :] = pw[:, 0:128] * caa + pw[:, 128:256] * cab
        wbs_ref[r0:r0 + SSM_GROUP, :] = pw[:, 256:384] * ba + pw[:, 384:512] * bb
    ca = cas_ref[...]
    wbt = wbs_ref[...]
    ca_ref[...] = ca.astype(bf16)
    wb_ref[...] = wbt.T.astype(bf16)

    blk = SUB * SSM_GROUP
    nblk = rows // blk
    rt = wbt[rows - blk:, :]
    m = [None] * nblk
    for d in range(1, nblk):
        m[d] = lax.dot_general(ca[(d - 1) * blk:d * blk, :], rt, _NT, preferred_element_type=f32,
                               precision=lax.Precision.HIGHEST)
    ca0 = jnp.concatenate([caa, ca[:blk - SSM_GROUP, :]], axis=0)
    kj = lax.dot_general(ca0, tile_rows(ba, SUB), _NT, preferred_element_type=f32,
                         precision=lax.Precision.HIGHEST)
    lane = lax.broadcasted_iota(jnp.int32, (SSM_GROUP, LANES), 1)
    hrow = lax.broadcasted_iota(jnp.int32, (SSM_GROUP, LANES), 0)
    s0_lane = lane // SSM_GROUP
    skip = jnp.where(lane % SSM_GROUP == hrow, dd_ref[...], 0.0)
    kjs = [kj[j * SSM_GROUP:(j + 1) * SSM_GROUP, :] for j in range(SUB)]
    kjs[0] = kjs[0] + skip
    drows = []
    for t0 in range(SUB):
        acc = jnp.zeros((SSM_GROUP, LANES), f32)
        for j in range(t0 + 1):
            acc = acc + jnp.where(s0_lane == t0 - j, kjs[j], 0.0)
        drows.append(acc)
    m[0] = jnp.concatenate(drows, axis=0)
    zero = jnp.zeros((blk, blk), f32)
    r1 = jnp.concatenate([m[d] for d in range(nblk - 1, -1, -1)], axis=1)
    r0 = jnp.concatenate([m[d] for d in range(nblk - 2, -1, -1)] + [zero], axis=1)
    tt_ref[...] = jnp.concatenate([r0, r1], axis=0).astype(bf16)


def _s5_tables(lre2, lim2, caa, cab, ba, bb, dd):
    groups = lre2.shape[0]
    rows = TABLE_T * SSM_GROUP

    def gspec(shape):
        return pl.BlockSpec((None,) + shape, lambda g: (g,) + (0,) * len(shape))

    return pl.pallas_call(
        _s5_tables_kernel,
        grid=(groups,),
        in_specs=[gspec((1, LANES)), gspec((1, LANES)), gspec((SSM_GROUP, LANES)), gspec((SSM_GROUP, LANES)),
                  gspec((SSM_GROUP, LANES)), gspec((SSM_GROUP, LANES)), gspec((SSM_GROUP, 1))],
        out_specs=[gspec((2 * SUB * SSM_GROUP, rows)), gspec((rows, LANES)), gspec((LANES, rows))],
        out_shape=[jax.ShapeDtypeStruct((groups, 2 * SUB * SSM_GROUP, rows), bf16),
                   jax.ShapeDtypeStruct((groups, rows, LANES), bf16),
                   jax.ShapeDtypeStruct((groups, LANES, rows), bf16)],
        scratch_shapes=[pltpu.VMEM((TABLE_T, 4 * LANES), f32), pltpu.VMEM((rows, LANES), f32),
                        pltpu.VMEM((rows, LANES), f32)],
        compiler_params=_params(40, ("parallel",)),
        name="s5_tables",
    )(lre2, lim2, caa, cab, ba, bb, dd)


def _cmul(ar, ai, x):
    half = x.shape[0] // 2
    xr = x[:half]
    xi = x[half:]
    return jnp.concatenate([ar * xr - ai * xi, ar * xi + ai * xr], axis=0)


def _s5_kernel(ut_ref, tt_ref, ca_ref, wb_ref, ap_ref, h0_ref, y_ref, hfin_ref, yt_ref, *, nc):
    g8 = pl.program_id(1)
    t_len, _, bc = ut_ref.shape
    rows = t_len * SSM_GROUP
    pair = 2 * SUB * SSM_GROUP
    table_rows = tt_ref.shape[1]
    z = ut_ref[...].reshape(rows, bc)
    ys = []
    for t2 in range(rows // pair):
        kk = pair * (t2 + 1)
        ys.append(jnp.dot(tt_ref[:, table_rows - kk:], z[:kk], preferred_element_type=f32))
    y = jnp.concatenate(ys, axis=0) if len(ys) > 1 else ys[0]
    state = jnp.dot(wb_ref[:, table_rows - rows:], z, preferred_element_type=f32)
    ap = ap_ref[...]
    h0 = h0_ref[...]
    state = state + _cmul(ap[:, 0:1], ap[:, 1:2], h0)
    lane = lax.broadcasted_iota(jnp.int32, state.shape, 1) % nc
    step = 0
    while (1 << step) < nc:
        sh = 1 << step
        shifted = jnp.where(lane >= sh, pltpu.roll(state, sh, axis=1), 0.0)
        state = state + _cmul(ap[:, 2 + 2 * step:3 + 2 * step], ap[:, 3 + 2 * step:4 + 2 * step], shifted)
        step += 1
    hfin_ref[...] = state
    if nc > 1:
        h_in = jnp.where(lane >= 1, pltpu.roll(state, 1, axis=1), 0.0) + h0
    else:
        h_in = h0
    y = y + jnp.dot(ca_ref[:rows, :], h_in.astype(bf16), preferred_element_type=f32)
    off = pl.multiple_of(g8 * SSM_GROUP, SSM_GROUP)
    yt_ref[:, pl.ds(off, SSM_GROUP), :] = y.reshape(t_len, SSM_GROUP, bc)

    @pl.when(g8 == pl.num_programs(1) - 1)
    def _():
        for t in range(t_len):
            y_ref[t] = yt_ref[t].T.astype(bf16)


def _s5(ut4, tt, ca, wb, apow, h0, nc):
    t_len, groups, _, bc = ut4.shape
    per = LANES // SSM_GROUP
    rows = TABLE_T * SSM_GROUP
    mix = groups * SSM_GROUP
    ncol = apow.shape[-1]
    return pl.pallas_call(
        functools.partial(_s5_kernel, nc=nc),
        grid=(groups // per, per),
        in_specs=[pl.BlockSpec((t_len, None, SSM_GROUP, bc), lambda G, g: (0, G * per + g, 0, 0)),
                  pl.BlockSpec((None, 2 * SUB * SSM_GROUP, rows), lambda G, g: (G * per + g, 0, 0)),
                  pl.BlockSpec((None, rows, LANES), lambda G, g: (G * per + g, 0, 0)),
                  pl.BlockSpec((None, LANES, rows), lambda G, g: (G * per + g, 0, 0)),
                  pl.BlockSpec((None, SSM_STATE, ncol), lambda G, g: (G * per + g, 0, 0)),
                  pl.BlockSpec((None, 2 * SSM_STATE, bc), lambda G, g: (G * per + g, 0, 0))],
        out_specs=[pl.BlockSpec((t_len, bc, LANES), lambda G, g: (0, 0, G)),
                   pl.BlockSpec((None, 2 * SSM_STATE, bc), lambda G, g: (G * per + g, 0, 0))],
        out_shape=[jax.ShapeDtypeStruct((t_len, bc, mix), bf16),
                   jax.ShapeDtypeStruct((groups, 2 * SSM_STATE, bc), f32)],
        scratch_shapes=[pltpu.VMEM((t_len, LANES, bc), f32)],
        compiler_params=_params(48, ("parallel", "arbitrary")),
        name="s5",
    )(ut4, tt, ca, wb, apow, h0)


def _ld(ref, start, n, c0, c1):
    if len(ref.shape) == 3:
        s, r = divmod(start, ref.shape[1])
        return ref[s, r:r + n, c0:c1]
    return ref[start:start + n, c0:c1]


def _post_kernel(y_ref, p_ref, x_ref, mk_ref, mv_ref, wglu_ref, bglu_ref, wout_ref, gpost_ref,
                 o_ref, cat_ref, *, glu, segs, mem_scale):
    mix = y_ref.shape[-1]
    memw = mk_ref.shape[-1]
    hd = memw // MEM_HEADS
    rows = cat_ref.shape[0]
    y = y_ref[...].reshape(rows, mix).astype(f32)
    gate = p_ref[:, :mix] if len(p_ref.shape) == 2 else p_ref[:, :, :mix].reshape(rows, mix)
    gate = gate.astype(f32)
    if glu:
        y = _gelu_tanh(y)
        zz = jnp.dot(y.astype(bf16), wglu_ref[...], preferred_element_type=f32) + bglu_ref[...]
        y = y * _sigmoid(zz)
    cat_ref[:, :mix] = (y * (gate * _sigmoid(gate))).astype(bf16)
    for b, pieces in enumerate(segs):
        for h in range(MEM_HEADS):
            cq = mix + h * hd
            cg = mix + memw + h * hd
            q = jnp.concatenate([_ld(p_ref, st, n, cq, cq + hd) for st, n in pieces], axis=0)
            mg = jnp.concatenate([_ld(p_ref, st, n, cg, cg + hd) for st, n in pieces], axis=0).astype(f32)
            k = mk_ref[b, :, h * hd:(h + 1) * hd]
            v = mv_ref[b, :, h * hd:(h + 1) * hd]
            s = lax.dot_general(q, k, _NT, preferred_element_type=f32) * mem_scale
            e = jnp.exp(s - jnp.max(s, axis=-1, keepdims=True))
            p = e / jnp.sum(e, axis=-1, keepdims=True)
            o = jnp.dot(p.astype(bf16), v, preferred_element_type=f32)
            om = (o * (mg * _sigmoid(mg))).astype(bf16)
            off = 0
            for st, n in pieces:
                cat_ref[st:st + n, cq:cq + hd] = om[off:off + n]
                off += n
    out = jnp.dot(cat_ref[...], wout_ref[...], preferred_element_type=f32)
    d = out.shape[-1]
    o_ref[...] = (x_ref[...].reshape(rows, d) + _rms_scale(out) * gpost_ref[...]).reshape(o_ref.shape)


def _post(y, p, x, mk, mv, wglu, bglu, wout, gpost, *, glu, segs, grid, y_spec, p_spec, x_spec, mem_spec, rows):
    mem_scale = 1.0 / math.sqrt(mk.shape[-1] // MEM_HEADS)
    return pl.pallas_call(
        functools.partial(_post_kernel, glu=glu, segs=segs, mem_scale=mem_scale),
        grid=grid,
        in_specs=[y_spec, p_spec, x_spec, mem_spec, mem_spec,
                  _resident(wglu.shape), _resident(bglu.shape), _resident(wout.shape), _resident(gpost.shape)],
        out_specs=x_spec,
        out_shape=jax.ShapeDtypeStruct(x.shape, f32),
        scratch_shapes=[pltpu.VMEM((rows, wout.shape[0]), bf16)],
        compiler_params=_params(48, ("parallel",)),
        name="post_glu" if glu else "post",
    )(y, p, x, mk, mv, wglu, bglu, wout, gpost)


def _kvb_kernel(x_ref, gkv_ref, gb_ref, wkv_ref, wb_ref, k_ref, v_ref, kvb_ref, q_ref, pr_ref):
    xs = _rms_scale(x_ref[...])
    hkv = (xs * gkv_ref[...]).astype(bf16)
    hb = (xs * gb_ref[...]).astype(bf16)
    mix = k_ref.shape[1]
    step = 512
    for c in range(0, wkv_ref.shape[1], step):
        r = jnp.dot(hkv, wkv_ref[:, c:c + step], preferred_element_type=f32)
        kvb_ref[:, c:c + step] = r.astype(bf16)
        if c < mix:
            k_ref[:, c:c + step] = r
        else:
            v_ref[:, c - mix:c - mix + step] = r
    for c in range(0, wb_ref.shape[1], step):
        r = jnp.dot(hb, wb_ref[:, c:c + step], preferred_element_type=f32).astype(bf16)
        if c < mix:
            q_ref[:, c:c + step] = r
        else:
            pr_ref[:, c - mix:c - mix + step] = r


def _kvb(x, gkv, gb, wkv, wb, mix):
    n, d = x.shape
    tm = min(256, n)
    rest = wb.shape[1] - mix
    row = lambda w: pl.BlockSpec((tm, w), lambda i: (i, 0))
    return pl.pallas_call(
        _kvb_kernel,
        grid=(n // tm,),
        in_specs=[row(d), _resident(gkv.shape), _resident(gb.shape), _resident(wkv.shape), _resident(wb.shape)],
        out_specs=[row(mix), row(mix), row(2 * mix), row(mix), row(rest)],
        out_shape=[jax.ShapeDtypeStruct((n, mix), f32), jax.ShapeDtypeStruct((n, mix), f32),
                   jax.ShapeDtypeStruct((n, 2 * mix), bf16), jax.ShapeDtypeStruct((n, mix), bf16),
                   jax.ShapeDtypeStruct((n, rest), bf16)],
        compiler_params=_params(56, ("parallel",)),
        name="kvb",
    )(x, gkv, gb, wkv, wb)


def _sb_block(q, k, v, ntri, carry, acc, mask):
    z = lax.dot_general(q, k, _NT, preferred_element_type=f32)
    sp = jnp.maximum(z, 0.0) + jnp.log2(1.0 + jnp.exp2(-jnp.abs(z)))
    if mask is not None:
        sp = jnp.where(mask, sp, 0.0)
    inner = jnp.dot(sp.astype(bf16), ntri, preferred_element_type=f32)
    w = jnp.exp2((z - sp) + (inner + carry))
    if mask is not None:
        w = jnp.where(mask, w, 0.0)
    acc = acc + jnp.dot(w.astype(bf16), v, preferred_element_type=f32)
    carry = carry + (inner[:, 0:1] - sp[:, 0:1])
    return acc, carry


def _tri_and_mask(n):
    r = lax.broadcasted_iota(jnp.int32, (n, n), 0)
    c = lax.broadcasted_iota(jnp.int32, (n, n), 1)
    return jnp.where(r > c, -1.0, 0.0).astype(bf16), c < r


def _sb_attn_kernel(q_ref, k_ref, v_ref, o_ref, acc_ref, car_ref, *, nq):
    tq = ATTN_BLOCK
    i0 = pl.program_id(2) * nq
    ntri, causal = _tri_and_mask(tq)

    def run(r, off, mask, first):
        q = q_ref[r * tq:(r + 1) * tq, :]
        k = k_ref[pl.ds(off, tq), :]
        v = v_ref[pl.ds(off, tq), :]
        acc = jnp.zeros((tq, q.shape[1]), f32) if first else acc_ref[r]
        carry = jnp.zeros((tq, 1), f32) if first else car_ref[r]
        acc, carry = _sb_block(q, k, v, ntri, carry, acc, mask)
        acc_ref[r] = acc
        car_ref[r] = carry

    for r in range(nq):
        run(r, pl.multiple_of((i0 + r) * tq, tq), causal, True)

    def body(jj, c):
        for r in range(nq):
            run(r, pl.multiple_of((i0 + r - jj) * tq, tq), None, False)
        return c

    lax.fori_loop(1, i0 + 1, body, 0)
    for e in range(1, nq):
        for r in range(e, nq):
            run(r, (r - e) * tq, None, False)
    for r in range(nq):
        o_ref[r * tq:(r + 1) * tq, :] = acc_ref[r].astype(bf16)


def _sb_attn(q, kvb, bsz, seq, heads):
    tq = ATTN_BLOCK
    hd = SB_HEAD_DIM
    nq = 4 if (seq // tq) % 4 == 0 else 1
    steps = seq // (tq * nq)
    return pl.pallas_call(
        functools.partial(_sb_attn_kernel, nq=nq),
        grid=(bsz, heads, steps),
        in_specs=[pl.BlockSpec((nq * tq, hd), lambda b, h, i: (b * steps + i, h)),
                  pl.BlockSpec((seq, hd), lambda b, h, i: (b, h)),
                  pl.BlockSpec((seq, hd), lambda b, h, i: (b, heads + h))],
        out_specs=pl.BlockSpec((nq * tq, hd), lambda b, h, i: (b * steps + i, h)),
        out_shape=jax.ShapeDtypeStruct(q.shape, bf16),
        scratch_shapes=[pltpu.VMEM((nq, tq, hd), f32), pltpu.VMEM((nq, tq, 1), f32)],
        compiler_params=_params(40, ("parallel", "parallel", "arbitrary")),
        name="sb_attn",
    )(q, kvb, kvb)


def _sb_attn_sample_kernel(q_ref, kn_ref, vn_ref, kc_ref, vc_ref, o_ref):
    tq = q_ref.shape[0]
    past = kc_ref.shape[0]
    blk = min(ATTN_BLOCK, past)
    q = q_ref[...]
    tri_n, causal = _tri_and_mask(tq)
    acc = jnp.zeros((tq, q_ref.shape[1]), f32)
    carry = jnp.zeros((tq, 1), f32)
    acc, carry = _sb_block(q, kn_ref[...], vn_ref[...], tri_n, carry, acc, causal)
    tri_p, _ = _tri_and_mask(blk)
    for j in range(past // blk - 1, -1, -1):
        kb = kc_ref[j * blk:(j + 1) * blk, :].astype(bf16)
        vb = vc_ref[j * blk:(j + 1) * blk, :].astype(bf16)
        acc, carry = _sb_block(q, kb, vb, tri_p, carry, acc, None)
    o_ref[...] = acc.astype(bf16)


def _sb_attn_sample(q, kvb, cache_k3, cache_v3, bsz, tq, heads):
    hd = SB_HEAD_DIM
    past = cache_k3.shape[1]
    return pl.pallas_call(
        _sb_attn_sample_kernel,
        grid=(bsz, heads),
        in_specs=[pl.BlockSpec((tq, hd), lambda b, h: (b, h)),
                  pl.BlockSpec((tq, hd), lambda b, h: (b, h)),
                  pl.BlockSpec((tq, hd), lambda b, h: (b, heads + h)),
                  pl.BlockSpec((None, past, hd), lambda b, h: (b, 0, h)),
                  pl.BlockSpec((None, past, hd), lambda b, h: (b, 0, h))],
        out_specs=pl.BlockSpec((tq, hd), lambda b, h: (b, h)),
        out_shape=jax.ShapeDtypeStruct(q.shape, bf16),
        compiler_params=_params(40, ("parallel", "parallel")),
        name="sb_attn_sample",
    )(q, kvb, kvb, cache_k3, cache_v3)


def _ssm_param_tables(lam_re, lam_im, log_dt, b_re, b_im, c_re, c_im, dvec):
    dt = jnp.exp(log_dt.astype(f32))[:, None]
    lr = lam_re.astype(f32)
    li = lam_im.astype(f32)
    lre = lr * dt
    lim = li * dt
    mag = jnp.exp(lre)
    nr = mag * jnp.cos(lim) - 1.0
    ni = mag * jnp.sin(lim)
    den = lr * lr + li * li
    fr = ((nr * lr + ni * li) / den)[..., None]
    fi = ((ni * lr - nr * li) / den)[..., None]
    b_r = b_re.astype(f32)
    b_i = b_im.astype(f32)
    br = jnp.swapaxes(fr * b_r - fi * b_i, 1, 2)
    bi = jnp.swapaxes(fr * b_i + fi * b_r, 1, 2)
    cr = c_re.astype(f32)
    ci = c_im.astype(f32)
    dup = lambda v: jnp.concatenate([v, v], axis=-1)
    lre2 = dup(lre)[:, None, :]
    lim2 = dup(lim)[:, None, :]
    caa = jnp.concatenate([cr, -ci], axis=-1)
    cab = jnp.concatenate([-ci, -cr], axis=-1)
    ba = jnp.concatenate([br, bi], axis=-1)
    bb = jnp.concatenate([-bi, br], axis=-1)
    dd = dvec.astype(f32)[:, :, None]
    return lre, lim, (lre2, lim2, caa, cab, ba, bb, dd)


def _chunk_powers(lre, lim, t_len, nc):
    cols = []
    exps = [t_len]
    j = 0
    while (1 << j) < nc:
        exps.append(t_len * (1 << j))
        j += 1
    for e in exps:
        mag = jnp.exp(lre * e)
        cols += [mag * jnp.cos(lim * e), mag * jnp.sin(lim * e)]
    return jnp.stack(cols, axis=-1)


def _layer_a(x, t_len, nc, h0_lanes, mk, mv, wa, tables, lre, lim, prompt):
    n, d = x.shape
    mix = wa["w_glu"].shape[0]
    groups = mix // SSM_GROUP
    tt, ca, wb = tables
    apow = _chunk_powers(lre, lim, t_len, nc)
    if prompt:
        bc = n // t_len
        ns = 2
        xt = jnp.transpose(x.reshape(bc, t_len, d), (1, 0, 2))
        ut4, p = _inproj_a(xt, wa["g_pre"], wa["w_ut"], wa["w_rest"], ns)
        y, hfin = _s5(ut4, tt, ca, wb, apow, h0_lanes, nc)
        streams = bc // nc
        rest = p.shape[-1]
        segs = tuple(tuple((s * bc + b * nc, nc) for s in range(ns)) for b in range(streams))
        blk = lambda w: pl.BlockSpec((ns, bc, w), lambda i: (i, 0, 0))
        x1t = _post(y, p, xt, mk, mv, wa["w_glu"], wa["b_glu"], wa["w_out"], wa["g_post"],
                    glu=True, segs=segs, grid=(t_len // ns,), rows=ns * bc,
                    y_spec=blk(mix), p_spec=blk(rest), x_spec=blk(d), mem_spec=_resident(mk.shape))
        return jnp.transpose(x1t, (1, 0, 2)).reshape(n, d), hfin
    streams = n // t_len
    lanes = h0_lanes.shape[-1]
    u, p = _inproj_plain(x, wa["g_pre"], wa["w_in"], mix)
    ut4 = jnp.transpose(u.reshape(streams, t_len, groups, SSM_GROUP), (1, 2, 3, 0))
    ut4 = jnp.pad(ut4, ((0, 0), (0, 0), (0, 0), (0, lanes - streams)))
    y3, hfin = _s5(ut4, tt, ca, wb, apow, h0_lanes, nc)
    y = jnp.transpose(y3[:, :streams, :], (1, 0, 2)).reshape(n, mix)
    rest = p.shape[-1]
    whole = lambda w: pl.BlockSpec((n, w), lambda i: (0, 0))
    x1 = _post(y, p, x, mk, mv, wa["w_glu"], wa["b_glu"], wa["w_out"], wa["g_post"],
               glu=True, segs=tuple(((b * t_len, t_len),) for b in range(streams)), grid=(1,), rows=n,
               y_spec=whole(mix), p_spec=whole(rest), x_spec=whole(d), mem_spec=_resident(mk.shape))
    return x1, hfin


def kernel(x_prompt, x_sample, cache_k, cache_v, cache_mem_k, cache_mem_v, state_ssm, mem_prompt, w_in_a, w_out_a, g_pre_a, g_post_a, ssm_lam_re, ssm_lam_im, ssm_log_dt, ssm_b_re, ssm_b_im, ssm_c_re, ssm_c_im, ssm_d, w_glu, b_glu, g_kv, w_kv, w_in_b, w_out_b, g_pre_b, g_post_b, w_mem_k, w_mem_v):
    bsz, seq, d = x_prompt.shape
    dbsz, dseq, _ = x_sample.shape
    mix = w_glu.shape[-1]
    memw = w_mem_k.shape[-1]
    heads = mix // SB_HEAD_DIM
    groups = mix // SSM_GROUP
    n_mem = mem_prompt.shape[1]
    depth = w_mem_k.shape[0]
    assert depth == 2 and w_in_a.shape[0] == 1 and w_in_b.shape[0] == 1
    assert seq % CHUNK == 0 and (bsz * seq // CHUNK) % LANES == 0 and seq % ATTN_BLOCK == 0
    assert dseq % (2 * SUB) == 0 and dseq <= TABLE_T and dbsz <= LANES
    nc = seq // CHUNK
    assert nc & (nc - 1) == 0

    row = lambda v: v.astype(f32).reshape(1, -1)
    wa = dict(
        w_in=w_in_a[0].astype(bf16),
        w_ut=w_in_a[0][:, :mix].T.astype(bf16),
        w_rest=w_in_a[0][:, mix:].astype(bf16),
        w_glu=w_glu[0].astype(bf16), b_glu=row(b_glu[0]),
        w_out=w_out_a[0].astype(bf16), g_pre=row(g_pre_a[0]), g_post=row(g_post_a[0]))
    qscale = math.log2(math.e) / math.sqrt(SB_HEAD_DIM)
    w_b = jnp.concatenate([w_in_b[0][:, :mix] * qscale, w_in_b[0][:, mix:]], axis=1).astype(bf16)
    w_kv_b = w_kv.astype(bf16)
    w_out_bb = w_out_b[0].astype(bf16)
    w_mem = jnp.concatenate([w_mem_k[0], w_mem_k[1], w_mem_v[0], w_mem_v[1]], axis=1).astype(bf16)

    memf, memb = _memkv(mem_prompt.reshape(bsz * n_mem, d), w_mem, memw)
    mem_k_prompt = memf[:depth].reshape(depth, bsz, n_mem, MEM_HEADS, memw // MEM_HEADS)
    mem_v_prompt = memf[depth:].reshape(depth, bsz, n_mem, MEM_HEADS, memw // MEM_HEADS)
    mkp = memb[:depth].reshape(depth, bsz, n_mem, memw)
    mvp = memb[depth:].reshape(depth, bsz, n_mem, memw)
    mks = cache_mem_k.reshape(depth, dbsz, n_mem, memw).astype(bf16)
    mvs = cache_mem_v.reshape(depth, dbsz, n_mem, memw).astype(bf16)

    lre, lim, tab_in = _ssm_param_tables(ssm_lam_re[0], ssm_lam_im[0], ssm_log_dt[0], ssm_b_re[0], ssm_b_im[0],
                                         ssm_c_re[0], ssm_c_im[0], ssm_d[0])
    tables = _s5_tables(*tab_in)

    n_p = bsz * seq
    bc = n_p // CHUNK
    h0_p = jnp.zeros((groups, 2 * SSM_STATE, bc), f32)
    x1_p, hfin_p = _layer_a(x_prompt.reshape(n_p, d), CHUNK, nc, h0_p, mkp[0], mvp[0], wa, tables, lre, lim, True)
    k_p, v_p, kvb_p, q_p, pr_p = _kvb(x1_p, row(g_kv), row(g_pre_b[0]), w_kv_b, w_b, mix)
    o_p = _sb_attn(q_p, kvb_p, bsz, seq, heads)
    rows_b = ATTN_BLOCK
    per_b = seq // rows_b
    tile = lambda w: pl.BlockSpec((rows_b, w), lambda i: (i, 0))
    y_p = _post(o_p, pr_p, x1_p, mkp[1], mvp[1], wa["w_glu"], wa["b_glu"], w_out_bb, row(g_post_b[0]),
                glu=False, segs=(((0, rows_b),),), grid=(n_p // rows_b,), rows=rows_b,
                y_spec=tile(mix), p_spec=tile(pr_p.shape[-1]), x_spec=tile(d),
                mem_spec=pl.BlockSpec((1, n_mem, memw), lambda i: (i // per_b, 0, 0)))

    n_s = dbsz * dseq
    st = state_ssm[0].astype(f32)
    h0_s = jnp.transpose(jnp.concatenate([st[..., 0], st[..., 1]], axis=-1), (1, 2, 0))
    h0_s = jnp.pad(h0_s, ((0, 0), (0, 0), (0, LANES - dbsz)))
    x1_s, hfin_s = _layer_a(x_sample.reshape(n_s, d), dseq, 1, h0_s, mks[0], mvs[0], wa, tables, lre, lim, False)
    k_s, v_s, kvb_s, q_s, pr_s = _kvb(x1_s, row(g_kv), row(g_pre_b[0]), w_kv_b, w_b, mix)
    past = cache_k.shape[1]
    o_s = _sb_attn_sample(q_s, kvb_s, cache_k.reshape(dbsz, past, mix), cache_v.reshape(dbsz, past, mix),
                          dbsz, dseq, heads)
    whole = lambda w: pl.BlockSpec((n_s, w), lambda i: (0, 0))
    y_s = _post(o_s, pr_s, x1_s, mks[1], mvs[1], wa["w_glu"], wa["b_glu"], w_out_bb, row(g_post_b[0]),
                glu=False, segs=tuple(((b * dseq, dseq),) for b in range(dbsz)), grid=(1,), rows=n_s,
                y_spec=whole(mix), p_spec=whole(pr_s.shape[-1]), x_spec=whole(d),
                mem_spec=_resident(mks[1].shape))

    def ssm_out(hfin, lanes_idx):
        h = hfin[:, :, lanes_idx]
        h = jnp.transpose(h, (2, 0, 1))
        return jnp.stack([h[..., :SSM_STATE], h[..., SSM_STATE:]], axis=-1)[None]

    ssm_prompt = ssm_out(hfin_p, jnp.arange(bsz) * nc + (nc - 1)).astype(x_prompt.dtype)
    ssm_sample = ssm_out(hfin_s, jnp.arange(dbsz)).astype(state_ssm.dtype)
    shp = (bsz, seq, heads, SB_HEAD_DIM)
    shs = (dbsz, dseq, heads, SB_HEAD_DIM)
    return (y_p.reshape(bsz, seq, d), y_s.reshape(dbsz, dseq, d),
            k_p.reshape(shp), v_p.reshape(shp), k_s.reshape(shs), v_s.reshape(shs),
            ssm_prompt, ssm_sample, mem_k_prompt, mem_v_prompt)
```

```python
import functools
import math

import jax
import jax.numpy as jnp
from jax import lax
from jax.experimental import pallas as pl
from jax.experimental.pallas import tpu as pltpu

EPS = 1e-6
CHUNK = 64
SSM_GROUP = 16
SSM_STATE = 64
SB_HEAD_DIM = 128
MEM_HEADS = 4
SUB = 8
TABLE_T = 64
LANES = 128
ATTN_BLOCK = 256
MIB = 1024 * 1024

bf16 = jnp.bfloat16
f32 = jnp.float32

_NT = (((1,), (1,)), ((), ()))


def _params(vmem_mib, semantics):
    return pltpu.CompilerParams(vmem_limit_bytes=vmem_mib * MIB, dimension_semantics=semantics)


def _resident(shape):
    zeros = (0,) * len(shape)
    return pl.BlockSpec(shape, lambda *_: zeros, pipeline_mode=pl.Buffered(1))


def _rms_scale(x):
    return x * lax.rsqrt(jnp.mean(x * x, axis=-1, keepdims=True) + EPS)


def _sigmoid(x):
    return 1.0 / (1.0 + jnp.exp(-x))


def _gelu_tanh(x):
    c = math.sqrt(2.0 / math.pi)
    return 0.5 * x * (1.0 + jnp.tanh(c * (x + 0.044715 * (x * x * x))))


def _memkv_kernel(x_ref, w_ref, of_ref, ob_ref):
    acc = jnp.dot(x_ref[...].astype(bf16), w_ref[...], preferred_element_type=f32)
    width = of_ref.shape[-1]
    for j in range(of_ref.shape[0]):
        blk = acc[:, j * width:(j + 1) * width]
        of_ref[j] = blk
        ob_ref[j] = blk.astype(bf16)


def _memkv(mem, w_cat, width):
    rows, d = mem.shape
    nout = w_cat.shape[1] // width
    tm = 256
    return pl.pallas_call(
        _memkv_kernel,
        grid=(rows // tm,),
        in_specs=[pl.BlockSpec((tm, d), lambda i: (i, 0)), _resident(w_cat.shape)],
        out_specs=[pl.BlockSpec((nout, tm, width), lambda i: (0, i, 0)),
                   pl.BlockSpec((nout, tm, width), lambda i: (0, i, 0))],
        out_shape=[jax.ShapeDtypeStruct((nout, rows, width), f32),
                   jax.ShapeDtypeStruct((nout, rows, width), bf16)],
        compiler_params=_params(40, ("parallel",)),
        name="memkv",
    )(mem, w_cat)


def _inproj_a_kernel(x_ref, g_ref, wut_ref, wr_ref, ut_ref, p_ref, hn_ref):
    ns, bc, _ = x_ref.shape
    for s in range(ns):
        hn = (_rms_scale(x_ref[s]) * g_ref[...]).astype(bf16)
        hn_ref[s * bc:(s + 1) * bc, :] = hn
        ut = lax.dot_general(wut_ref[...], hn, _NT, preferred_element_type=f32)
        ut_ref[s] = ut.reshape(ut_ref.shape[1:]).astype(bf16)
    step = 512
    for c in range(0, wr_ref.shape[1], step):
        r = jnp.dot(hn_ref[...], wr_ref[:, c:c + step], preferred_element_type=f32)
        p_ref[:, :, c:c + step] = r.astype(bf16).reshape(ns, bc, step)


def _inproj_a(xt, g, w_ut, w_rest, ns):
    t, bc, d = xt.shape
    mix = w_ut.shape[0]
    rest = w_rest.shape[1]
    groups = mix // SSM_GROUP
    return pl.pallas_call(
        _inproj_a_kernel,
        grid=(t // ns,),
        in_specs=[pl.BlockSpec((ns, bc, d), lambda i: (i, 0, 0)),
                  _resident(g.shape), _resident(w_ut.shape), _resident(w_rest.shape)],
        out_specs=[pl.BlockSpec((ns, groups, SSM_GROUP, bc), lambda i: (i, 0, 0, 0)),
                   pl.BlockSpec((ns, bc, rest), lambda i: (i, 0, 0))],
        out_shape=[jax.ShapeDtypeStruct((t, groups, SSM_GROUP, bc), bf16),
                   jax.ShapeDtypeStruct((t, bc, rest), bf16)],
        scratch_shapes=[pltpu.VMEM((ns * bc, d), bf16)],
        compiler_params=_params(52, ("parallel",)),
        name="inproj_a",
    )(xt, g, w_ut, w_rest)


def _inproj_plain_kernel(x_ref, g_ref, w_ref, u_ref, p_ref):
    hn = (_rms_scale(x_ref[...]) * g_ref[...]).astype(bf16)
    mix = u_ref.shape[1]
    u_ref[...] = jnp.dot(hn, w_ref[:, :mix], preferred_element_type=f32).astype(bf16)
    p_ref[...] = jnp.dot(hn, w_ref[:, mix:], preferred_element_type=f32).astype(bf16)


def _inproj_plain(x, g, w, mix):
    rows, d = x.shape
    rest = w.shape[1] - mix
    return pl.pallas_call(
        _inproj_plain_kernel,
        grid=(1,),
        in_specs=[pl.BlockSpec((rows, d), lambda i: (0, 0)), _resident(g.shape), _resident(w.shape)],
        out_specs=[pl.BlockSpec((rows, mix), lambda i: (0, 0)), pl.BlockSpec((rows, rest), lambda i: (0, 0))],
        out_shape=[jax.ShapeDtypeStruct((rows, mix), bf16), jax.ShapeDtypeStruct((rows, rest), bf16)],
        compiler_params=_params(40, ("arbitrary",)),
        name="inproj_plain",
    )(x, g, w)


def _s5_tables_kernel(lre_ref, lim_ref, caa_ref, cab_ref, ba_ref, bb_ref, dd_ref,
                      tt_ref, ca_ref, wb_ref, pw_ref, cas_ref, wbs_ref):
    t_len = TABLE_T
    rows = t_len * SSM_GROUP
    lre = lre_ref[...]
    lim = lim_ref[...]
    kk = lax.broadcasted_iota(jnp.int32, (t_len, LANES), 0).astype(f32)

    def powers(k):
        mag = jnp.exp(lre * k)
        th = lim * k
        return mag * jnp.cos(th), mag * jnp.sin(th)

    pr1, pi1 = powers(kk + 1.0)
    pr0, pi0 = powers((t_len - 1.0) - kk)
    pw_ref[...] = jnp.concatenate([pr1, pi1, pr0, pi0], axis=1)

    def tile_rows(v, n):
        return jnp.broadcast_to(v[None], (n,) + v.shape).reshape(n * v.shape[0], v.shape[1])

    caa = caa_ref[...]
    cab = cab_ref[...]
    ba = ba_ref[...]
    bb = bb_ref[...]
    for t in range(t_len):
        pw = jnp.broadcast_to(pw_ref[t:t + 1, :], (SSM_GROUP, 4 * LANES))
        r0 = t * SSM_GROUP
        cas_ref[r0:r0 + SSM_GROUP, :] = pw[:, 0:128] * caa + pw[:, 128:256] * cab
        wbs_ref[r0:r0 + SSM_GROUP, :] = pw[:, 256:384] * ba + pw[:, 384:512] * bb
    ca = cas_ref[...]
    wbt = wbs_ref[...]
    ca_ref[...] = ca.astype(bf16)
    wb_ref[...] = wbt.T.astype(bf16)

    blk = SUB * SSM_GROUP
    nblk = rows // blk
    rt = wbt[rows - blk:, :]
    m = [None] * nblk
    for d in range(1, nblk):
        m[d] = lax.dot_general(ca[(d - 1) * blk:d * blk, :], rt, _NT, preferred_element_type=f32,
                               precision=lax.Precision.HIGHEST)
    ca0 = jnp.concatenate([caa, ca[:blk - SSM_GROUP, :]], axis=0)
    kj = lax.dot_general(ca0, tile_rows(ba, SUB), _NT, preferred_element_type=f32,
                         precision=lax.Precision.HIGHEST)
    lane = lax.broadcasted_iota(jnp.int32, (SSM_GROUP, LANES), 1)
    hrow = lax.broadcasted_iota(jnp.int32, (SSM_GROUP, LANES), 0)
    s0_lane = lane // SSM_GROUP
    skip = jnp.where(lane % SSM_GROUP == hrow, dd_ref[...], 0.0)
    kjs = [kj[j * SSM_GROUP:(j + 1) * SSM_GROUP, :] for j in range(SUB)]
    kjs[0] = kjs[0] + skip
    drows = []
    for t0 in range(SUB):
        acc = jnp.zeros((SSM_GROUP, LANES), f32)
        for j in range(t0 + 1):
            acc = acc + jnp.where(s0_lane == t0 - j, kjs[j], 0.0)
        drows.append(acc)
    m[0] = jnp.concatenate(drows, axis=0)
    zero = jnp.zeros((blk, blk), f32)
    r1 = jnp.concatenate([m[d] for d in range(nblk - 1, -1, -1)], axis=1)
    r0 = jnp.concatenate([m[d] for d in range(nblk - 2, -1, -1)] + [zero], axis=1)
    tt_ref[...] = jnp.concatenate([r0, r1], axis=0).astype(bf16)


def _s5_tables(lre2, lim2, caa, cab, ba, bb, dd):
    groups = lre2.shape[0]
    rows = TABLE_T * SSM_GROUP

    def gspec(shape):
        return pl.BlockSpec((None,) + shape, lambda g: (g,) + (0,) * len(shape))

    return pl.pallas_call(
        _s5_tables_kernel,
        grid=(groups,),
        in_specs=[gspec((1, LANES)), gspec((1, LANES)), gspec((SSM_GROUP, LANES)), gspec((SSM_GROUP, LANES)),
                  gspec((SSM_GROUP, LANES)), gspec((SSM_GROUP, LANES)), gspec((SSM_GROUP, 1))],
        out_specs=[gspec((2 * SUB * SSM_GROUP, rows)), gspec((rows, LANES)), gspec((LANES, rows))],
        out_shape=[jax.ShapeDtypeStruct((groups, 2 * SUB * SSM_GROUP, rows), bf16),
                   jax.ShapeDtypeStruct((groups, rows, LANES), bf16),
                   jax.ShapeDtypeStruct((groups, LANES, rows), bf16)],
        scratch_shapes=[pltpu.VMEM((TABLE_T, 4 * LANES), f32), pltpu.VMEM((rows, LANES), f32),
                        pltpu.VMEM((rows, LANES), f32)],
        compiler_params=_params(40, ("parallel",)),
        name="s5_tables",
    )(lre2, lim2, caa, cab, ba, bb, dd)


def _cmul(ar, ai, x):
    half = x.shape[0] // 2
    xr = x[:half]
    xi = x[half:]
    return jnp.concatenate([ar * xr - ai * xi, ar * xi + ai * xr], axis=0)


def _s5_kernel(ut_ref, tt_ref, ca_ref, wb_ref, ap_ref, h0_ref, y_ref, hfin_ref, yt_ref, *, nc):
    g8 = pl.program_id(1)
    t_len, _, bc = ut_ref.shape
    rows = t_len * SSM_GROUP
    pair = 2 * SUB * SSM_GROUP
    table_rows = tt_ref.shape[1]
    z = ut_ref[...].reshape(rows, bc)
    ys = []
    for t2 in range(rows // pair):
        kk = pair * (t2 + 1)
        ys.append(jnp.dot(tt_ref[:, table_rows - kk:], z[:kk], preferred_element_type=f32))
    y = jnp.concatenate(ys, axis=0) if len(ys) > 1 else ys[0]
    state = jnp.dot(wb_ref[:, table_rows - rows:], z, preferred_element_type=f32)
    ap = ap_ref[...]
    h0 = h0_ref[...]
    state = state + _cmul(ap[:, 0:1], ap[:, 1:2], h0)
    lane = lax.broadcasted_iota(jnp.int32, state.shape, 1) % nc
    step = 0
    while (1 << step) < nc:
        sh = 1 << step
        shifted = jnp.where(lane >= sh, pltpu.roll(state, sh, axis=1), 0.0)
        state = state + _cmul(ap[:, 2 + 2 * step:3 + 2 * step], ap[:, 3 + 2 * step:4 + 2 * step], shifted)
        step += 1
    hfin_ref[...] = state
    if nc > 1:
        h_in = jnp.where(lane >= 1, pltpu.roll(state, 1, axis=1), 0.0) + h0
    else:
        h_in = h0
    y = y + jnp.dot(ca_ref[:rows, :], h_in.astype(bf16), preferred_element_type=f32)
    off = pl.multiple_of(g8 * SSM_GROUP, SSM_GROUP)
    yt_ref[:, pl.ds(off, SSM_GROUP), :] = y.reshape(t_len, SSM_GROUP, bc)

    @pl.when(g8 == pl.num_programs(1) - 1)
    def _():
        for t in range(t_len):
            y_ref[t] = yt_ref[t].T.astype(bf16)


def _s5(ut4, tt, ca, wb, apow, h0, nc):
    t_len, groups, _, bc = ut4.shape
    per = LANES // SSM_GROUP
    rows = t_len * SSM_GROUP
    last = TABLE_T * SSM_GROUP // rows - 1
    assert (last + 1) * rows == TABLE_T * SSM_GROUP
    mix = groups * SSM_GROUP
    ncol = apow.shape[-1]
    return pl.pallas_call(
        functools.partial(_s5_kernel, nc=nc),
        grid=(groups // per, per),
        in_specs=[pl.BlockSpec((t_len, None, SSM_GROUP, bc), lambda G, g: (0, G * per + g, 0, 0)),
                  pl.BlockSpec((None, 2 * SUB * SSM_GROUP, rows), lambda G, g: (G * per + g, 0, last)),
                  pl.BlockSpec((None, rows, LANES), lambda G, g: (G * per + g, 0, 0)),
                  pl.BlockSpec((None, LANES, rows), lambda G, g: (G * per + g, 0, last)),
                  pl.BlockSpec((None, SSM_STATE, ncol), lambda G, g: (G * per + g, 0, 0)),
                  pl.BlockSpec((None, 2 * SSM_STATE, bc), lambda G, g: (G * per + g, 0, 0))],
        out_specs=[pl.BlockSpec((t_len, bc, LANES), lambda G, g: (0, 0, G)),
                   pl.BlockSpec((None, 2 * SSM_STATE, bc), lambda G, g: (G * per + g, 0, 0))],
        out_shape=[jax.ShapeDtypeStruct((t_len, bc, mix), bf16),
                   jax.ShapeDtypeStruct((groups, 2 * SSM_STATE, bc), f32)],
        scratch_shapes=[pltpu.VMEM((t_len, LANES, bc), f32)],
        compiler_params=_params(48, ("parallel", "arbitrary")),
        name="s5",
    )(ut4, tt, ca, wb, apow, h0)


def _ld(ref, start, n, c0, c1):
    if len(ref.shape) == 3:
        s, r = divmod(start, ref.shape[1])
        return ref[s, r:r + n, c0:c1]
    return ref[start:start + n, c0:c1]


def _post_kernel(y_ref, p_ref, x_ref, mk_ref, mv_ref, wglu_ref, bglu_ref, wout_ref, gpost_ref,
                 o_ref, cat_ref, *, glu, segs, mem_scale):
    mix = y_ref.shape[-1]
    memw = mk_ref.shape[-1]
    hd = memw // MEM_HEADS
    rows = cat_ref.shape[0]
    y = y_ref[...].reshape(rows, mix).astype(f32)
    gate = p_ref[:, :mix] if len(p_ref.shape) == 2 else p_ref[:, :, :mix].reshape(rows, mix)
    gate = gate.astype(f32)
    if glu:
        y = _gelu_tanh(y)
        zz = jnp.dot(y.astype(bf16), wglu_ref[...], preferred_element_type=f32) + bglu_ref[...]
        y = y * _sigmoid(zz)
    cat_ref[:, :mix] = (y * (gate * _sigmoid(gate))).astype(bf16)
    for b, pieces in enumerate(segs):
        for h in range(MEM_HEADS):
            cq = mix + h * hd
            cg = mix + memw + h * hd
            q = jnp.concatenate([_ld(p_ref, st, n, cq, cq + hd) for st, n in pieces], axis=0)
            mg = jnp.concatenate([_ld(p_ref, st, n, cg, cg + hd) for st, n in pieces], axis=0).astype(f32)
            k = mk_ref[b, :, h * hd:(h + 1) * hd]
            v = mv_ref[b, :, h * hd:(h + 1) * hd]
            s = lax.dot_general(q, k, _NT, preferred_element_type=f32) * mem_scale
            e = jnp.exp(s - jnp.max(s, axis=-1, keepdims=True))
            p = e / jnp.sum(e, axis=-1, keepdims=True)
            o = jnp.dot(p.astype(bf16), v, preferred_element_type=f32)
            om = (o * (mg * _sigmoid(mg))).astype(bf16)
            off = 0
            for st, n in pieces:
                cat_ref[st:st + n, cq:cq + hd] = om[off:off + n]
                off += n
    out = jnp.dot(cat_ref[...], wout_ref[...], preferred_element_type=f32)
    d = out.shape[-1]
    o_ref[...] = (x_ref[...].reshape(rows, d) + _rms_scale(out) * gpost_ref[...]).reshape(o_ref.shape)


def _post(y, p, x, mk, mv, wglu, bglu, wout, gpost, *, glu, segs, grid, y_spec, p_spec, x_spec, mem_spec, rows):
    mem_scale = 1.0 / math.sqrt(mk.shape[-1] // MEM_HEADS)
    return pl.pallas_call(
        functools.partial(_post_kernel, glu=glu, segs=segs, mem_scale=mem_scale),
        grid=grid,
        in_specs=[y_spec, p_spec, x_spec, mem_spec, mem_spec,
                  _resident(wglu.shape), _resident(bglu.shape), _resident(wout.shape), _resident(gpost.shape)],
        out_specs=x_spec,
        out_shape=jax.ShapeDtypeStruct(x.shape, f32),
        scratch_shapes=[pltpu.VMEM((rows, wout.shape[0]), bf16)],
        compiler_params=_params(48, ("parallel",)),
        name="post_glu" if glu else "post",
    )(y, p, x, mk, mv, wglu, bglu, wout, gpost)


def _kvb_kernel(x_ref, gkv_ref, gb_ref, wkv_ref, wb_ref, k_ref, v_ref, kvb_ref, q_ref, pr_ref):
    xs = _rms_scale(x_ref[...])
    hkv = (xs * gkv_ref[...]).astype(bf16)
    hb = (xs * gb_ref[...]).astype(bf16)
    heads, hd = k_ref.shape[1:]
    mix = heads * hd
    step = 512
    for c in range(0, wkv_ref.shape[1], step):
        r = jnp.dot(hkv, wkv_ref[:, c:c + step], preferred_element_type=f32)
        kvb_ref[:, c:c + step] = r.astype(bf16)
        for j in range(step // hd):
            h = (c % mix) // hd + j
            (k_ref if c < mix else v_ref)[:, h, :] = r[:, j * hd:(j + 1) * hd]
    for c in range(0, wb_ref.shape[1], step):
        r = jnp.dot(hb, wb_ref[:, c:c + step], preferred_element_type=f32).astype(bf16)
        if c < mix:
            q_ref[:, c:c + step] = r
        else:
            pr_ref[:, c - mix:c - mix + step] = r


def _kvb(x, gkv, gb, wkv, wb, mix):
    n, d = x.shape
    tm = min(256, n)
    rest = wb.shape[1] - mix
    heads = mix // SB_HEAD_DIM
    row = lambda w: pl.BlockSpec((tm, w), lambda i: (i, 0))
    per_head = pl.BlockSpec((tm, heads, SB_HEAD_DIM), lambda i: (i, 0, 0))
    kv_shape = jax.ShapeDtypeStruct((n, heads, SB_HEAD_DIM), f32)
    return pl.pallas_call(
        _kvb_kernel,
        grid=(n // tm,),
        in_specs=[row(d), _resident(gkv.shape), _resident(gb.shape), _resident(wkv.shape), _resident(wb.shape)],
        out_specs=[per_head, per_head, row(2 * mix), row(mix), row(rest)],
        out_shape=[kv_shape, kv_shape,
                   jax.ShapeDtypeStruct((n, 2 * mix), bf16), jax.ShapeDtypeStruct((n, mix), bf16),
                   jax.ShapeDtypeStruct((n, rest), bf16)],
        compiler_params=_params(56, ("parallel",)),
        name="kvb",
    )(x, gkv, gb, wkv, wb)


_MASKED = -1e30


def _sb_softplus_tri(z, ntri, mask):
    sp = jnp.maximum(z, 0.0) + jnp.log2(1.0 + jnp.exp2(-jnp.abs(z)))
    if mask is not None:
        sp = jnp.where(mask, sp, 0.0)
    inner = jnp.dot(sp.astype(bf16), ntri, preferred_element_type=f32)
    t = (z - sp) + inner
    if mask is not None:
        t = jnp.where(mask, t, _MASKED)
    return t, inner[:, 0:1] - sp[:, 0:1]


def _sb_apply(t, d, v, carry, acc):
    w = jnp.exp2(t + carry)
    return acc + jnp.dot(w.astype(bf16), v, preferred_element_type=f32), carry + d


def _sb_block(q, k, v, ntri, carry, acc, mask):
    z = lax.dot_general(q, k, _NT, preferred_element_type=f32)
    t, d = _sb_softplus_tri(z, ntri, mask)
    return _sb_apply(t, d, v, carry, acc)


def _tri_and_mask(n):
    r = lax.broadcasted_iota(jnp.int32, (n, n), 0)
    c = lax.broadcasted_iota(jnp.int32, (n, n), 1)
    return jnp.where(r > c, -1.0, 0.0).astype(bf16), c < r


def _sb_attn_kernel(q_ref, k_ref, v_ref, o_ref, acc_ref, car_ref, t_ref, d_ref, *, nq):
    tq = ATTN_BLOCK
    i0 = pl.program_id(2) * nq
    ntri, causal = _tri_and_mask(tq)

    def rows(ref, kb):
        off = kb * tq if isinstance(kb, int) else pl.multiple_of(kb * tq, tq)
        return ref[pl.ds(off, tq), :]

    def pieces(r_lo, diag):
        out = []
        r = r_lo
        if diag:
            out.append((r * tq, (r + 1) * tq, causal))
            r += 1
        while r < nq:
            n = min(2, nq - r)
            out.append((r * tq, (r + n) * tq, None))
            r += n
        return out

    def step(prev, cur):
        zs = []
        if cur is not None:
            k = rows(k_ref, cur[0])
            for a, b, mask in pieces(cur[1], cur[2]):
                zs.append((a, b, mask, lax.dot_general(q_ref[a:b, :], k, _NT, preferred_element_type=f32)))
        if prev is not None:
            v = rows(v_ref, prev[0])
            for a, b, _ in pieces(prev[1], False):
                acc, carry = _sb_apply(t_ref[a:b, :], d_ref[a:b, :], v, car_ref[a:b, :], acc_ref[a:b, :])
                acc_ref[a:b, :] = acc
                car_ref[a:b, :] = carry
        for a, b, mask, z in zs:
            t, d = _sb_softplus_tri(z, ntri, mask)
            t_ref[a:b, :] = t
            d_ref[a:b, :] = d

    acc_ref[...] = jnp.zeros(acc_ref.shape, f32)
    car_ref[...] = jnp.zeros(car_ref.shape, f32)
    prev = None
    for p in range(nq):
        cur = (i0 + nq - 1 - p, nq - 1 - p, True)
        step(prev, cur)
        prev = cur

    def body(j, c):
        step((i0 - j + 1, 0, False), (i0 - j, 0, False))
        return c

    lax.fori_loop(1, i0 + 1, body, 0)
    step((0, 0, False), None)
    o_ref[...] = acc_ref[...].astype(bf16)


def _sb_attn(q, kvb, bsz, seq, heads):
    tq = ATTN_BLOCK
    hd = SB_HEAD_DIM
    nq = 4 if (seq // tq) % 4 == 0 else 1
    steps = seq // (tq * nq)
    return pl.pallas_call(
        functools.partial(_sb_attn_kernel, nq=nq),
        grid=(bsz, heads, steps),
        in_specs=[pl.BlockSpec((nq * tq, hd), lambda b, h, i: (b * steps + i, h)),
                  pl.BlockSpec((seq, hd), lambda b, h, i: (b, h)),
                  pl.BlockSpec((seq, hd), lambda b, h, i: (b, heads + h))],
        out_specs=pl.BlockSpec((nq * tq, hd), lambda b, h, i: (b * steps + i, h)),
        out_shape=jax.ShapeDtypeStruct(q.shape, bf16),
        scratch_shapes=[pltpu.VMEM((nq * tq, hd), f32), pltpu.VMEM((nq * tq, 1), f32),
                        pltpu.VMEM((nq * tq, tq), f32), pltpu.VMEM((nq * tq, 1), f32)],
        compiler_params=_params(40, ("parallel", "parallel", "arbitrary")),
        name="sb_attn",
    )(q, kvb, kvb)


def _sb_attn_sample_kernel(q_ref, kn_ref, vn_ref, kc_ref, vc_ref, o_ref):
    h = pl.program_id(1)
    tq = q_ref.shape[0]
    past = kc_ref.shape[0]
    blk = min(ATTN_BLOCK, past)
    q = q_ref[...]
    tri_n, causal = _tri_and_mask(tq)
    acc = jnp.zeros((tq, q_ref.shape[1]), f32)
    carry = jnp.zeros((tq, 1), f32)
    acc, carry = _sb_block(q, kn_ref[...], vn_ref[...], tri_n, carry, acc, causal)
    tri_p, _ = _tri_and_mask(blk)
    for j in range(past // blk - 1, -1, -1):
        kb = kc_ref[j * blk:(j + 1) * blk, h, :].astype(bf16)
        vb = vc_ref[j * blk:(j + 1) * blk, h, :].astype(bf16)
        acc, carry = _sb_block(q, kb, vb, tri_p, carry, acc, None)
    o_ref[...] = acc.astype(bf16)


def _sb_attn_sample(q, kvb, cache_k, cache_v, tq):
    bsz, past, heads, hd = cache_k.shape
    cache = pl.BlockSpec((None, past, heads, hd), lambda b, h: (b, 0, 0, 0))
    return pl.pallas_call(
        _sb_attn_sample_kernel,
        grid=(bsz, heads),
        in_specs=[pl.BlockSpec((tq, hd), lambda b, h: (b, h)),
                  pl.BlockSpec((tq, hd), lambda b, h: (b, h)),
                  pl.BlockSpec((tq, hd), lambda b, h: (b, heads + h)),
                  cache, cache],
        out_specs=pl.BlockSpec((tq, hd), lambda b, h: (b, h)),
        out_shape=jax.ShapeDtypeStruct(q.shape, bf16),
        compiler_params=_params(48, ("parallel", "arbitrary")),
        name="sb_attn_sample",
    )(q, kvb, kvb, cache_k, cache_v)


def _ssm_param_tables(lam_re, lam_im, log_dt, b_re, b_im, c_re, c_im, dvec):
    dt = jnp.exp(log_dt.astype(f32))[:, None]
    lr = lam_re.astype(f32)
    li = lam_im.astype(f32)
    lre = lr * dt
    lim = li * dt
    mag = jnp.exp(lre)
    nr = mag * jnp.cos(lim) - 1.0
    ni = mag * jnp.sin(lim)
    den = lr * lr + li * li
    fr = ((nr * lr + ni * li) / den)[..., None]
    fi = ((ni * lr - nr * li) / den)[..., None]
    b_r = b_re.astype(f32)
    b_i = b_im.astype(f32)
    br = jnp.swapaxes(fr * b_r - fi * b_i, 1, 2)
    bi = jnp.swapaxes(fr * b_i + fi * b_r, 1, 2)
    cr = c_re.astype(f32)
    ci = c_im.astype(f32)
    dup = lambda v: jnp.concatenate([v, v], axis=-1)
    lre2 = dup(lre)[:, None, :]
    lim2 = dup(lim)[:, None, :]
    caa = jnp.concatenate([cr, -ci], axis=-1)
    cab = jnp.concatenate([-ci, -cr], axis=-1)
    ba = jnp.concatenate([br, bi], axis=-1)
    bb = jnp.concatenate([-bi, br], axis=-1)
    dd = dvec.astype(f32)[:, :, None]
    return lre, lim, (lre2, lim2, caa, cab, ba, bb, dd)


def _chunk_powers(lre, lim, t_len, nc):
    cols = []
    exps = [t_len]
    j = 0
    while (1 << j) < nc:
        exps.append(t_len * (1 << j))
        j += 1
    for e in exps:
        mag = jnp.exp(lre * e)
        cols += [mag * jnp.cos(lim * e), mag * jnp.sin(lim * e)]
    return jnp.stack(cols, axis=-1)


def _layer_a(x, t_len, nc, h0_lanes, mk, mv, wa, tables, lre, lim, prompt):
    n, d = x.shape
    mix = wa["w_glu"].shape[0]
    groups = mix // SSM_GROUP
    tt, ca, wb = tables
    apow = _chunk_powers(lre, lim, t_len, nc)
    if prompt:
        bc = n // t_len
        ns = 2
        xt = jnp.transpose(x.reshape(bc, t_len, d), (1, 0, 2))
        ut4, p = _inproj_a(xt, wa["g_pre"], wa["w_ut"], wa["w_rest"], ns)
        y, hfin = _s5(ut4, tt, ca, wb, apow, h0_lanes, nc)
        streams = bc // nc
        rest = p.shape[-1]
        segs = tuple(tuple((s * bc + b * nc, nc) for s in range(ns)) for b in range(streams))
        blk = lambda w: pl.BlockSpec((ns, bc, w), lambda i: (i, 0, 0))
        x1t = _post(y, p, xt, mk, mv, wa["w_glu"], wa["b_glu"], wa["w_out"], wa["g_post"],
                    glu=True, segs=segs, grid=(t_len // ns,), rows=ns * bc,
                    y_spec=blk(mix), p_spec=blk(rest), x_spec=blk(d), mem_spec=_resident(mk.shape))
        return jnp.transpose(x1t, (1, 0, 2)).reshape(n, d), hfin
    streams = n // t_len
    lanes = h0_lanes.shape[-1]
    u, p = _inproj_plain(x, wa["g_pre"], wa["w_in"], mix)
    ut4 = jnp.transpose(u.reshape(streams, t_len, groups, SSM_GROUP), (1, 2, 3, 0))
    ut4 = jnp.pad(ut4, ((0, 0), (0, 0), (0, 0), (0, lanes - streams)))
    y3, hfin = _s5(ut4, tt, ca, wb, apow, h0_lanes, nc)
    y = jnp.transpose(y3[:, :streams, :], (1, 0, 2)).reshape(n, mix)
    rest = p.shape[-1]
    whole = lambda w: pl.BlockSpec((n, w), lambda i: (0, 0))
    x1 = _post(y, p, x, mk, mv, wa["w_glu"], wa["b_glu"], wa["w_out"], wa["g_post"],
               glu=True, segs=tuple(((b * t_len, t_len),) for b in range(streams)), grid=(1,), rows=n,
               y_spec=whole(mix), p_spec=whole(rest), x_spec=whole(d), mem_spec=_resident(mk.shape))
    return x1, hfin


def kernel(x_prompt, x_sample, cache_k, cache_v, cache_mem_k, cache_mem_v, state_ssm, mem_prompt, w_in_a, w_out_a, g_pre_a, g_post_a, ssm_lam_re, ssm_lam_im, ssm_log_dt, ssm_b_re, ssm_b_im, ssm_c_re, ssm_c_im, ssm_d, w_glu, b_glu, g_kv, w_kv, w_in_b, w_out_b, g_pre_b, g_post_b, w_mem_k, w_mem_v):
    bsz, seq, d = x_prompt.shape
    dbsz, dseq, _ = x_sample.shape
    mix = w_glu.shape[-1]
    memw = w_mem_k.shape[-1]
    heads = mix // SB_HEAD_DIM
    groups = mix // SSM_GROUP
    n_mem = mem_prompt.shape[1]
    depth = w_mem_k.shape[0]
    assert depth == 2 and w_in_a.shape[0] == 1 and w_in_b.shape[0] == 1
    assert seq % CHUNK == 0 and (bsz * seq // CHUNK) % LANES == 0 and seq % ATTN_BLOCK == 0
    assert dseq % (2 * SUB) == 0 and dseq <= TABLE_T and dbsz <= LANES
    nc = seq // CHUNK
    assert nc & (nc - 1) == 0

    row = lambda v: v.astype(f32).reshape(1, -1)
    wa = dict(
        w_in=w_in_a[0].astype(bf16),
        w_ut=w_in_a[0][:, :mix].T.astype(bf16),
        w_rest=w_in_a[0][:, mix:].astype(bf16),
        w_glu=w_glu[0].astype(bf16), b_glu=row(b_glu[0]),
        w_out=w_out_a[0].astype(bf16), g_pre=row(g_pre_a[0]), g_post=row(g_post_a[0]))
    qscale = math.log2(math.e) / math.sqrt(SB_HEAD_DIM)
    w_b = jnp.concatenate([w_in_b[0][:, :mix] * qscale, w_in_b[0][:, mix:]], axis=1).astype(bf16)
    w_kv_b = w_kv.astype(bf16)
    w_out_bb = w_out_b[0].astype(bf16)
    w_mem = jnp.concatenate([w_mem_k[0], w_mem_k[1], w_mem_v[0], w_mem_v[1]], axis=1).astype(bf16)

    memf, memb = _memkv(mem_prompt.reshape(bsz * n_mem, d), w_mem, memw)
    mem_k_prompt = memf[:depth].reshape(depth, bsz, n_mem, MEM_HEADS, memw // MEM_HEADS)
    mem_v_prompt = memf[depth:].reshape(depth, bsz, n_mem, MEM_HEADS, memw // MEM_HEADS)
    mkp = memb[:depth].reshape(depth, bsz, n_mem, memw)
    mvp = memb[depth:].reshape(depth, bsz, n_mem, memw)
    mks = cache_mem_k.reshape(depth, dbsz, n_mem, memw).astype(bf16)
    mvs = cache_mem_v.reshape(depth, dbsz, n_mem, memw).astype(bf16)

    lre, lim, tab_in = _ssm_param_tables(ssm_lam_re[0], ssm_lam_im[0], ssm_log_dt[0], ssm_b_re[0], ssm_b_im[0],
                                         ssm_c_re[0], ssm_c_im[0], ssm_d[0])
    tables = _s5_tables(*tab_in)

    n_p = bsz * seq
    bc = n_p // CHUNK
    h0_p = jnp.zeros((groups, 2 * SSM_STATE, bc), f32)
    x1_p, hfin_p = _layer_a(x_prompt.reshape(n_p, d), CHUNK, nc, h0_p, mkp[0], mvp[0], wa, tables, lre, lim, True)
    k_p, v_p, kvb_p, q_p, pr_p = _kvb(x1_p, row(g_kv), row(g_pre_b[0]), w_kv_b, w_b, mix)
    o_p = _sb_attn(q_p, kvb_p, bsz, seq, heads)
    rows_b = ATTN_BLOCK
    per_b = seq // rows_b
    tile = lambda w: pl.BlockSpec((rows_b, w), lambda i: (i, 0))
    y_p = _post(o_p, pr_p, x1_p, mkp[1], mvp[1], wa["w_glu"], wa["b_glu"], w_out_bb, row(g_post_b[0]),
                glu=False, segs=(((0, rows_b),),), grid=(n_p // rows_b,), rows=rows_b,
                y_spec=tile(mix), p_spec=tile(pr_p.shape[-1]), x_spec=tile(d),
                mem_spec=pl.BlockSpec((1, n_mem, memw), lambda i: (i // per_b, 0, 0)))

    n_s = dbsz * dseq
    st = state_ssm[0].astype(f32)
    h0_s = jnp.transpose(jnp.concatenate([st[..., 0], st[..., 1]], axis=-1), (1, 2, 0))
    h0_s = jnp.pad(h0_s, ((0, 0), (0, 0), (0, LANES - dbsz)))
    x1_s, hfin_s = _layer_a(x_sample.reshape(n_s, d), dseq, 1, h0_s, mks[0], mvs[0], wa, tables, lre, lim, False)
    k_s, v_s, kvb_s, q_s, pr_s = _kvb(x1_s, row(g_kv), row(g_pre_b[0]), w_kv_b, w_b, mix)
    o_s = _sb_attn_sample(q_s, kvb_s, cache_k, cache_v, dseq)
    whole = lambda w: pl.BlockSpec((n_s, w), lambda i: (0, 0))
    y_s = _post(o_s, pr_s, x1_s, mks[1], mvs[1], wa["w_glu"], wa["b_glu"], w_out_bb, row(g_post_b[0]),
                glu=False, segs=tuple(((b * dseq, dseq),) for b in range(dbsz)), grid=(1,), rows=n_s,
                y_spec=whole(mix), p_spec=whole(pr_s.shape[-1]), x_spec=whole(d),
                mem_spec=_resident(mks[1].shape))

    def ssm_out(hfin, lanes_idx):
        h = hfin[:, :, lanes_idx]
        h = jnp.transpose(h, (2, 0, 1))
        return jnp.stack([h[..., :SSM_STATE], h[..., SSM_STATE:]], axis=-1)[None]

    ssm_prompt = ssm_out(hfin_p, jnp.arange(bsz) * nc + (nc - 1)).astype(x_prompt.dtype)
    ssm_sample = ssm_out(hfin_s, jnp.arange(dbsz)).astype(state_ssm.dtype)
    shp = (bsz, seq, heads, SB_HEAD_DIM)
    shs = (dbsz, dseq, heads, SB_HEAD_DIM)
    return (y_p.reshape(bsz, seq, d), y_s.reshape(dbsz, dseq, d),
            k_p.reshape(shp), v_p.reshape(shp), k_s.reshape(shs), v_s.reshape(shs),
            ssm_prompt, ssm_sample, mem_k_prompt, mem_v_prompt)
```

```python
import functools
import math

import jax
import jax.numpy as jnp
from jax import lax
from jax.experimental import pallas as pl
from jax.experimental.pallas import tpu as pltpu

EPS = 1e-6
CHUNK = 64
SSM_GROUP = 16
SSM_STATE = 64
SB_HEAD_DIM = 128
MEM_HEADS = 4
SUB = 8
TABLE_T = 64
LANES = 128
ATTN_BLOCK = 256
MIB = 1024 * 1024

bf16 = jnp.bfloat16
f32 = jnp.float32

_NT = (((1,), (1,)), ((), ()))


def _params(vmem_mib, semantics):
    return pltpu.CompilerParams(vmem_limit_bytes=vmem_mib * MIB, dimension_semantics=semantics)


def _resident(shape):
    zeros = (0,) * len(shape)
    return pl.BlockSpec(shape, lambda *_: zeros, pipeline_mode=pl.Buffered(1))


def _rms_scale(x):
    return x * lax.rsqrt(jnp.mean(x * x, axis=-1, keepdims=True) + EPS)


def _sigmoid(x):
    return 1.0 / (1.0 + jnp.exp(-x))


def _gelu_tanh(x):
    c = math.sqrt(2.0 / math.pi)
    return 0.5 * x * (1.0 + jnp.tanh(c * (x + 0.044715 * (x * x * x))))


def _memkv_kernel(x_ref, w_ref, of_ref, ob_ref):
    acc = jnp.dot(x_ref[...].astype(bf16), w_ref[...], preferred_element_type=f32)
    width = of_ref.shape[-1]
    for j in range(of_ref.shape[0]):
        blk = acc[:, j * width:(j + 1) * width]
        of_ref[j] = blk
        ob_ref[j] = blk.astype(bf16)


def _memkv(mem, w_cat, width):
    rows, d = mem.shape
    nout = w_cat.shape[1] // width
    tm = 256
    return pl.pallas_call(
        _memkv_kernel,
        grid=(rows // tm,),
        in_specs=[pl.BlockSpec((tm, d), lambda i: (i, 0)), _resident(w_cat.shape)],
        out_specs=[pl.BlockSpec((nout, tm, width), lambda i: (0, i, 0)),
                   pl.BlockSpec((nout, tm, width), lambda i: (0, i, 0))],
        out_shape=[jax.ShapeDtypeStruct((nout, rows, width), f32),
                   jax.ShapeDtypeStruct((nout, rows, width), bf16)],
        compiler_params=_params(40, ("parallel",)),
        name="memkv",
    )(mem, w_cat)


def _offset_row_copies(x_hbm, buf, sem, step, slot, to_hbm=False):
    ns = buf.shape[1]
    out = []
    for s in range(ns):
        hbm = x_hbm.at[:, step * ns + s, :]
        vmem = buf.at[slot, s]
        out.append(pltpu.make_async_copy(vmem, hbm, sem.at[slot]) if to_hbm
                   else pltpu.make_async_copy(hbm, vmem, sem.at[slot]))
    return out


def _fetch_offset_rows(x_hbm, buf, sem):
    i = pl.program_id(0)

    @pl.when(i == 0)
    def _():
        for c in _offset_row_copies(x_hbm, buf, sem, 0, 0):
            c.start()

    @pl.when(i + 1 < pl.num_programs(0))
    def _():
        for c in _offset_row_copies(x_hbm, buf, sem, i + 1, (i + 1) % 2):
            c.start()

    slot = i % 2
    for c in _offset_row_copies(x_hbm, buf, sem, i, slot):
        c.wait()
    return slot


def _inproj_a_kernel(x_hbm, g_ref, wut_ref, wr_ref, ut_ref, p_ref, hn_ref, xbuf, xsem):
    _, ns, bc, _ = xbuf.shape
    slot = _fetch_offset_rows(x_hbm, xbuf, xsem)
    for s in range(ns):
        hn = (_rms_scale(xbuf[slot, s]) * g_ref[...]).astype(bf16)
        hn_ref[s * bc:(s + 1) * bc, :] = hn
        ut = lax.dot_general(wut_ref[...], hn, _NT, preferred_element_type=f32)
        ut_ref[s] = ut.reshape(ut_ref.shape[1:]).astype(bf16)
    step = 512
    for c in range(0, wr_ref.shape[1], step):
        r = jnp.dot(hn_ref[...], wr_ref[:, c:c + step], preferred_element_type=f32)
        p_ref[:, :, c:c + step] = r.astype(bf16).reshape(ns, bc, step)


def _inproj_a(x3, g, w_ut, w_rest, ns):
    bc, t, d = x3.shape
    mix = w_ut.shape[0]
    rest = w_rest.shape[1]
    groups = mix // SSM_GROUP
    return pl.pallas_call(
        _inproj_a_kernel,
        grid=(t // ns,),
        in_specs=[pl.BlockSpec(memory_space=pl.ANY),
                  _resident(g.shape), _resident(w_ut.shape), _resident(w_rest.shape)],
        out_specs=[pl.BlockSpec((ns, groups, SSM_GROUP, bc), lambda i: (i, 0, 0, 0)),
                   pl.BlockSpec((ns, bc, rest), lambda i: (i, 0, 0))],
        out_shape=[jax.ShapeDtypeStruct((t, groups, SSM_GROUP, bc), bf16),
                   jax.ShapeDtypeStruct((t, bc, rest), bf16)],
        scratch_shapes=[pltpu.VMEM((ns * bc, d), bf16), pltpu.VMEM((2, ns, bc, d), f32),
                        pltpu.SemaphoreType.DMA((2,))],
        compiler_params=_params(52, ("arbitrary",)),
        name="inproj_a",
    )(x3, g, w_ut, w_rest)


def _inproj_plain_kernel(x_ref, g_ref, w_ref, u_ref, p_ref):
    hn = (_rms_scale(x_ref[...]) * g_ref[...]).astype(bf16)
    mix = u_ref.shape[1]
    u_ref[...] = jnp.dot(hn, w_ref[:, :mix], preferred_element_type=f32).astype(bf16)
    p_ref[...] = jnp.dot(hn, w_ref[:, mix:], preferred_element_type=f32).astype(bf16)


def _inproj_plain(x, g, w, mix):
    rows, d = x.shape
    rest = w.shape[1] - mix
    return pl.pallas_call(
        _inproj_plain_kernel,
        grid=(1,),
        in_specs=[pl.BlockSpec((rows, d), lambda i: (0, 0)), _resident(g.shape), _resident(w.shape)],
        out_specs=[pl.BlockSpec((rows, mix), lambda i: (0, 0)), pl.BlockSpec((rows, rest), lambda i: (0, 0))],
        out_shape=[jax.ShapeDtypeStruct((rows, mix), bf16), jax.ShapeDtypeStruct((rows, rest), bf16)],
        compiler_params=_params(40, ("arbitrary",)),
        name="inproj_plain",
    )(x, g, w)


def _s5_tables_kernel(lre_ref, lim_ref, caa_ref, cab_ref, ba_ref, bb_ref, dd_ref,
                      tt_ref, ca_ref, wb_ref, pw_ref, cas_ref, wbs_ref):
    t_len = TABLE_T
    rows = t_len * SSM_GROUP
    lre = lre_ref[...]
    lim = lim_ref[...]
    kk = lax.broadcasted_iota(jnp.int32, (t_len, LANES), 0).astype(f32)

    def powers(k):
        mag = jnp.exp(lre * k)
        th = lim * k
        return mag * jnp.cos(th), mag * jnp.sin(th)

    pr1, pi1 = powers(kk + 1.0)
    pr0, pi0 = powers((t_len - 1.0) - kk)
    pw_ref[...] = jnp.concatenate([pr1, pi1, pr0, pi0], axis=1)

    def tile_rows(v, n):
        return jnp.broadcast_to(v[None], (n,) + v.shape).reshape(n * v.shape[0], v.shape[1])

    caa = caa_ref[...]
    cab = cab_ref[...]
    ba = ba_ref[...]
    bb = bb_ref[...]
    for t in range(t_len):
        pw = jnp.broadcast_to(pw_ref[t:t + 1, :], (SSM_GROUP, 4 * LANES))
        r0 = t * SSM_GROUP
        cas_ref[r0:r0 + SSM_GROUP, :] = pw[:, 0:128] * caa + pw[:, 128:256] * cab
        wbs_ref[r0:r0 + SSM_GROUP, :] = pw[:, 256:384] * ba + pw[:, 384:512] * bb
    ca = cas_ref[...]
    wbt = wbs_ref[...]
    ca_ref[...] = ca.astype(bf16)
    wb_ref[...] = wbt.T.astype(bf16)

    blk = SUB * SSM_GROUP
    nblk = rows // blk
    rt = wbt[rows - blk:, :]
    m = [None] * nblk
    for d in range(1, nblk):
        m[d] = lax.dot_general(ca[(d - 1) * blk:d * blk, :], rt, _NT, preferred_element_type=f32,
                               precision=lax.Precision.HIGHEST)
    ca0 = jnp.concatenate([caa, ca[:blk - SSM_GROUP, :]], axis=0)
    kj = lax.dot_general(ca0, tile_rows(ba, SUB), _NT, preferred_element_type=f32,
                         precision=lax.Precision.HIGHEST)
    lane = lax.broadcasted_iota(jnp.int32, (SSM_GROUP, LANES), 1)
    hrow = lax.broadcasted_iota(jnp.int32, (SSM_GROUP, LANES), 0)
    s0_lane = lane // SSM_GROUP
    skip = jnp.where(lane % SSM_GROUP == hrow, dd_ref[...], 0.0)
    kjs = [kj[j * SSM_GROUP:(j + 1) * SSM_GROUP, :] for j in range(SUB)]
    kjs[0] = kjs[0] + skip
    drows = []
    for t0 in range(SUB):
        acc = jnp.zeros((SSM_GROUP, LANES), f32)
        for j in range(t0 + 1):
            acc = acc + jnp.where(s0_lane == t0 - j, kjs[j], 0.0)
        drows.append(acc)
    m[0] = jnp.concatenate(drows, axis=0)
    zero = jnp.zeros((blk, blk), f32)
    r1 = jnp.concatenate([m[d] for d in range(nblk - 1, -1, -1)], axis=1)
    r0 = jnp.concatenate([m[d] for d in range(nblk - 2, -1, -1)] + [zero], axis=1)
    tt_ref[...] = jnp.concatenate([r0, r1], axis=0).astype(bf16)


def _s5_tables(lre2, lim2, caa, cab, ba, bb, dd):
    groups = lre2.shape[0]
    rows = TABLE_T * SSM_GROUP

    def gspec(shape):
        return pl.BlockSpec((None,) + shape, lambda g: (g,) + (0,) * len(shape))

    return pl.pallas_call(
        _s5_tables_kernel,
        grid=(groups,),
        in_specs=[gspec((1, LANES)), gspec((1, LANES)), gspec((SSM_GROUP, LANES)), gspec((SSM_GROUP, LANES)),
                  gspec((SSM_GROUP, LANES)), gspec((SSM_GROUP, LANES)), gspec((SSM_GROUP, 1))],
        out_specs=[gspec((2 * SUB * SSM_GROUP, rows)), gspec((rows, LANES)), gspec((LANES, rows))],
        out_shape=[jax.ShapeDtypeStruct((groups, 2 * SUB * SSM_GROUP, rows), bf16),
                   jax.ShapeDtypeStruct((groups, rows, LANES), bf16),
                   jax.ShapeDtypeStruct((groups, LANES, rows), bf16)],
        scratch_shapes=[pltpu.VMEM((TABLE_T, 4 * LANES), f32), pltpu.VMEM((rows, LANES), f32),
                        pltpu.VMEM((rows, LANES), f32)],
        compiler_params=_params(40, ("parallel",)),
        name="s5_tables",
    )(lre2, lim2, caa, cab, ba, bb, dd)


def _cmul(ar, ai, x):
    half = x.shape[0] // 2
    xr = x[:half]
    xi = x[half:]
    return jnp.concatenate([ar * xr - ai * xi, ar * xi + ai * xr], axis=0)


def _s5_kernel(ut_ref, tt_ref, ca_ref, wb_ref, ap_ref, h0_ref, y_ref, hfin_ref, yt_ref, *, nc):
    g8 = pl.program_id(1)
    t_len, _, bc = ut_ref.shape
    rows = t_len * SSM_GROUP
    pair = 2 * SUB * SSM_GROUP
    table_rows = tt_ref.shape[1]
    z = ut_ref[...].reshape(rows, bc)
    ys = []
    for t2 in range(rows // pair):
        kk = pair * (t2 + 1)
        ys.append(jnp.dot(tt_ref[:, table_rows - kk:], z[:kk], preferred_element_type=f32))
    y = jnp.concatenate(ys, axis=0) if len(ys) > 1 else ys[0]
    state = jnp.dot(wb_ref[:, table_rows - rows:], z, preferred_element_type=f32)
    ap = ap_ref[...]
    h0 = h0_ref[...]
    state = state + _cmul(ap[:, 0:1], ap[:, 1:2], h0)
    lane = lax.broadcasted_iota(jnp.int32, state.shape, 1) % nc
    step = 0
    while (1 << step) < nc:
        sh = 1 << step
        shifted = jnp.where(lane >= sh, pltpu.roll(state, sh, axis=1), 0.0)
        state = state + _cmul(ap[:, 2 + 2 * step:3 + 2 * step], ap[:, 3 + 2 * step:4 + 2 * step], shifted)
        step += 1
    hfin_ref[...] = state
    if nc > 1:
        h_in = jnp.where(lane >= 1, pltpu.roll(state, 1, axis=1), 0.0) + h0
    else:
        h_in = h0
    y = y + jnp.dot(ca_ref[:rows, :], h_in.astype(bf16), preferred_element_type=f32)
    off = pl.multiple_of(g8 * SSM_GROUP, SSM_GROUP)
    yt_ref[:, pl.ds(off, SSM_GROUP), :] = y.reshape(t_len, SSM_GROUP, bc)

    @pl.when(g8 == pl.num_programs(1) - 1)
    def _():
        for t in range(t_len):
            y_ref[t] = yt_ref[t].T.astype(bf16)


def _s5(ut4, tt, ca, wb, apow, h0, nc):
    t_len, groups, _, bc = ut4.shape
    per = LANES // SSM_GROUP
    rows = t_len * SSM_GROUP
    last = TABLE_T * SSM_GROUP // rows - 1
    assert (last + 1) * rows == TABLE_T * SSM_GROUP
    mix = groups * SSM_GROUP
    ncol = apow.shape[-1]
    return pl.pallas_call(
        functools.partial(_s5_kernel, nc=nc),
        grid=(groups // per, per),
        in_specs=[pl.BlockSpec((t_len, None, SSM_GROUP, bc), lambda G, g: (0, G * per + g, 0, 0)),
                  pl.BlockSpec((None, 2 * SUB * SSM_GROUP, rows), lambda G, g: (G * per + g, 0, last)),
                  pl.BlockSpec((None, rows, LANES), lambda G, g: (G * per + g, 0, 0)),
                  pl.BlockSpec((None, LANES, rows), lambda G, g: (G * per + g, 0, last)),
                  pl.BlockSpec((None, SSM_STATE, ncol), lambda G, g: (G * per + g, 0, 0)),
                  pl.BlockSpec((None, 2 * SSM_STATE, bc), lambda G, g: (G * per + g, 0, 0))],
        out_specs=[pl.BlockSpec((t_len, bc, LANES), lambda G, g: (0, 0, G)),
                   pl.BlockSpec((None, 2 * SSM_STATE, bc), lambda G, g: (G * per + g, 0, 0))],
        out_shape=[jax.ShapeDtypeStruct((t_len, bc, mix), bf16),
                   jax.ShapeDtypeStruct((groups, 2 * SSM_STATE, bc), f32)],
        scratch_shapes=[pltpu.VMEM((t_len, LANES, bc), f32)],
        compiler_params=_params(48, ("parallel", "arbitrary")),
        name="s5",
    )(ut4, tt, ca, wb, apow, h0)


def _ld(ref, start, n, c0, c1):
    if len(ref.shape) == 3:
        s, r = divmod(start, ref.shape[1])
        return ref[s, r:r + n, c0:c1]
    return ref[start:start + n, c0:c1]


def _post_kernel(y_ref, p_ref, x_ref, mk_ref, mv_ref, wglu_ref, bglu_ref, wout_ref, gpost_ref,
                 o_ref, cat_ref, *dma, glu, segs, mem_scale):
    mix = y_ref.shape[-1]
    memw = mk_ref.shape[-1]
    hd = memw // MEM_HEADS
    rows = cat_ref.shape[0]
    if dma:
        xbuf, xsem, obuf, osem = dma
        slot = _fetch_offset_rows(x_ref, xbuf, xsem)
    y = y_ref[...].reshape(rows, mix).astype(f32)
    gate = p_ref[:, :mix] if len(p_ref.shape) == 2 else p_ref[:, :, :mix].reshape(rows, mix)
    gate = gate.astype(f32)
    if glu:
        y = _gelu_tanh(y)
        zz = jnp.dot(y.astype(bf16), wglu_ref[...], preferred_element_type=f32) + bglu_ref[...]
        y = y * _sigmoid(zz)
    cat_ref[:, :mix] = (y * (gate * _sigmoid(gate))).astype(bf16)
    for b, pieces in enumerate(segs):
        for h in range(MEM_HEADS):
            cq = mix + h * hd
            cg = mix + memw + h * hd
            q = jnp.concatenate([_ld(p_ref, st, n, cq, cq + hd) for st, n in pieces], axis=0)
            mg = jnp.concatenate([_ld(p_ref, st, n, cg, cg + hd) for st, n in pieces], axis=0).astype(f32)
            k = mk_ref[b, :, h * hd:(h + 1) * hd]
            v = mv_ref[b, :, h * hd:(h + 1) * hd]
            s = lax.dot_general(q, k, _NT, preferred_element_type=f32) * mem_scale
            e = jnp.exp(s - jnp.max(s, axis=-1, keepdims=True))
            p = e / jnp.sum(e, axis=-1, keepdims=True)
            o = jnp.dot(p.astype(bf16), v, preferred_element_type=f32)
            om = (o * (mg * _sigmoid(mg))).astype(bf16)
            off = 0
            for st, n in pieces:
                cat_ref[st:st + n, cq:cq + hd] = om[off:off + n]
                off += n
    out = jnp.dot(cat_ref[...], wout_ref[...], preferred_element_type=f32)
    d = out.shape[-1]
    branch = _rms_scale(out) * gpost_ref[...]
    if not dma:
        o_ref[...] = (x_ref[...].reshape(rows, d) + branch).reshape(o_ref.shape)
        return
    i = pl.program_id(0)
    last = pl.num_programs(0) - 1

    @pl.when(i >= 2)
    def _():
        for c in _offset_row_copies(o_ref, obuf, osem, i - 2, slot, to_hbm=True):
            c.wait()

    obuf[slot] = (xbuf[slot].reshape(rows, d) + branch).reshape(obuf.shape[1:])
    for c in _offset_row_copies(o_ref, obuf, osem, i, slot, to_hbm=True):
        c.start()

    @pl.when(i == last)
    def _():
        for c in _offset_row_copies(o_ref, obuf, osem, i, slot, to_hbm=True):
            c.wait()

    @pl.when((i == last) & (i >= 1))
    def _():
        for c in _offset_row_copies(o_ref, obuf, osem, i - 1, 1 - slot, to_hbm=True):
            c.wait()


def _post(y, p, x, mk, mv, wglu, bglu, wout, gpost, *, glu, segs, grid, y_spec, p_spec, x_spec, mem_spec, rows,
          offsets_per_step=None):
    mem_scale = 1.0 / math.sqrt(mk.shape[-1] // MEM_HEADS)
    scratch = [pltpu.VMEM((rows, wout.shape[0]), bf16)]
    semantics = ("parallel",)
    if x_spec is None:
        x_spec = pl.BlockSpec(memory_space=pl.ANY)
        buf = pltpu.VMEM((2, offsets_per_step, x.shape[0], x.shape[2]), f32)
        scratch += [buf, pltpu.SemaphoreType.DMA((2,)), buf, pltpu.SemaphoreType.DMA((2,))]
        semantics = ("arbitrary",)
    return pl.pallas_call(
        functools.partial(_post_kernel, glu=glu, segs=segs, mem_scale=mem_scale),
        grid=grid,
        in_specs=[y_spec, p_spec, x_spec, mem_spec, mem_spec,
                  _resident(wglu.shape), _resident(bglu.shape), _resident(wout.shape), _resident(gpost.shape)],
        out_specs=x_spec,
        out_shape=jax.ShapeDtypeStruct(x.shape, f32),
        scratch_shapes=scratch,
        compiler_params=_params(52, semantics),
        name="post_glu" if glu else "post",
    )(y, p, x, mk, mv, wglu, bglu, wout, gpost)


def _head_copies(kv_hbm, buf, sem, step, slot):
    k_hbm, v_hbm = kv_hbm
    heads = k_hbm.shape[1]
    tm = buf.shape[2]
    out = []
    for j in range(2 * heads):
        dst = (k_hbm if j < heads else v_hbm).at[pl.ds(step * tm, tm), j % heads, :]
        out.append(pltpu.make_async_copy(buf.at[slot, j], dst, sem.at[slot]))
    return out


def _kvb_kernel(x_ref, gkv_ref, gb_ref, wkv_ref, wb_ref, k_hbm, v_hbm, kvb_ref, q_ref, pr_ref, hbuf, hsem):
    i = pl.program_id(0)
    last = pl.num_programs(0) - 1
    slot = i % 2
    xs = _rms_scale(x_ref[...])
    hkv = (xs * gkv_ref[...]).astype(bf16)
    hb = (xs * gb_ref[...]).astype(bf16)
    heads, hd = k_hbm.shape[1:]
    mix = heads * hd

    @pl.when(i >= 2)
    def _():
        for c in _head_copies((k_hbm, v_hbm), hbuf, hsem, i - 2, slot):
            c.wait()

    step = 512
    for c in range(0, wkv_ref.shape[1], step):
        r = jnp.dot(hkv, wkv_ref[:, c:c + step], preferred_element_type=f32)
        kvb_ref[:, c:c + step] = r.astype(bf16)
        for j in range(step // hd):
            hbuf[slot, c // hd + j] = r[:, j * hd:(j + 1) * hd]
    for c in _head_copies((k_hbm, v_hbm), hbuf, hsem, i, slot):
        c.start()
    for c in range(0, wb_ref.shape[1], step):
        r = jnp.dot(hb, wb_ref[:, c:c + step], preferred_element_type=f32).astype(bf16)
        if c < mix:
            q_ref[:, c:c + step] = r
        else:
            pr_ref[:, c - mix:c - mix + step] = r

    @pl.when(i == last)
    def _():
        for c in _head_copies((k_hbm, v_hbm), hbuf, hsem, i, slot):
            c.wait()

    @pl.when((i == last) & (i >= 1))
    def _():
        for c in _head_copies((k_hbm, v_hbm), hbuf, hsem, i - 1, 1 - slot):
            c.wait()


def _kvb(x, gkv, gb, wkv, wb, mix):
    n, d = x.shape
    tm = min(256, n)
    rest = wb.shape[1] - mix
    heads = mix // SB_HEAD_DIM
    row = lambda w: pl.BlockSpec((tm, w), lambda i: (i, 0))
    in_hbm = pl.BlockSpec(memory_space=pl.ANY)
    kv_shape = jax.ShapeDtypeStruct((n, heads, SB_HEAD_DIM), f32)
    return pl.pallas_call(
        _kvb_kernel,
        grid=(n // tm,),
        in_specs=[row(d), _resident(gkv.shape), _resident(gb.shape), _resident(wkv.shape), _resident(wb.shape)],
        out_specs=[in_hbm, in_hbm, row(2 * mix), row(mix), row(rest)],
        out_shape=[kv_shape, kv_shape,
                   jax.ShapeDtypeStruct((n, 2 * mix), bf16), jax.ShapeDtypeStruct((n, mix), bf16),
                   jax.ShapeDtypeStruct((n, rest), bf16)],
        scratch_shapes=[pltpu.VMEM((2, 2 * heads, tm, SB_HEAD_DIM), f32), pltpu.SemaphoreType.DMA((2,))],
        compiler_params=_params(56, ("arbitrary",)),
        name="kvb",
    )(x, gkv, gb, wkv, wb)


_MASKED = -1e30


def _sb_softplus_tri(z, ntri, mask):
    sp = jnp.maximum(z, 0.0) + jnp.log2(1.0 + jnp.exp2(-jnp.abs(z)))
    if mask is not None:
        sp = jnp.where(mask, sp, 0.0)
    inner = jnp.dot(sp.astype(bf16), ntri, preferred_element_type=f32)
    t = (z - sp) + inner
    if mask is not None:
        t = jnp.where(mask, t, _MASKED)
    return t, inner[:, 0:1] - sp[:, 0:1]


def _sb_apply(t, d, v, carry, acc):
    w = jnp.exp2(t + carry)
    return acc + jnp.dot(w.astype(bf16), v, preferred_element_type=f32), carry + d


def _sb_block(q, k, v, ntri, carry, acc, mask):
    z = lax.dot_general(q, k, _NT, preferred_element_type=f32)
    t, d = _sb_softplus_tri(z, ntri, mask)
    return _sb_apply(t, d, v, carry, acc)


def _tri_and_mask(n):
    r = lax.broadcasted_iota(jnp.int32, (n, n), 0)
    c = lax.broadcasted_iota(jnp.int32, (n, n), 1)
    return jnp.where(r > c, -1.0, 0.0).astype(bf16), c < r


def _sb_attn_kernel(q_ref, k_ref, v_ref, o_ref, acc_ref, car_ref, t_ref, d_ref, *, nq):
    tq = ATTN_BLOCK
    i0 = pl.program_id(2) * nq
    ntri, causal = _tri_and_mask(tq)

    def rows(ref, kb):
        off = kb * tq if isinstance(kb, int) else pl.multiple_of(kb * tq, tq)
        return ref[pl.ds(off, tq), :]

    def pieces(r_lo, diag):
        out = []
        r = r_lo
        if diag:
            out.append((r * tq, (r + 1) * tq, causal))
            r += 1
        while r < nq:
            n = min(2, nq - r)
            out.append((r * tq, (r + n) * tq, None))
            r += n
        return out

    def step(prev, cur):
        zs = []
        if cur is not None:
            k = rows(k_ref, cur[0])
            for a, b, mask in pieces(cur[1], cur[2]):
                zs.append((a, b, mask, lax.dot_general(q_ref[a:b, :], k, _NT, preferred_element_type=f32)))
        if prev is not None:
            v = rows(v_ref, prev[0])
            for a, b, _ in pieces(prev[1], False):
                acc, carry = _sb_apply(t_ref[a:b, :], d_ref[a:b, :], v, car_ref[a:b, :], acc_ref[a:b, :])
                acc_ref[a:b, :] = acc
                car_ref[a:b, :] = carry
        for a, b, mask, z in zs:
            t, d = _sb_softplus_tri(z, ntri, mask)
            t_ref[a:b, :] = t
            d_ref[a:b, :] = d

    acc_ref[...] = jnp.zeros(acc_ref.shape, f32)
    car_ref[...] = jnp.zeros(car_ref.shape, f32)
    prev = None
    for p in range(nq):
        cur = (i0 + nq - 1 - p, nq - 1 - p, True)
        step(prev, cur)
        prev = cur

    def body(j, c):
        step((i0 - j + 1, 0, False), (i0 - j, 0, False))
        return c

    lax.fori_loop(1, i0 + 1, body, 0)
    step((0, 0, False), None)
    o_ref[...] = acc_ref[...].astype(bf16)


def _sb_attn(q, kvb, bsz, seq, heads):
    tq = ATTN_BLOCK
    hd = SB_HEAD_DIM
    nq = 4 if (seq // tq) % 4 == 0 else 1
    steps = seq // (tq * nq)
    return pl.pallas_call(
        functools.partial(_sb_attn_kernel, nq=nq),
        grid=(bsz, heads, steps),
        in_specs=[pl.BlockSpec((nq * tq, hd), lambda b, h, i: (b * steps + i, h)),
                  pl.BlockSpec((seq, hd), lambda b, h, i: (b, h)),
                  pl.BlockSpec((seq, hd), lambda b, h, i: (b, heads + h))],
        out_specs=pl.BlockSpec((nq * tq, hd), lambda b, h, i: (b * steps + i, h)),
        out_shape=jax.ShapeDtypeStruct(q.shape, bf16),
        scratch_shapes=[pltpu.VMEM((nq * tq, hd), f32), pltpu.VMEM((nq * tq, 1), f32),
                        pltpu.VMEM((nq * tq, tq), f32), pltpu.VMEM((nq * tq, 1), f32)],
        compiler_params=_params(40, ("parallel", "parallel", "arbitrary")),
        name="sb_attn",
    )(q, kvb, kvb)


def _cache_copies(cache_hbm, buf, sem, lin, slot):
    heads = cache_hbm[0].shape[2]
    b = lin // heads
    h = lin % heads
    return [pltpu.make_async_copy(cache_hbm[j].at[b, :, h, :], buf.at[slot, j], sem.at[slot]) for j in range(2)]


def _sb_attn_sample_kernel(q_ref, kn_ref, vn_ref, kc_hbm, vc_hbm, o_ref, cbuf, csem):
    nh = pl.num_programs(1)
    lin = pl.program_id(0) * nh + pl.program_id(1)
    total = pl.num_programs(0) * nh
    cache = (kc_hbm, vc_hbm)

    @pl.when(lin == 0)
    def _():
        for c in _cache_copies(cache, cbuf, csem, 0, 0):
            c.start()

    @pl.when(lin + 1 < total)
    def _():
        for c in _cache_copies(cache, cbuf, csem, lin + 1, (lin + 1) % 2):
            c.start()

    slot = lin % 2
    for c in _cache_copies(cache, cbuf, csem, lin, slot):
        c.wait()

    tq = q_ref.shape[0]
    past = cbuf.shape[2]
    blk = min(ATTN_BLOCK, past)
    q = q_ref[...]
    tri_n, causal = _tri_and_mask(tq)
    acc = jnp.zeros((tq, q_ref.shape[1]), f32)
    carry = jnp.zeros((tq, 1), f32)
    acc, carry = _sb_block(q, kn_ref[...], vn_ref[...], tri_n, carry, acc, causal)
    tri_p, _ = _tri_and_mask(blk)
    for j in range(past // blk - 1, -1, -1):
        kb = cbuf[slot, 0, j * blk:(j + 1) * blk, :].astype(bf16)
        vb = cbuf[slot, 1, j * blk:(j + 1) * blk, :].astype(bf16)
        acc, carry = _sb_block(q, kb, vb, tri_p, carry, acc, None)
    o_ref[...] = acc.astype(bf16)


def _sb_attn_sample(q, kvb, cache_k, cache_v, tq):
    bsz, past, heads, hd = cache_k.shape
    in_hbm = pl.BlockSpec(memory_space=pl.ANY)
    return pl.pallas_call(
        _sb_attn_sample_kernel,
        grid=(bsz, heads),
        in_specs=[pl.BlockSpec((tq, hd), lambda b, h: (b, h)),
                  pl.BlockSpec((tq, hd), lambda b, h: (b, h)),
                  pl.BlockSpec((tq, hd), lambda b, h: (b, heads + h)),
                  in_hbm, in_hbm],
        out_specs=pl.BlockSpec((tq, hd), lambda b, h: (b, h)),
        out_shape=jax.ShapeDtypeStruct(q.shape, bf16),
        scratch_shapes=[pltpu.VMEM((2, 2, past, hd), f32), pltpu.SemaphoreType.DMA((2,))],
        compiler_params=_params(40, ("arbitrary", "arbitrary")),
        name="sb_attn_sample",
    )(q, kvb, kvb, cache_k, cache_v)


def _ssm_param_tables(lam_re, lam_im, log_dt, b_re, b_im, c_re, c_im, dvec):
    dt = jnp.exp(log_dt.astype(f32))[:, None]
    lr = lam_re.astype(f32)
    li = lam_im.astype(f32)
    lre = lr * dt
    lim = li * dt
    mag = jnp.exp(lre)
    nr = mag * jnp.cos(lim) - 1.0
    ni = mag * jnp.sin(lim)
    den = lr * lr + li * li
    fr = ((nr * lr + ni * li) / den)[..., None]
    fi = ((ni * lr - nr * li) / den)[..., None]
    b_r = b_re.astype(f32)
    b_i = b_im.astype(f32)
    br = jnp.swapaxes(fr * b_r - fi * b_i, 1, 2)
    bi = jnp.swapaxes(fr * b_i + fi * b_r, 1, 2)
    cr = c_re.astype(f32)
    ci = c_im.astype(f32)
    dup = lambda v: jnp.concatenate([v, v], axis=-1)
    lre2 = dup(lre)[:, None, :]
    lim2 = dup(lim)[:, None, :]
    caa = jnp.concatenate([cr, -ci], axis=-1)
    cab = jnp.concatenate([-ci, -cr], axis=-1)
    ba = jnp.concatenate([br, bi], axis=-1)
    bb = jnp.concatenate([-bi, br], axis=-1)
    dd = dvec.astype(f32)[:, :, None]
    return lre, lim, (lre2, lim2, caa, cab, ba, bb, dd)


def _chunk_powers(lre, lim, t_len, nc):
    cols = []
    exps = [t_len]
    j = 0
    while (1 << j) < nc:
        exps.append(t_len * (1 << j))
        j += 1
    for e in exps:
        mag = jnp.exp(lre * e)
        cols += [mag * jnp.cos(lim * e), mag * jnp.sin(lim * e)]
    return jnp.stack(cols, axis=-1)


def _layer_a(x, t_len, nc, h0_lanes, mk, mv, wa, tables, lre, lim, prompt):
    n, d = x.shape
    mix = wa["w_glu"].shape[0]
    groups = mix // SSM_GROUP
    tt, ca, wb = tables
    apow = _chunk_powers(lre, lim, t_len, nc)
    if prompt:
        bc = n // t_len
        ns = 2
        x3 = x.reshape(bc, t_len, d)
        ut4, p = _inproj_a(x3, wa["g_pre"], wa["w_ut"], wa["w_rest"], ns)
        y, hfin = _s5(ut4, tt, ca, wb, apow, h0_lanes, nc)
        streams = bc // nc
        rest = p.shape[-1]
        segs = tuple(tuple((s * bc + b * nc, nc) for s in range(ns)) for b in range(streams))
        blk = lambda w: pl.BlockSpec((ns, bc, w), lambda i: (i, 0, 0))
        x1 = _post(y, p, x3, mk, mv, wa["w_glu"], wa["b_glu"], wa["w_out"], wa["g_post"],
                   glu=True, segs=segs, grid=(t_len // ns,), rows=ns * bc, offsets_per_step=ns,
                   y_spec=blk(mix), p_spec=blk(rest), x_spec=None, mem_spec=_resident(mk.shape))
        return x1.reshape(n, d), hfin
    streams = n // t_len
    lanes = h0_lanes.shape[-1]
    u, p = _inproj_plain(x, wa["g_pre"], wa["w_in"], mix)
    ut4 = jnp.transpose(u.reshape(streams, t_len, groups, SSM_GROUP), (1, 2, 3, 0))
    ut4 = jnp.pad(ut4, ((0, 0), (0, 0), (0, 0), (0, lanes - streams)))
    y3, hfin = _s5(ut4, tt, ca, wb, apow, h0_lanes, nc)
    y = jnp.transpose(y3[:, :streams, :], (1, 0, 2)).reshape(n, mix)
    rest = p.shape[-1]
    whole = lambda w: pl.BlockSpec((n, w), lambda i: (0, 0))
    x1 = _post(y, p, x, mk, mv, wa["w_glu"], wa["b_glu"], wa["w_out"], wa["g_post"],
               glu=True, segs=tuple(((b * t_len, t_len),) for b in range(streams)), grid=(1,), rows=n,
               y_spec=whole(mix), p_spec=whole(rest), x_spec=whole(d), mem_spec=_resident(mk.shape))
    return x1, hfin


def kernel(x_prompt, x_sample, cache_k, cache_v, cache_mem_k, cache_mem_v, state_ssm, mem_prompt, w_in_a, w_out_a, g_pre_a, g_post_a, ssm_lam_re, ssm_lam_im, ssm_log_dt, ssm_b_re, ssm_b_im, ssm_c_re, ssm_c_im, ssm_d, w_glu, b_glu, g_kv, w_kv, w_in_b, w_out_b, g_pre_b, g_post_b, w_mem_k, w_mem_v):
    bsz, seq, d = x_prompt.shape
    dbsz, dseq, _ = x_sample.shape
    mix = w_glu.shape[-1]
    memw = w_mem_k.shape[-1]
    heads = mix // SB_HEAD_DIM
    groups = mix // SSM_GROUP
    n_mem = mem_prompt.shape[1]
    depth = w_mem_k.shape[0]
    assert depth == 2 and w_in_a.shape[0] == 1 and w_in_b.shape[0] == 1
    assert seq % CHUNK == 0 and (bsz * seq // CHUNK) % LANES == 0 and seq % ATTN_BLOCK == 0
    assert dseq % (2 * SUB) == 0 and dseq <= TABLE_T and dbsz <= LANES
    nc = seq // CHUNK
    assert nc & (nc - 1) == 0

    row = lambda v: v.astype(f32).reshape(1, -1)
    wa = dict(
        w_in=w_in_a[0].astype(bf16),
        w_ut=w_in_a[0][:, :mix].T.astype(bf16),
        w_rest=w_in_a[0][:, mix:].astype(bf16),
        w_glu=w_glu[0].astype(bf16), b_glu=row(b_glu[0]),
        w_out=w_out_a[0].astype(bf16), g_pre=row(g_pre_a[0]), g_post=row(g_post_a[0]))
    qscale = math.log2(math.e) / math.sqrt(SB_HEAD_DIM)
    w_b = jnp.concatenate([w_in_b[0][:, :mix] * qscale, w_in_b[0][:, mix:]], axis=1).astype(bf16)
    w_kv_b = w_kv.astype(bf16)
    w_out_bb = w_out_b[0].astype(bf16)
    w_mem = jnp.concatenate([w_mem_k[0], w_mem_k[1], w_mem_v[0], w_mem_v[1]], axis=1).astype(bf16)

    memf, memb = _memkv(mem_prompt.reshape(bsz * n_mem, d), w_mem, memw)
    mem_k_prompt = memf[:depth].reshape(depth, bsz, n_mem, MEM_HEADS, memw // MEM_HEADS)
    mem_v_prompt = memf[depth:].reshape(depth, bsz, n_mem, MEM_HEADS, memw // MEM_HEADS)
    mkp = memb[:depth].reshape(depth, bsz, n_mem, memw)
    mvp = memb[depth:].reshape(depth, bsz, n_mem, memw)
    mks = cache_mem_k.reshape(depth, dbsz, n_mem, memw).astype(bf16)
    mvs = cache_mem_v.reshape(depth, dbsz, n_mem, memw).astype(bf16)

    lre, lim, tab_in = _ssm_param_tables(ssm_lam_re[0], ssm_lam_im[0], ssm_log_dt[0], ssm_b_re[0], ssm_b_im[0],
                                         ssm_c_re[0], ssm_c_im[0], ssm_d[0])
    tables = _s5_tables(*tab_in)

    n_p = bsz * seq
    bc = n_p // CHUNK
    h0_p = jnp.zeros((groups, 2 * SSM_STATE, bc), f32)
    x1_p, hfin_p = _layer_a(x_prompt.reshape(n_p, d), CHUNK, nc, h0_p, mkp[0], mvp[0], wa, tables, lre, lim, True)
    k_p, v_p, kvb_p, q_p, pr_p = _kvb(x1_p, row(g_kv), row(g_pre_b[0]), w_kv_b, w_b, mix)
    o_p = _sb_attn(q_p, kvb_p, bsz, seq, heads)
    rows_b = ATTN_BLOCK
    per_b = seq // rows_b
    tile = lambda w: pl.BlockSpec((rows_b, w), lambda i: (i, 0))
    y_p = _post(o_p, pr_p, x1_p, mkp[1], mvp[1], wa["w_glu"], wa["b_glu"], w_out_bb, row(g_post_b[0]),
                glu=False, segs=(((0, rows_b),),), grid=(n_p // rows_b,), rows=rows_b,
                y_spec=tile(mix), p_spec=tile(pr_p.shape[-1]), x_spec=tile(d),
                mem_spec=pl.BlockSpec((1, n_mem, memw), lambda i: (i // per_b, 0, 0)))

    n_s = dbsz * dseq
    st = state_ssm[0].astype(f32)
    h0_s = jnp.transpose(jnp.concatenate([st[..., 0], st[..., 1]], axis=-1), (1, 2, 0))
    h0_s = jnp.pad(h0_s, ((0, 0), (0, 0), (0, LANES - dbsz)))
    x1_s, hfin_s = _layer_a(x_sample.reshape(n_s, d), dseq, 1, h0_s, mks[0], mvs[0], wa, tables, lre, lim, False)
    k_s, v_s, kvb_s, q_s, pr_s = _kvb(x1_s, row(g_kv), row(g_pre_b[0]), w_kv_b, w_b, mix)
    o_s = _sb_attn_sample(q_s, kvb_s, cache_k, cache_v, dseq)
    whole = lambda w: pl.BlockSpec((n_s, w), lambda i: (0, 0))
    y_s = _post(o_s, pr_s, x1_s, mks[1], mvs[1], wa["w_glu"], wa["b_glu"], w_out_bb, row(g_post_b[0]),
                glu=False, segs=tuple(((b * dseq, dseq),) for b in range(dbsz)), grid=(1,), rows=n_s,
                y_spec=whole(mix), p_spec=whole(pr_s.shape[-1]), x_spec=whole(d),
                mem_spec=_resident(mks[1].shape))

    def ssm_out(hfin, lanes_idx):
        h = hfin[:, :, lanes_idx]
        h = jnp.transpose(h, (2, 0, 1))
        return jnp.stack([h[..., :SSM_STATE], h[..., SSM_STATE:]], axis=-1)[None]

    ssm_prompt = ssm_out(hfin_p, jnp.arange(bsz) * nc + (nc - 1)).astype(x_prompt.dtype)
    ssm_sample = ssm_out(hfin_s, jnp.arange(dbsz)).astype(state_ssm.dtype)
    shp = (bsz, seq, heads, SB_HEAD_DIM)
    shs = (dbsz, dseq, heads, SB_HEAD_DIM)
    return (y_p.reshape(bsz, seq, d), y_s.reshape(dbsz, dseq, d),
            k_p.reshape(shp), v_p.reshape(shp), k_s.reshape(shs), v_s.reshape(shs),
            ssm_prompt, ssm_sample, mem_k_prompt, mem_v_prompt)
```

```python
import functools
import math

import jax
import jax.numpy as jnp
from jax import lax
from jax.experimental import pallas as pl
from jax.experimental.pallas import tpu as pltpu

EPS = 1e-6
CHUNK = 64
SSM_GROUP = 16
SSM_STATE = 64
SB_HEAD_DIM = 128
MEM_HEADS = 4
SUB = 8
TABLE_T = 64
LANES = 128
ATTN_BLOCK = 256
MIB = 1024 * 1024

bf16 = jnp.bfloat16
f32 = jnp.float32

_NT = (((1,), (1,)), ((), ()))


def _params(vmem_mib, semantics):
    return pltpu.CompilerParams(vmem_limit_bytes=vmem_mib * MIB, dimension_semantics=semantics)


def _resident(shape):
    zeros = (0,) * len(shape)
    return pl.BlockSpec(shape, lambda *_: zeros, pipeline_mode=pl.Buffered(1))


def _rms_scale(x):
    return x * lax.rsqrt(jnp.mean(x * x, axis=-1, keepdims=True) + EPS)


def _sigmoid(x):
    return 1.0 / (1.0 + jnp.exp(-x))


def _gelu_tanh(x):
    c = math.sqrt(2.0 / math.pi)
    return 0.5 * x * (1.0 + jnp.tanh(c * (x + 0.044715 * (x * x * x))))


def _memkv_kernel(x_ref, w_ref, of_ref, ob_ref):
    acc = jnp.dot(x_ref[...].astype(bf16), w_ref[...], preferred_element_type=f32)
    width = of_ref.shape[-1]
    for j in range(of_ref.shape[0]):
        blk = acc[:, j * width:(j + 1) * width]
        of_ref[j] = blk
        ob_ref[j] = blk.astype(bf16)


def _memkv(mem, w_cat, width):
    rows, d = mem.shape
    nout = w_cat.shape[1] // width
    tm = 256
    return pl.pallas_call(
        _memkv_kernel,
        grid=(rows // tm,),
        in_specs=[pl.BlockSpec((tm, d), lambda i: (i, 0)), _resident(w_cat.shape)],
        out_specs=[pl.BlockSpec((nout, tm, width), lambda i: (0, i, 0)),
                   pl.BlockSpec((nout, tm, width), lambda i: (0, i, 0))],
        out_shape=[jax.ShapeDtypeStruct((nout, rows, width), f32),
                   jax.ShapeDtypeStruct((nout, rows, width), bf16)],
        compiler_params=_params(40, ("parallel",)),
        name="memkv",
    )(mem, w_cat)


def _offset_row_copies(x_hbm, buf, sem, step, slot, to_hbm=False):
    ns = buf.shape[1]
    out = []
    for s in range(ns):
        hbm = x_hbm.at[:, step * ns + s, :]
        vmem = buf.at[slot, s]
        out.append(pltpu.make_async_copy(vmem, hbm, sem.at[slot]) if to_hbm
                   else pltpu.make_async_copy(hbm, vmem, sem.at[slot]))
    return out


def _fetch_offset_rows(x_hbm, buf, sem):
    i = pl.program_id(0)

    @pl.when(i == 0)
    def _():
        for c in _offset_row_copies(x_hbm, buf, sem, 0, 0):
            c.start()

    @pl.when(i + 1 < pl.num_programs(0))
    def _():
        for c in _offset_row_copies(x_hbm, buf, sem, i + 1, (i + 1) % 2):
            c.start()

    slot = i % 2
    for c in _offset_row_copies(x_hbm, buf, sem, i, slot):
        c.wait()
    return slot


def _inproj_a_kernel(x_hbm, g_ref, wut_ref, wr_ref, ut_ref, p_ref, hn_ref, xbuf, xsem):
    _, ns, bc, _ = xbuf.shape
    slot = _fetch_offset_rows(x_hbm, xbuf, xsem)
    for s in range(ns):
        hn = (_rms_scale(xbuf[slot, s]) * g_ref[...]).astype(bf16)
        hn_ref[s * bc:(s + 1) * bc, :] = hn
        ut = lax.dot_general(wut_ref[...], hn, _NT, preferred_element_type=f32)
        ut_ref[s] = ut.reshape(ut_ref.shape[1:]).astype(bf16)
    step = 512
    for c in range(0, wr_ref.shape[1], step):
        r = jnp.dot(hn_ref[...], wr_ref[:, c:c + step], preferred_element_type=f32)
        p_ref[:, :, c:c + step] = r.astype(bf16).reshape(ns, bc, step)


def _inproj_a(x3, g, w_ut, w_rest, ns):
    bc, t, d = x3.shape
    mix = w_ut.shape[0]
    rest = w_rest.shape[1]
    groups = mix // SSM_GROUP
    return pl.pallas_call(
        _inproj_a_kernel,
        grid=(t // ns,),
        in_specs=[pl.BlockSpec(memory_space=pl.ANY),
                  _resident(g.shape), _resident(w_ut.shape), _resident(w_rest.shape)],
        out_specs=[pl.BlockSpec((ns, groups, SSM_GROUP, bc), lambda i: (i, 0, 0, 0)),
                   pl.BlockSpec((ns, bc, rest), lambda i: (i, 0, 0))],
        out_shape=[jax.ShapeDtypeStruct((t, groups, SSM_GROUP, bc), bf16),
                   jax.ShapeDtypeStruct((t, bc, rest), bf16)],
        scratch_shapes=[pltpu.VMEM((ns * bc, d), bf16), pltpu.VMEM((2, ns, bc, d), f32),
                        pltpu.SemaphoreType.DMA((2,))],
        compiler_params=_params(52, ("arbitrary",)),
        name="inproj_a",
    )(x3, g, w_ut, w_rest)


def _inproj_plain_kernel(x_ref, g_ref, w_ref, u_ref, p_ref):
    hn = (_rms_scale(x_ref[...]) * g_ref[...]).astype(bf16)
    mix = u_ref.shape[1]
    u_ref[...] = jnp.dot(hn, w_ref[:, :mix], preferred_element_type=f32).astype(bf16)
    p_ref[...] = jnp.dot(hn, w_ref[:, mix:], preferred_element_type=f32).astype(bf16)


def _inproj_plain(x, g, w, mix):
    rows, d = x.shape
    rest = w.shape[1] - mix
    return pl.pallas_call(
        _inproj_plain_kernel,
        grid=(1,),
        in_specs=[pl.BlockSpec((rows, d), lambda i: (0, 0)), _resident(g.shape), _resident(w.shape)],
        out_specs=[pl.BlockSpec((rows, mix), lambda i: (0, 0)), pl.BlockSpec((rows, rest), lambda i: (0, 0))],
        out_shape=[jax.ShapeDtypeStruct((rows, mix), bf16), jax.ShapeDtypeStruct((rows, rest), bf16)],
        compiler_params=_params(40, ("arbitrary",)),
        name="inproj_plain",
    )(x, g, w)


def _s5_tables_kernel(lre_ref, lim_ref, caa_ref, cab_ref, ba_ref, bb_ref, dd_ref,
                      tt_ref, ca_ref, wb_ref, pw_ref, cas_ref, wbs_ref):
    t_len = TABLE_T
    rows = t_len * SSM_GROUP
    lre = lre_ref[...]
    lim = lim_ref[...]
    kk = lax.broadcasted_iota(jnp.int32, (t_len, LANES), 0).astype(f32)

    def powers(k):
        mag = jnp.exp(lre * k)
        th = lim * k
        return mag * jnp.cos(th), mag * jnp.sin(th)

    pr1, pi1 = powers(kk + 1.0)
    pr0, pi0 = powers((t_len - 1.0) - kk)
    pw_ref[...] = jnp.concatenate([pr1, pi1, pr0, pi0], axis=1)

    def tile_rows(v, n):
        return jnp.broadcast_to(v[None], (n,) + v.shape).reshape(n * v.shape[0], v.shape[1])

    caa = caa_ref[...]
    cab = cab_ref[...]
    ba = ba_ref[...]
    bb = bb_ref[...]
    for t in range(t_len):
        pw = jnp.broadcast_to(pw_ref[t:t + 1, :], (SSM_GROUP, 4 * LANES))
        r0 = t * SSM_GROUP
        cas_ref[r0:r0 + SSM_GROUP, :] = pw[:, 0:128] * caa + pw[:, 128:256] * cab
        wbs_ref[r0:r0 + SSM_GROUP, :] = pw[:, 256:384] * ba + pw[:, 384:512] * bb
    ca = cas_ref[...]
    wbt = wbs_ref[...]
    ca_ref[...] = ca.astype(bf16)
    wb_ref[...] = wbt.T.astype(bf16)

    blk = SUB * SSM_GROUP
    nblk = rows // blk
    rt = wbt[rows - blk:, :]
    m = [None] * nblk
    for d in range(1, nblk):
        m[d] = lax.dot_general(ca[(d - 1) * blk:d * blk, :], rt, _NT, preferred_element_type=f32,
                               precision=lax.Precision.HIGHEST)
    ca0 = jnp.concatenate([caa, ca[:blk - SSM_GROUP, :]], axis=0)
    kj = lax.dot_general(ca0, tile_rows(ba, SUB), _NT, preferred_element_type=f32,
                         precision=lax.Precision.HIGHEST)
    lane = lax.broadcasted_iota(jnp.int32, (SSM_GROUP, LANES), 1)
    hrow = lax.broadcasted_iota(jnp.int32, (SSM_GROUP, LANES), 0)
    s0_lane = lane // SSM_GROUP
    skip = jnp.where(lane % SSM_GROUP == hrow, dd_ref[...], 0.0)
    kjs = [kj[j * SSM_GROUP:(j + 1) * SSM_GROUP, :] for j in range(SUB)]
    kjs[0] = kjs[0] + skip
    drows = []
    for t0 in range(SUB):
        acc = jnp.zeros((SSM_GROUP, LANES), f32)
        for j in range(t0 + 1):
            acc = acc + jnp.where(s0_lane == t0 - j, kjs[j], 0.0)
        drows.append(acc)
    m[0] = jnp.concatenate(drows, axis=0)
    zero = jnp.zeros((blk, blk), f32)
    r1 = jnp.concatenate([m[d] for d in range(nblk - 1, -1, -1)], axis=1)
    r0 = jnp.concatenate([m[d] for d in range(nblk - 2, -1, -1)] + [zero], axis=1)
    tt_ref[...] = jnp.concatenate([r0, r1], axis=0).astype(bf16)


def _s5_tables(lre2, lim2, caa, cab, ba, bb, dd):
    groups = lre2.shape[0]
    rows = TABLE_T * SSM_GROUP

    def gspec(shape):
        return pl.BlockSpec((None,) + shape, lambda g: (g,) + (0,) * len(shape))

    return pl.pallas_call(
        _s5_tables_kernel,
        grid=(groups,),
        in_specs=[gspec((1, LANES)), gspec((1, LANES)), gspec((SSM_GROUP, LANES)), gspec((SSM_GROUP, LANES)),
                  gspec((SSM_GROUP, LANES)), gspec((SSM_GROUP, LANES)), gspec((SSM_GROUP, 1))],
        out_specs=[gspec((2 * SUB * SSM_GROUP, rows)), gspec((rows, LANES)), gspec((LANES, rows))],
        out_shape=[jax.ShapeDtypeStruct((groups, 2 * SUB * SSM_GROUP, rows), bf16),
                   jax.ShapeDtypeStruct((groups, rows, LANES), bf16),
                   jax.ShapeDtypeStruct((groups, LANES, rows), bf16)],
        scratch_shapes=[pltpu.VMEM((TABLE_T, 4 * LANES), f32), pltpu.VMEM((rows, LANES), f32),
                        pltpu.VMEM((rows, LANES), f32)],
        compiler_params=_params(40, ("parallel",)),
        name="s5_tables",
    )(lre2, lim2, caa, cab, ba, bb, dd)


def _cmul(ar, ai, x):
    half = x.shape[0] // 2
    xr = x[:half]
    xi = x[half:]
    return jnp.concatenate([ar * xr - ai * xi, ar * xi + ai * xr], axis=0)


def _s5_kernel(ut_ref, tt_ref, ca_ref, wb_ref, ap_ref, h0_ref, y_ref, hfin_ref, yt_ref, *, nc):
    g8 = pl.program_id(1)
    t_len, _, bc = ut_ref.shape
    rows = t_len * SSM_GROUP
    pair = 2 * SUB * SSM_GROUP
    table_rows = tt_ref.shape[1]
    z = ut_ref[...].reshape(rows, bc)
    ys = []
    for t2 in range(rows // pair):
        kk = pair * (t2 + 1)
        ys.append(jnp.dot(tt_ref[:, table_rows - kk:], z[:kk], preferred_element_type=f32))
    y = jnp.concatenate(ys, axis=0) if len(ys) > 1 else ys[0]
    state = jnp.dot(wb_ref[:, table_rows - rows:], z, preferred_element_type=f32)
    ap = ap_ref[...]
    h0 = h0_ref[...]
    state = state + _cmul(ap[:, 0:1], ap[:, 1:2], h0)
    lane = lax.broadcasted_iota(jnp.int32, state.shape, 1) % nc
    step = 0
    while (1 << step) < nc:
        sh = 1 << step
        shifted = jnp.where(lane >= sh, pltpu.roll(state, sh, axis=1), 0.0)
        state = state + _cmul(ap[:, 2 + 2 * step:3 + 2 * step], ap[:, 3 + 2 * step:4 + 2 * step], shifted)
        step += 1
    hfin_ref[...] = state
    if nc > 1:
        h_in = jnp.where(lane >= 1, pltpu.roll(state, 1, axis=1), 0.0) + h0
    else:
        h_in = h0
    y = y + jnp.dot(ca_ref[:rows, :], h_in.astype(bf16), preferred_element_type=f32)
    off = pl.multiple_of(g8 * SSM_GROUP, SSM_GROUP)
    yt_ref[:, pl.ds(off, SSM_GROUP), :] = y.reshape(t_len, SSM_GROUP, bc)

    @pl.when(g8 == pl.num_programs(1) - 1)
    def _():
        for t in range(t_len):
            y_ref[t] = yt_ref[t].T.astype(bf16)


def _s5(ut4, tt, ca, wb, apow, h0, nc):
    t_len, groups, _, bc = ut4.shape
    per = LANES // SSM_GROUP
    rows = t_len * SSM_GROUP
    last = TABLE_T * SSM_GROUP // rows - 1
    assert (last + 1) * rows == TABLE_T * SSM_GROUP
    mix = groups * SSM_GROUP
    ncol = apow.shape[-1]
    return pl.pallas_call(
        functools.partial(_s5_kernel, nc=nc),
        grid=(groups // per, per),
        in_specs=[pl.BlockSpec((t_len, None, SSM_GROUP, bc), lambda G, g: (0, G * per + g, 0, 0)),
                  pl.BlockSpec((None, 2 * SUB * SSM_GROUP, rows), lambda G, g: (G * per + g, 0, last)),
                  pl.BlockSpec((None, rows, LANES), lambda G, g: (G * per + g, 0, 0)),
                  pl.BlockSpec((None, LANES, rows), lambda G, g: (G * per + g, 0, last)),
                  pl.BlockSpec((None, SSM_STATE, ncol), lambda G, g: (G * per + g, 0, 0)),
                  pl.BlockSpec((None, 2 * SSM_STATE, bc), lambda G, g: (G * per + g, 0, 0))],
        out_specs=[pl.BlockSpec((t_len, bc, LANES), lambda G, g: (0, 0, G)),
                   pl.BlockSpec((None, 2 * SSM_STATE, bc), lambda G, g: (G * per + g, 0, 0))],
        out_shape=[jax.ShapeDtypeStruct((t_len, bc, mix), bf16),
                   jax.ShapeDtypeStruct((groups, 2 * SSM_STATE, bc), f32)],
        scratch_shapes=[pltpu.VMEM((t_len, LANES, bc), f32)],
        compiler_params=_params(48, ("parallel", "arbitrary")),
        name="s5",
    )(ut4, tt, ca, wb, apow, h0)


def _ld(ref, start, n, c0, c1):
    if len(ref.shape) == 3:
        s, r = divmod(start, ref.shape[1])
        return ref[s, r:r + n, c0:c1]
    return ref[start:start + n, c0:c1]


def _post_kernel(y_ref, p_ref, x_ref, mk_ref, mv_ref, wglu_ref, bglu_ref, wout_ref, gpost_ref,
                 o_ref, cat_ref, *dma, glu, segs, mem_scale):
    mix = y_ref.shape[-1]
    memw = mk_ref.shape[-1]
    hd = memw // MEM_HEADS
    rows = cat_ref.shape[0]
    if dma:
        xbuf, xsem, obuf, osem = dma
        slot = _fetch_offset_rows(x_ref, xbuf, xsem)
    y = y_ref[...].reshape(rows, mix).astype(f32)
    gate = p_ref[:, :mix] if len(p_ref.shape) == 2 else p_ref[:, :, :mix].reshape(rows, mix)
    gate = gate.astype(f32)
    if glu:
        y = _gelu_tanh(y)
        zz = jnp.dot(y.astype(bf16), wglu_ref[...], preferred_element_type=f32) + bglu_ref[...]
        y = y * _sigmoid(zz)
    cat_ref[:, :mix] = (y * (gate * _sigmoid(gate))).astype(bf16)
    for b, pieces in enumerate(segs):
        for h in range(MEM_HEADS):
            cq = mix + h * hd
            cg = mix + memw + h * hd
            q = jnp.concatenate([_ld(p_ref, st, n, cq, cq + hd) for st, n in pieces], axis=0)
            mg = jnp.concatenate([_ld(p_ref, st, n, cg, cg + hd) for st, n in pieces], axis=0).astype(f32)
            k = mk_ref[b, :, h * hd:(h + 1) * hd]
            v = mv_ref[b, :, h * hd:(h + 1) * hd]
            s = lax.dot_general(q, k, _NT, preferred_element_type=f32) * mem_scale
            e = jnp.exp(s - jnp.max(s, axis=-1, keepdims=True))
            p = e / jnp.sum(e, axis=-1, keepdims=True)
            o = jnp.dot(p.astype(bf16), v, preferred_element_type=f32)
            om = (o * (mg * _sigmoid(mg))).astype(bf16)
            off = 0
            for st, n in pieces:
                cat_ref[st:st + n, cq:cq + hd] = om[off:off + n]
                off += n
    out = jnp.dot(cat_ref[...], wout_ref[...], preferred_element_type=f32)
    d = out.shape[-1]
    branch = _rms_scale(out) * gpost_ref[...]
    if not dma:
        o_ref[...] = (x_ref[...].reshape(rows, d) + branch).reshape(o_ref.shape)
        return
    i = pl.program_id(0)
    last = pl.num_programs(0) - 1

    @pl.when(i >= 2)
    def _():
        for c in _offset_row_copies(o_ref, obuf, osem, i - 2, slot, to_hbm=True):
            c.wait()

    obuf[slot] = (xbuf[slot].reshape(rows, d) + branch).reshape(obuf.shape[1:])
    for c in _offset_row_copies(o_ref, obuf, osem, i, slot, to_hbm=True):
        c.start()

    @pl.when(i == last)
    def _():
        for c in _offset_row_copies(o_ref, obuf, osem, i, slot, to_hbm=True):
            c.wait()

    @pl.when((i == last) & (i >= 1))
    def _():
        for c in _offset_row_copies(o_ref, obuf, osem, i - 1, 1 - slot, to_hbm=True):
            c.wait()


def _post(y, p, x, mk, mv, wglu, bglu, wout, gpost, *, glu, segs, grid, y_spec, p_spec, x_spec, mem_spec, rows,
          offsets_per_step=None):
    mem_scale = 1.0 / math.sqrt(mk.shape[-1] // MEM_HEADS)
    scratch = [pltpu.VMEM((rows, wout.shape[0]), bf16)]
    semantics = ("parallel",)
    if x_spec is None:
        x_spec = pl.BlockSpec(memory_space=pl.ANY)
        buf = pltpu.VMEM((2, offsets_per_step, x.shape[0], x.shape[2]), f32)
        scratch += [buf, pltpu.SemaphoreType.DMA((2,)), buf, pltpu.SemaphoreType.DMA((2,))]
        semantics = ("arbitrary",)
    return pl.pallas_call(
        functools.partial(_post_kernel, glu=glu, segs=segs, mem_scale=mem_scale),
        grid=grid,
        in_specs=[y_spec, p_spec, x_spec, mem_spec, mem_spec,
                  _resident(wglu.shape), _resident(bglu.shape), _resident(wout.shape), _resident(gpost.shape)],
        out_specs=x_spec,
        out_shape=jax.ShapeDtypeStruct(x.shape, f32),
        scratch_shapes=scratch,
        compiler_params=_params(52, semantics),
        name="post_glu" if glu else "post",
    )(y, p, x, mk, mv, wglu, bglu, wout, gpost)


def _kvb_kernel(x_ref, gkv_ref, gb_ref, wkv_ref, wb_ref, k_ref, v_ref, kvb_ref, q_ref, pr_ref):
    xs = _rms_scale(x_ref[...])
    hkv = (xs * gkv_ref[...]).astype(bf16)
    hb = (xs * gb_ref[...]).astype(bf16)
    nst, heads, rps, hd = k_ref.shape
    mix = heads * hd
    step = 512
    for c in range(0, wkv_ref.shape[1], step):
        r = jnp.dot(hkv, wkv_ref[:, c:c + step], preferred_element_type=f32)
        kvb_ref[:, c:c + step] = r.astype(bf16)
        for j in range(step // hd):
            h = (c % mix) // hd + j
            (k_ref if c < mix else v_ref)[:, h] = r[:, j * hd:(j + 1) * hd].reshape(nst, rps, hd)
    for c in range(0, wb_ref.shape[1], step):
        r = jnp.dot(hb, wb_ref[:, c:c + step], preferred_element_type=f32).astype(bf16)
        if c < mix:
            q_ref[:, c:c + step] = r
        else:
            pr_ref[:, c - mix:c - mix + step] = r


def _kvb(x, gkv, gb, wkv, wb, mix, streams):
    n, d = x.shape
    seq = n // streams
    tm = min(256, n)
    rps = min(tm, seq)
    assert tm % rps == 0 and seq % rps == 0 and rps % 8 == 0
    per_stream = seq // rps
    rest = wb.shape[1] - mix
    heads = mix // SB_HEAD_DIM
    row = lambda w: pl.BlockSpec((tm, w), lambda i: (i, 0))
    head_major = pl.BlockSpec((tm // rps, heads, rps, SB_HEAD_DIM), lambda i: (i // per_stream, 0, i % per_stream, 0))
    kv_shape = jax.ShapeDtypeStruct((streams, heads, seq, SB_HEAD_DIM), f32)
    return pl.pallas_call(
        _kvb_kernel,
        grid=(n // tm,),
        in_specs=[row(d), _resident(gkv.shape), _resident(gb.shape), _resident(wkv.shape), _resident(wb.shape)],
        out_specs=[head_major, head_major, row(2 * mix), row(mix), row(rest)],
        out_shape=[kv_shape, kv_shape,
                   jax.ShapeDtypeStruct((n, 2 * mix), bf16), jax.ShapeDtypeStruct((n, mix), bf16),
                   jax.ShapeDtypeStruct((n, rest), bf16)],
        compiler_params=_params(56, ("parallel",)),
        name="kvb",
    )(x, gkv, gb, wkv, wb)


_MASKED = -1e30


def _sb_softplus_tri(z, ntri, mask):
    sp = jnp.maximum(z, 0.0) + jnp.log2(1.0 + jnp.exp2(-jnp.abs(z)))
    if mask is not None:
        sp = jnp.where(mask, sp, 0.0)
    inner = jnp.dot(sp.astype(bf16), ntri, preferred_element_type=f32)
    t = (z - sp) + inner
    if mask is not None:
        t = jnp.where(mask, t, _MASKED)
    return t, inner[:, 0:1] - sp[:, 0:1]


def _sb_apply(t, d, v, carry, acc):
    w = jnp.exp2(t + carry)
    return acc + jnp.dot(w.astype(bf16), v, preferred_element_type=f32), carry + d


def _sb_block(q, k, v, ntri, carry, acc, mask):
    z = lax.dot_general(q, k, _NT, preferred_element_type=f32)
    t, d = _sb_softplus_tri(z, ntri, mask)
    return _sb_apply(t, d, v, carry, acc)


def _tri_and_mask(n):
    r = lax.broadcasted_iota(jnp.int32, (n, n), 0)
    c = lax.broadcasted_iota(jnp.int32, (n, n), 1)
    return jnp.where(r > c, -1.0, 0.0).astype(bf16), c < r


def _sb_attn_kernel(q_ref, k_ref, v_ref, o_ref, acc_ref, car_ref, t_ref, d_ref, *, nq):
    tq = ATTN_BLOCK
    i0 = pl.program_id(2) * nq
    ntri, causal = _tri_and_mask(tq)

    def rows(ref, kb):
        off = kb * tq if isinstance(kb, int) else pl.multiple_of(kb * tq, tq)
        return ref[pl.ds(off, tq), :]

    def pieces(r_lo, diag):
        out = []
        r = r_lo
        if diag:
            out.append((r * tq, (r + 1) * tq, causal))
            r += 1
        while r < nq:
            n = min(2, nq - r)
            out.append((r * tq, (r + n) * tq, None))
            r += n
        return out

    def step(prev, cur):
        zs = []
        if cur is not None:
            k = rows(k_ref, cur[0])
            for a, b, mask in pieces(cur[1], cur[2]):
                zs.append((a, b, mask, lax.dot_general(q_ref[a:b, :], k, _NT, preferred_element_type=f32)))
        if prev is not None:
            v = rows(v_ref, prev[0])
            for a, b, _ in pieces(prev[1], False):
                acc, carry = _sb_apply(t_ref[a:b, :], d_ref[a:b, :], v, car_ref[a:b, :], acc_ref[a:b, :])
                acc_ref[a:b, :] = acc
                car_ref[a:b, :] = carry
        for a, b, mask, z in zs:
            t, d = _sb_softplus_tri(z, ntri, mask)
            t_ref[a:b, :] = t
            d_ref[a:b, :] = d

    acc_ref[...] = jnp.zeros(acc_ref.shape, f32)
    car_ref[...] = jnp.zeros(car_ref.shape, f32)
    prev = None
    for p in range(nq):
        cur = (i0 + nq - 1 - p, nq - 1 - p, True)
        step(prev, cur)
        prev = cur

    def body(j, c):
        step((i0 - j + 1, 0, False), (i0 - j, 0, False))
        return c

    lax.fori_loop(1, i0 + 1, body, 0)
    step((0, 0, False), None)
    o_ref[...] = acc_ref[...].astype(bf16)


def _sb_attn(q, kvb, bsz, seq, heads):
    tq = ATTN_BLOCK
    hd = SB_HEAD_DIM
    nq = 4 if (seq // tq) % 4 == 0 else 1
    steps = seq // (tq * nq)
    return pl.pallas_call(
        functools.partial(_sb_attn_kernel, nq=nq),
        grid=(bsz, heads, steps),
        in_specs=[pl.BlockSpec((nq * tq, hd), lambda b, h, i: (b * steps + i, h)),
                  pl.BlockSpec((seq, hd), lambda b, h, i: (b, h)),
                  pl.BlockSpec((seq, hd), lambda b, h, i: (b, heads + h))],
        out_specs=pl.BlockSpec((nq * tq, hd), lambda b, h, i: (b * steps + i, h)),
        out_shape=jax.ShapeDtypeStruct(q.shape, bf16),
        scratch_shapes=[pltpu.VMEM((nq * tq, hd), f32), pltpu.VMEM((nq * tq, 1), f32),
                        pltpu.VMEM((nq * tq, tq), f32), pltpu.VMEM((nq * tq, 1), f32)],
        compiler_params=_params(40, ("parallel", "parallel", "arbitrary")),
        name="sb_attn",
    )(q, kvb, kvb)


def _sb_attn_sample_kernel(q_ref, kn_ref, vn_ref, kc_ref, vc_ref, o_ref):
    tq = q_ref.shape[0]
    past = kc_ref.shape[0]
    blk = min(ATTN_BLOCK, past)
    q = q_ref[...]
    tri_n, causal = _tri_and_mask(tq)
    acc = jnp.zeros((tq, q_ref.shape[1]), f32)
    carry = jnp.zeros((tq, 1), f32)
    acc, carry = _sb_block(q, kn_ref[...], vn_ref[...], tri_n, carry, acc, causal)
    tri_p, _ = _tri_and_mask(blk)
    for j in range(past // blk - 1, -1, -1):
        kb = kc_ref[j * blk:(j + 1) * blk, :].astype(bf16)
        vb = vc_ref[j * blk:(j + 1) * blk, :].astype(bf16)
        acc, carry = _sb_block(q, kb, vb, tri_p, carry, acc, None)
    o_ref[...] = acc.astype(bf16)


def _sb_attn_sample(q, kvb, cache_k, cache_v, tq):
    bsz, heads, past, hd = cache_k.shape
    cache = pl.BlockSpec((None, None, past, hd), lambda b, h: (b, h, 0, 0))
    return pl.pallas_call(
        _sb_attn_sample_kernel,
        grid=(bsz, heads),
        in_specs=[pl.BlockSpec((tq, hd), lambda b, h: (b, h)),
                  pl.BlockSpec((tq, hd), lambda b, h: (b, h)),
                  pl.BlockSpec((tq, hd), lambda b, h: (b, heads + h)),
                  cache, cache],
        out_specs=pl.BlockSpec((tq, hd), lambda b, h: (b, h)),
        out_shape=jax.ShapeDtypeStruct(q.shape, bf16),
        compiler_params=_params(40, ("parallel", "parallel")),
        name="sb_attn_sample",
    )(q, kvb, kvb, cache_k, cache_v)


def _ssm_param_tables(lam_re, lam_im, log_dt, b_re, b_im, c_re, c_im, dvec):
    dt = jnp.exp(log_dt.astype(f32))[:, None]
    lr = lam_re.astype(f32)
    li = lam_im.astype(f32)
    lre = lr * dt
    lim = li * dt
    mag = jnp.exp(lre)
    nr = mag * jnp.cos(lim) - 1.0
    ni = mag * jnp.sin(lim)
    den = lr * lr + li * li
    fr = ((nr * lr + ni * li) / den)[..., None]
    fi = ((ni * lr - nr * li) / den)[..., None]
    b_r = b_re.astype(f32)
    b_i = b_im.astype(f32)
    br = jnp.swapaxes(fr * b_r - fi * b_i, 1, 2)
    bi = jnp.swapaxes(fr * b_i + fi * b_r, 1, 2)
    cr = c_re.astype(f32)
    ci = c_im.astype(f32)
    dup = lambda v: jnp.concatenate([v, v], axis=-1)
    lre2 = dup(lre)[:, None, :]
    lim2 = dup(lim)[:, None, :]
    caa = jnp.concatenate([cr, -ci], axis=-1)
    cab = jnp.concatenate([-ci, -cr], axis=-1)
    ba = jnp.concatenate([br, bi], axis=-1)
    bb = jnp.concatenate([-bi, br], axis=-1)
    dd = dvec.astype(f32)[:, :, None]
    return lre, lim, (lre2, lim2, caa, cab, ba, bb, dd)


def _chunk_powers(lre, lim, t_len, nc):
    cols = []
    exps = [t_len]
    j = 0
    while (1 << j) < nc:
        exps.append(t_len * (1 << j))
        j += 1
    for e in exps:
        mag = jnp.exp(lre * e)
        cols += [mag * jnp.cos(lim * e), mag * jnp.sin(lim * e)]
    return jnp.stack(cols, axis=-1)


def _layer_a(x, t_len, nc, h0_lanes, mk, mv, wa, tables, lre, lim, prompt):
    n, d = x.shape
    mix = wa["w_glu"].shape[0]
    groups = mix // SSM_GROUP
    tt, ca, wb = tables
    apow = _chunk_powers(lre, lim, t_len, nc)
    if prompt:
        bc = n // t_len
        ns = 2
        x3 = x.reshape(bc, t_len, d)
        ut4, p = _inproj_a(x3, wa["g_pre"], wa["w_ut"], wa["w_rest"], ns)
        y, hfin = _s5(ut4, tt, ca, wb, apow, h0_lanes, nc)
        streams = bc // nc
        rest = p.shape[-1]
        segs = tuple(tuple((s * bc + b * nc, nc) for s in range(ns)) for b in range(streams))
        blk = lambda w: pl.BlockSpec((ns, bc, w), lambda i: (i, 0, 0))
        x1 = _post(y, p, x3, mk, mv, wa["w_glu"], wa["b_glu"], wa["w_out"], wa["g_post"],
                   glu=True, segs=segs, grid=(t_len // ns,), rows=ns * bc, offsets_per_step=ns,
                   y_spec=blk(mix), p_spec=blk(rest), x_spec=None, mem_spec=_resident(mk.shape))
        return x1.reshape(n, d), hfin
    streams = n // t_len
    lanes = h0_lanes.shape[-1]
    u, p = _inproj_plain(x, wa["g_pre"], wa["w_in"], mix)
    ut4 = jnp.transpose(u.reshape(streams, t_len, groups, SSM_GROUP), (1, 2, 3, 0))
    ut4 = jnp.pad(ut4, ((0, 0), (0, 0), (0, 0), (0, lanes - streams)))
    y3, hfin = _s5(ut4, tt, ca, wb, apow, h0_lanes, nc)
    y = jnp.transpose(y3[:, :streams, :], (1, 0, 2)).reshape(n, mix)
    rest = p.shape[-1]
    whole = lambda w: pl.BlockSpec((n, w), lambda i: (0, 0))
    x1 = _post(y, p, x, mk, mv, wa["w_glu"], wa["b_glu"], wa["w_out"], wa["g_post"],
               glu=True, segs=tuple(((b * t_len, t_len),) for b in range(streams)), grid=(1,), rows=n,
               y_spec=whole(mix), p_spec=whole(rest), x_spec=whole(d), mem_spec=_resident(mk.shape))
    return x1, hfin


def kernel(x_prompt, x_sample, cache_k, cache_v, cache_mem_k, cache_mem_v, state_ssm, mem_prompt, w_in_a, w_out_a, g_pre_a, g_post_a, ssm_lam_re, ssm_lam_im, ssm_log_dt, ssm_b_re, ssm_b_im, ssm_c_re, ssm_c_im, ssm_d, w_glu, b_glu, g_kv, w_kv, w_in_b, w_out_b, g_pre_b, g_post_b, w_mem_k, w_mem_v):
    bsz, seq, d = x_prompt.shape
    dbsz, dseq, _ = x_sample.shape
    mix = w_glu.shape[-1]
    memw = w_mem_k.shape[-1]
    heads = mix // SB_HEAD_DIM
    groups = mix // SSM_GROUP
    n_mem = mem_prompt.shape[1]
    depth = w_mem_k.shape[0]
    assert depth == 2 and w_in_a.shape[0] == 1 and w_in_b.shape[0] == 1
    assert seq % CHUNK == 0 and (bsz * seq // CHUNK) % LANES == 0 and seq % ATTN_BLOCK == 0
    assert dseq % (2 * SUB) == 0 and dseq <= TABLE_T and dbsz <= LANES
    nc = seq // CHUNK
    assert nc & (nc - 1) == 0

    row = lambda v: v.astype(f32).reshape(1, -1)
    wa = dict(
        w_in=w_in_a[0].astype(bf16),
        w_ut=w_in_a[0][:, :mix].T.astype(bf16),
        w_rest=w_in_a[0][:, mix:].astype(bf16),
        w_glu=w_glu[0].astype(bf16), b_glu=row(b_glu[0]),
        w_out=w_out_a[0].astype(bf16), g_pre=row(g_pre_a[0]), g_post=row(g_post_a[0]))
    qscale = math.log2(math.e) / math.sqrt(SB_HEAD_DIM)
    w_b = jnp.concatenate([w_in_b[0][:, :mix] * qscale, w_in_b[0][:, mix:]], axis=1).astype(bf16)
    w_kv_b = w_kv.astype(bf16)
    w_out_bb = w_out_b[0].astype(bf16)
    w_mem = jnp.concatenate([w_mem_k[0], w_mem_k[1], w_mem_v[0], w_mem_v[1]], axis=1).astype(bf16)

    memf, memb = _memkv(mem_prompt.reshape(bsz * n_mem, d), w_mem, memw)
    mem_k_prompt = memf[:depth].reshape(depth, bsz, n_mem, MEM_HEADS, memw // MEM_HEADS)
    mem_v_prompt = memf[depth:].reshape(depth, bsz, n_mem, MEM_HEADS, memw // MEM_HEADS)
    mkp = memb[:depth].reshape(depth, bsz, n_mem, memw)
    mvp = memb[depth:].reshape(depth, bsz, n_mem, memw)
    mks = cache_mem_k.reshape(depth, dbsz, n_mem, memw).astype(bf16)
    mvs = cache_mem_v.reshape(depth, dbsz, n_mem, memw).astype(bf16)

    lre, lim, tab_in = _ssm_param_tables(ssm_lam_re[0], ssm_lam_im[0], ssm_log_dt[0], ssm_b_re[0], ssm_b_im[0],
                                         ssm_c_re[0], ssm_c_im[0], ssm_d[0])
    tables = _s5_tables(*tab_in)

    n_p = bsz * seq
    bc = n_p // CHUNK
    h0_p = jnp.zeros((groups, 2 * SSM_STATE, bc), f32)
    x1_p, hfin_p = _layer_a(x_prompt.reshape(n_p, d), CHUNK, nc, h0_p, mkp[0], mvp[0], wa, tables, lre, lim, True)
    k_p, v_p, kvb_p, q_p, pr_p = _kvb(x1_p, row(g_kv), row(g_pre_b[0]), w_kv_b, w_b, mix, bsz)
    o_p = _sb_attn(q_p, kvb_p, bsz, seq, heads)
    rows_b = ATTN_BLOCK
    per_b = seq // rows_b
    tile = lambda w: pl.BlockSpec((rows_b, w), lambda i: (i, 0))
    y_p = _post(o_p, pr_p, x1_p, mkp[1], mvp[1], wa["w_glu"], wa["b_glu"], w_out_bb, row(g_post_b[0]),
                glu=False, segs=(((0, rows_b),),), grid=(n_p // rows_b,), rows=rows_b,
                y_spec=tile(mix), p_spec=tile(pr_p.shape[-1]), x_spec=tile(d),
                mem_spec=pl.BlockSpec((1, n_mem, memw), lambda i: (i // per_b, 0, 0)))

    n_s = dbsz * dseq
    st = state_ssm[0].astype(f32)
    h0_s = jnp.transpose(jnp.concatenate([st[..., 0], st[..., 1]], axis=-1), (1, 2, 0))
    h0_s = jnp.pad(h0_s, ((0, 0), (0, 0), (0, LANES - dbsz)))
    x1_s, hfin_s = _layer_a(x_sample.reshape(n_s, d), dseq, 1, h0_s, mks[0], mvs[0], wa, tables, lre, lim, False)
    k_s, v_s, kvb_s, q_s, pr_s = _kvb(x1_s, row(g_kv), row(g_pre_b[0]), w_kv_b, w_b, mix, dbsz)
    head_major = lambda a: jnp.transpose(a, (0, 2, 1, 3))
    o_s = _sb_attn_sample(q_s, kvb_s, head_major(cache_k), head_major(cache_v), dseq)
    whole = lambda w: pl.BlockSpec((n_s, w), lambda i: (0, 0))
    y_s = _post(o_s, pr_s, x1_s, mks[1], mvs[1], wa["w_glu"], wa["b_glu"], w_out_bb, row(g_post_b[0]),
                glu=False, segs=tuple(((b * dseq, dseq),) for b in range(dbsz)), grid=(1,), rows=n_s,
                y_spec=whole(mix), p_spec=whole(pr_s.shape[-1]), x_spec=whole(d),
                mem_spec=_resident(mks[1].shape))

    def ssm_out(hfin, lanes_idx):
        h = hfin[:, :, lanes_idx]
        h = jnp.transpose(h, (2, 0, 1))
        return jnp.stack([h[..., :SSM_STATE], h[..., SSM_STATE:]], axis=-1)[None]

    ssm_prompt = ssm_out(hfin_p, jnp.arange(bsz) * nc + (nc - 1)).astype(x_prompt.dtype)
    ssm_sample = ssm_out(hfin_s, jnp.arange(dbsz)).astype(state_ssm.dtype)
    return (y_p.reshape(bsz, seq, d), y_s.reshape(dbsz, dseq, d),
            head_major(k_p), head_major(v_p), head_major(k_s), head_major(v_s),
            ssm_prompt, ssm_sample, mem_k_prompt, mem_v_prompt)
```

```python
import functools
import math

import jax
import jax.numpy as jnp
from jax import lax
from jax.experimental import pallas as pl
from jax.experimental.pallas import tpu as pltpu

EPS = 1e-6
CHUNK = 64
SSM_GROUP = 16
SSM_STATE = 64
SB_HEAD_DIM = 128
MEM_HEADS = 4
SUB = 8
TABLE_T = 64
LANES = 128
ATTN_BLOCK = 256
MIB = 1024 * 1024

bf16 = jnp.bfloat16
f32 = jnp.float32

_NT = (((1,), (1,)), ((), ()))


def _params(vmem_mib, semantics):
    return pltpu.CompilerParams(vmem_limit_bytes=vmem_mib * MIB, dimension_semantics=semantics)


def _resident(shape):
    zeros = (0,) * len(shape)
    return pl.BlockSpec(shape, lambda *_: zeros, pipeline_mode=pl.Buffered(1))


def _rms_scale(x):
    return x * lax.rsqrt(jnp.mean(x * x, axis=-1, keepdims=True) + EPS)


def _sigmoid(x):
    return 1.0 / (1.0 + jnp.exp(-x))


def _gelu_tanh(x):
    c = math.sqrt(2.0 / math.pi)
    return 0.5 * x * (1.0 + jnp.tanh(c * (x + 0.044715 * (x * x * x))))


def _memkv_kernel(x_ref, w_ref, of_ref, ob_ref):
    acc = jnp.dot(x_ref[...].astype(bf16), w_ref[...], preferred_element_type=f32)
    width = of_ref.shape[-1]
    for j in range(of_ref.shape[0]):
        blk = acc[:, j * width:(j + 1) * width]
        of_ref[j] = blk
        ob_ref[j] = blk.astype(bf16)


def _memkv(mem, w_cat, width):
    rows, d = mem.shape
    nout = w_cat.shape[1] // width
    tm = 256
    return pl.pallas_call(
        _memkv_kernel,
        grid=(rows // tm,),
        in_specs=[pl.BlockSpec((tm, d), lambda i: (i, 0)), _resident(w_cat.shape)],
        out_specs=[pl.BlockSpec((nout, tm, width), lambda i: (0, i, 0)),
                   pl.BlockSpec((nout, tm, width), lambda i: (0, i, 0))],
        out_shape=[jax.ShapeDtypeStruct((nout, rows, width), f32),
                   jax.ShapeDtypeStruct((nout, rows, width), bf16)],
        compiler_params=_params(40, ("parallel",)),
        name="memkv",
    )(mem, w_cat)


def _offset_row_copies(x_hbm, buf, sem, step, slot, to_hbm=False):
    ns = buf.shape[1]
    out = []
    for s in range(ns):
        hbm = x_hbm.at[:, step * ns + s, :]
        vmem = buf.at[slot, s]
        out.append(pltpu.make_async_copy(vmem, hbm, sem.at[slot]) if to_hbm
                   else pltpu.make_async_copy(hbm, vmem, sem.at[slot]))
    return out


def _fetch_offset_rows(x_hbm, buf, sem):
    i = pl.program_id(0)

    @pl.when(i == 0)
    def _():
        for c in _offset_row_copies(x_hbm, buf, sem, 0, 0):
            c.start()

    @pl.when(i + 1 < pl.num_programs(0))
    def _():
        for c in _offset_row_copies(x_hbm, buf, sem, i + 1, (i + 1) % 2):
            c.start()

    slot = i % 2
    for c in _offset_row_copies(x_hbm, buf, sem, i, slot):
        c.wait()
    return slot


def _inproj_a_kernel(x_hbm, g_ref, wut_ref, wr_ref, ut_ref, p_ref, hn_ref, xbuf, xsem):
    _, ns, bc, _ = xbuf.shape
    slot = _fetch_offset_rows(x_hbm, xbuf, xsem)
    for s in range(ns):
        hn = (_rms_scale(xbuf[slot, s]) * g_ref[...]).astype(bf16)
        hn_ref[s * bc:(s + 1) * bc, :] = hn
        ut = lax.dot_general(wut_ref[...], hn, _NT, preferred_element_type=f32)
        ut_ref[s] = ut.reshape(ut_ref.shape[1:]).astype(bf16)
    step = 512
    for c in range(0, wr_ref.shape[1], step):
        r = jnp.dot(hn_ref[...], wr_ref[:, c:c + step], preferred_element_type=f32)
        p_ref[:, :, c:c + step] = r.astype(bf16).reshape(ns, bc, step)


def _inproj_a(x3, g, w_ut, w_rest, ns):
    bc, t, d = x3.shape
    mix = w_ut.shape[0]
    rest = w_rest.shape[1]
    groups = mix // SSM_GROUP
    return pl.pallas_call(
        _inproj_a_kernel,
        grid=(t // ns,),
        in_specs=[pl.BlockSpec(memory_space=pl.ANY),
                  _resident(g.shape), _resident(w_ut.shape), _resident(w_rest.shape)],
        out_specs=[pl.BlockSpec((ns, groups, SSM_GROUP, bc), lambda i: (i, 0, 0, 0)),
                   pl.BlockSpec((ns, bc, rest), lambda i: (i, 0, 0))],
        out_shape=[jax.ShapeDtypeStruct((t, groups, SSM_GROUP, bc), bf16),
                   jax.ShapeDtypeStruct((t, bc, rest), bf16)],
        scratch_shapes=[pltpu.VMEM((ns * bc, d), bf16), pltpu.VMEM((2, ns, bc, d), f32),
                        pltpu.SemaphoreType.DMA((2,))],
        compiler_params=_params(52, ("arbitrary",)),
        name="inproj_a",
    )(x3, g, w_ut, w_rest)


def _inproj_plain_kernel(x_ref, g_ref, wut_ref, wr_ref, u_ref, p_ref):
    hn = (_rms_scale(x_ref[...]) * g_ref[...]).astype(bf16)
    u_ref[...] = lax.dot_general(hn, wut_ref[...], _NT, preferred_element_type=f32).astype(bf16)
    p_ref[...] = jnp.dot(hn, wr_ref[...], preferred_element_type=f32).astype(bf16)


def _inproj_plain(x, g, w_ut, w_rest):
    rows, d = x.shape
    mix = w_ut.shape[0]
    rest = w_rest.shape[1]
    return pl.pallas_call(
        _inproj_plain_kernel,
        grid=(1,),
        in_specs=[pl.BlockSpec((rows, d), lambda i: (0, 0)), _resident(g.shape), _resident(w_ut.shape),
                  _resident(w_rest.shape)],
        out_specs=[pl.BlockSpec((rows, mix), lambda i: (0, 0)), pl.BlockSpec((rows, rest), lambda i: (0, 0))],
        out_shape=[jax.ShapeDtypeStruct((rows, mix), bf16), jax.ShapeDtypeStruct((rows, rest), bf16)],
        compiler_params=_params(40, ("arbitrary",)),
        name="inproj_plain",
    )(x, g, w_ut, w_rest)


def _s5_tables_kernel(lre_ref, lim_ref, caa_ref, cab_ref, ba_ref, bb_ref, dd_ref,
                      tt_ref, ca_ref, wb_ref, pw_ref, cas_ref, wbs_ref):
    t_len = TABLE_T
    rows = t_len * SSM_GROUP
    lre = lre_ref[...]
    lim = lim_ref[...]
    kk = lax.broadcasted_iota(jnp.int32, (t_len, LANES), 0).astype(f32)

    def powers(k):
        mag = jnp.exp(lre * k)
        th = lim * k
        return mag * jnp.cos(th), mag * jnp.sin(th)

    pr1, pi1 = powers(kk + 1.0)
    pr0, pi0 = powers((t_len - 1.0) - kk)
    pw_ref[...] = jnp.concatenate([pr1, pi1, pr0, pi0], axis=1)

    def tile_rows(v, n):
        return jnp.broadcast_to(v[None], (n,) + v.shape).reshape(n * v.shape[0], v.shape[1])

    caa = caa_ref[...]
    cab = cab_ref[...]
    ba = ba_ref[...]
    bb = bb_ref[...]
    for t in range(t_len):
        pw = jnp.broadcast_to(pw_ref[t:t + 1, :], (SSM_GROUP, 4 * LANES))
        r0 = t * SSM_GROUP
        cas_ref[r0:r0 + SSM_GROUP, :] = pw[:, 0:128] * caa + pw[:, 128:256] * cab
        wbs_ref[r0:r0 + SSM_GROUP, :] = pw[:, 256:384] * ba + pw[:, 384:512] * bb
    ca = cas_ref[...]
    wbt = wbs_ref[...]
    ca_ref[...] = ca.astype(bf16)
    wb_ref[...] = wbt.T.astype(bf16)

    blk = SUB * SSM_GROUP
    nblk = rows // blk
    rt = wbt[rows - blk:, :]
    m = [None] * nblk
    for d in range(1, nblk):
        m[d] = lax.dot_general(ca[(d - 1) * blk:d * blk, :], rt, _NT, preferred_element_type=f32,
                               precision=lax.Precision.HIGHEST)
    ca0 = jnp.concatenate([caa, ca[:blk - SSM_GROUP, :]], axis=0)
    kj = lax.dot_general(ca0, tile_rows(ba, SUB), _NT, preferred_element_type=f32,
                         precision=lax.Precision.HIGHEST)
    lane = lax.broadcasted_iota(jnp.int32, (SSM_GROUP, LANES), 1)
    hrow = lax.broadcasted_iota(jnp.int32, (SSM_GROUP, LANES), 0)
    s0_lane = lane // SSM_GROUP
    skip = jnp.where(lane % SSM_GROUP == hrow, dd_ref[...], 0.0)
    kjs = [kj[j * SSM_GROUP:(j + 1) * SSM_GROUP, :] for j in range(SUB)]
    kjs[0] = kjs[0] + skip
    drows = []
    for t0 in range(SUB):
        acc = jnp.zeros((SSM_GROUP, LANES), f32)
        for j in range(t0 + 1):
            acc = acc + jnp.where(s0_lane == t0 - j, kjs[j], 0.0)
        drows.append(acc)
    m[0] = jnp.concatenate(drows, axis=0)
    zero = jnp.zeros((blk, blk), f32)
    r1 = jnp.concatenate([m[d] for d in range(nblk - 1, -1, -1)], axis=1)
    r0 = jnp.concatenate([m[d] for d in range(nblk - 2, -1, -1)] + [zero], axis=1)
    tt_ref[...] = jnp.concatenate([r0, r1], axis=0).astype(bf16)


def _s5_tables(lre2, lim2, caa, cab, ba, bb, dd):
    groups = lre2.shape[0]
    rows = TABLE_T * SSM_GROUP

    def gspec(shape):
        return pl.BlockSpec((None,) + shape, lambda g: (g,) + (0,) * len(shape))

    return pl.pallas_call(
        _s5_tables_kernel,
        grid=(groups,),
        in_specs=[gspec((1, LANES)), gspec((1, LANES)), gspec((SSM_GROUP, LANES)), gspec((SSM_GROUP, LANES)),
                  gspec((SSM_GROUP, LANES)), gspec((SSM_GROUP, LANES)), gspec((SSM_GROUP, 1))],
        out_specs=[gspec((2 * SUB * SSM_GROUP, rows)), gspec((rows, LANES)), gspec((LANES, rows))],
        out_shape=[jax.ShapeDtypeStruct((groups, 2 * SUB * SSM_GROUP, rows), bf16),
                   jax.ShapeDtypeStruct((groups, rows, LANES), bf16),
                   jax.ShapeDtypeStruct((groups, LANES, rows), bf16)],
        scratch_shapes=[pltpu.VMEM((TABLE_T, 4 * LANES), f32), pltpu.VMEM((rows, LANES), f32),
                        pltpu.VMEM((rows, LANES), f32)],
        compiler_params=_params(40, ("parallel",)),
        name="s5_tables",
    )(lre2, lim2, caa, cab, ba, bb, dd)


def _cmul(ar, ai, x):
    half = x.shape[0] // 2
    xr = x[:half]
    xi = x[half:]
    return jnp.concatenate([ar * xr - ai * xi, ar * xi + ai * xr], axis=0)


def _s5_kernel(ut_ref, tt_ref, ca_ref, wb_ref, ap_ref, h0_ref, y_ref, hfin_ref, yt_ref, *, nc):
    g8 = pl.program_id(1)
    t_len, _, bc = ut_ref.shape
    rows = t_len * SSM_GROUP
    pair = 2 * SUB * SSM_GROUP
    table_rows = tt_ref.shape[1]
    z = ut_ref[...].reshape(rows, bc)
    ys = []
    for t2 in range(rows // pair):
        kk = pair * (t2 + 1)
        ys.append(jnp.dot(tt_ref[:, table_rows - kk:], z[:kk], preferred_element_type=f32))
    y = jnp.concatenate(ys, axis=0) if len(ys) > 1 else ys[0]
    state = jnp.dot(wb_ref[:, table_rows - rows:], z, preferred_element_type=f32)
    ap = ap_ref[...]
    h0 = h0_ref[...]
    state = state + _cmul(ap[:, 0:1], ap[:, 1:2], h0)
    lane = lax.broadcasted_iota(jnp.int32, state.shape, 1) % nc
    step = 0
    while (1 << step) < nc:
        sh = 1 << step
        shifted = jnp.where(lane >= sh, pltpu.roll(state, sh, axis=1), 0.0)
        state = state + _cmul(ap[:, 2 + 2 * step:3 + 2 * step], ap[:, 3 + 2 * step:4 + 2 * step], shifted)
        step += 1
    hfin_ref[...] = state
    if nc > 1:
        h_in = jnp.where(lane >= 1, pltpu.roll(state, 1, axis=1), 0.0) + h0
    else:
        h_in = h0
    y = y + jnp.dot(ca_ref[:rows, :], h_in.astype(bf16), preferred_element_type=f32)
    off = pl.multiple_of(g8 * SSM_GROUP, SSM_GROUP)
    yt_ref[:, pl.ds(off, SSM_GROUP), :] = y.reshape(t_len, SSM_GROUP, bc)

    @pl.when(g8 == pl.num_programs(1) - 1)
    def _():
        for t in range(t_len):
            y_ref[t] = yt_ref[t].T.astype(bf16)


def _s5(ut4, tt, ca, wb, apow, h0, nc):
    t_len, groups, _, bc = ut4.shape
    per = LANES // SSM_GROUP
    rows = t_len * SSM_GROUP
    last = TABLE_T * SSM_GROUP // rows - 1
    assert (last + 1) * rows == TABLE_T * SSM_GROUP
    mix = groups * SSM_GROUP
    ncol = apow.shape[-1]
    return pl.pallas_call(
        functools.partial(_s5_kernel, nc=nc),
        grid=(groups // per, per),
        in_specs=[pl.BlockSpec((t_len, None, SSM_GROUP, bc), lambda G, g: (0, G * per + g, 0, 0)),
                  pl.BlockSpec((None, 2 * SUB * SSM_GROUP, rows), lambda G, g: (G * per + g, 0, last)),
                  pl.BlockSpec((None, rows, LANES), lambda G, g: (G * per + g, 0, 0)),
                  pl.BlockSpec((None, LANES, rows), lambda G, g: (G * per + g, 0, last)),
                  pl.BlockSpec((None, SSM_STATE, ncol), lambda G, g: (G * per + g, 0, 0)),
                  pl.BlockSpec((None, 2 * SSM_STATE, bc), lambda G, g: (G * per + g, 0, 0))],
        out_specs=[pl.BlockSpec((t_len, bc, LANES), lambda G, g: (0, 0, G)),
                   pl.BlockSpec((None, 2 * SSM_STATE, bc), lambda G, g: (G * per + g, 0, 0))],
        out_shape=[jax.ShapeDtypeStruct((t_len, bc, mix), bf16),
                   jax.ShapeDtypeStruct((groups, 2 * SSM_STATE, bc), f32)],
        scratch_shapes=[pltpu.VMEM((t_len, LANES, bc), f32)],
        compiler_params=_params(48, ("parallel", "arbitrary")),
        name="s5",
    )(ut4, tt, ca, wb, apow, h0)


def _ld(ref, start, n, c0, c1):
    if len(ref.shape) == 3:
        s, r = divmod(start, ref.shape[1])
        return ref[s, r:r + n, c0:c1]
    return ref[start:start + n, c0:c1]


def _post_kernel(y_ref, p_ref, x_ref, mk_ref, mv_ref, wglu_ref, bglu_ref, wout_ref, gpost_ref,
                 o_ref, cat_ref, *dma, glu, segs, mem_scale):
    mix = y_ref.shape[-1]
    memw = mk_ref.shape[-1]
    hd = memw // MEM_HEADS
    rows = cat_ref.shape[0]
    if dma:
        xbuf, xsem, obuf, osem = dma
        slot = _fetch_offset_rows(x_ref, xbuf, xsem)
    y = y_ref[...].reshape(rows, mix).astype(f32)
    gate = p_ref[:, :mix] if len(p_ref.shape) == 2 else p_ref[:, :, :mix].reshape(rows, mix)
    gate = gate.astype(f32)
    if glu:
        y = _gelu_tanh(y)
        zz = jnp.dot(y.astype(bf16), wglu_ref[...], preferred_element_type=f32) + bglu_ref[...]
        y = y * _sigmoid(zz)
    main = (y * (gate * _sigmoid(gate))).astype(bf16)
    out = jnp.dot(main, wout_ref[:mix, :], preferred_element_type=f32)
    for b, pieces in enumerate(segs):
        for h in range(MEM_HEADS):
            cq = mix + h * hd
            cg = mix + memw + h * hd
            q = jnp.concatenate([_ld(p_ref, st, n, cq, cq + hd) for st, n in pieces], axis=0)
            mg = jnp.concatenate([_ld(p_ref, st, n, cg, cg + hd) for st, n in pieces], axis=0).astype(f32)
            k = mk_ref[b, :, h * hd:(h + 1) * hd]
            v = mv_ref[b, :, h * hd:(h + 1) * hd]
            s = lax.dot_general(q, k, _NT, preferred_element_type=f32) * mem_scale
            e = jnp.exp(s - jnp.max(s, axis=-1, keepdims=True))
            p = e / jnp.sum(e, axis=-1, keepdims=True)
            o = jnp.dot(p.astype(bf16), v, preferred_element_type=f32)
            om = (o * (mg * _sigmoid(mg))).astype(bf16)
            off = 0
            for st, n in pieces:
                cat_ref[st:st + n, h * hd:(h + 1) * hd] = om[off:off + n]
                off += n
    out = out + jnp.dot(cat_ref[...], wout_ref[mix:, :], preferred_element_type=f32)
    d = out.shape[-1]
    branch = _rms_scale(out) * gpost_ref[...]
    if not dma:
        o_ref[...] = (x_ref[...].reshape(rows, d) + branch).reshape(o_ref.shape)
        return
    i = pl.program_id(0)
    last = pl.num_programs(0) - 1

    @pl.when(i >= 2)
    def _():
        for c in _offset_row_copies(o_ref, obuf, osem, i - 2, slot, to_hbm=True):
            c.wait()

    obuf[slot] = (xbuf[slot].reshape(rows, d) + branch).reshape(obuf.shape[1:])
    for c in _offset_row_copies(o_ref, obuf, osem, i, slot, to_hbm=True):
        c.start()

    @pl.when(i == last)
    def _():
        for c in _offset_row_copies(o_ref, obuf, osem, i, slot, to_hbm=True):
            c.wait()

    @pl.when((i == last) & (i >= 1))
    def _():
        for c in _offset_row_copies(o_ref, obuf, osem, i - 1, 1 - slot, to_hbm=True):
            c.wait()


def _post(y, p, x, mk, mv, wglu, bglu, wout, gpost, *, glu, segs, grid, y_spec, p_spec, x_spec, mem_spec, rows,
          offsets_per_step=None):
    mem_scale = 1.0 / math.sqrt(mk.shape[-1] // MEM_HEADS)
    scratch = [pltpu.VMEM((rows, mk.shape[-1]), bf16)]
    semantics = ("parallel",)
    if x_spec is None:
        x_spec = pl.BlockSpec(memory_space=pl.ANY)
        buf = pltpu.VMEM((2, offsets_per_step, x.shape[0], x.shape[2]), f32)
        scratch += [buf, pltpu.SemaphoreType.DMA((2,)), buf, pltpu.SemaphoreType.DMA((2,))]
        semantics = ("arbitrary",)
    return pl.pallas_call(
        functools.partial(_post_kernel, glu=glu, segs=segs, mem_scale=mem_scale),
        grid=grid,
        in_specs=[y_spec, p_spec, x_spec, mem_spec, mem_spec,
                  _resident(wglu.shape), _resident(bglu.shape), _resident(wout.shape), _resident(gpost.shape)],
        out_specs=x_spec,
        out_shape=jax.ShapeDtypeStruct(x.shape, f32),
        scratch_shapes=scratch,
        compiler_params=_params(52, semantics),
        name="post_glu" if glu else "post",
    )(y, p, x, mk, mv, wglu, bglu, wout, gpost)


def _kvb_kernel(x_ref, gkv_ref, gb_ref, wkv_ref, wb_ref, k_ref, v_ref, kvb_ref, q_ref, pr_ref):
    xs = _rms_scale(x_ref[...])
    hkv = (xs * gkv_ref[...]).astype(bf16)
    hb = (xs * gb_ref[...]).astype(bf16)
    nst, heads, rps, hd = k_ref.shape
    mix = heads * hd
    step = 512
    for c in range(0, wkv_ref.shape[1], step):
        r = jnp.dot(hkv, wkv_ref[:, c:c + step], preferred_element_type=f32)
        kvb_ref[:, c:c + step] = r.astype(bf16)
        for j in range(step // hd):
            h = (c % mix) // hd + j
            (k_ref if c < mix else v_ref)[:, h] = r[:, j * hd:(j + 1) * hd].reshape(nst, rps, hd)
    for c in range(0, wb_ref.shape[1], step):
        r = jnp.dot(hb, wb_ref[:, c:c + step], preferred_element_type=f32).astype(bf16)
        if c < mix:
            q_ref[:, c:c + step] = r
        else:
            pr_ref[:, c - mix:c - mix + step] = r


def _kvb(x, gkv, gb, wkv, wb, mix, streams):
    n, d = x.shape
    seq = n // streams
    tm = min(256, n)
    rps = min(tm, seq)
    assert tm % rps == 0 and seq % rps == 0 and rps % 8 == 0
    per_stream = seq // rps
    rest = wb.shape[1] - mix
    heads = mix // SB_HEAD_DIM
    row = lambda w: pl.BlockSpec((tm, w), lambda i: (i, 0))
    head_major = pl.BlockSpec((tm // rps, heads, rps, SB_HEAD_DIM), lambda i: (i // per_stream, 0, i % per_stream, 0))
    kv_shape = jax.ShapeDtypeStruct((streams, heads, seq, SB_HEAD_DIM), f32)
    return pl.pallas_call(
        _kvb_kernel,
        grid=(n // tm,),
        in_specs=[row(d), _resident(gkv.shape), _resident(gb.shape), _resident(wkv.shape), _resident(wb.shape)],
        out_specs=[head_major, head_major, row(2 * mix), row(mix), row(rest)],
        out_shape=[kv_shape, kv_shape,
                   jax.ShapeDtypeStruct((n, 2 * mix), bf16), jax.ShapeDtypeStruct((n, mix), bf16),
                   jax.ShapeDtypeStruct((n, rest), bf16)],
        compiler_params=_params(56, ("parallel",)),
        name="kvb",
    )(x, gkv, gb, wkv, wb)


_MASKED = -1e30


def _sb_softplus_tri(z, ntri, mask):
    sp = jnp.maximum(z, 0.0) + jnp.log2(1.0 + jnp.exp2(-jnp.abs(z)))
    if mask is not None:
        sp = jnp.where(mask, sp, 0.0)
    inner = jnp.dot(sp.astype(bf16), ntri, preferred_element_type=f32)
    t = (z - sp) + inner
    if mask is not None:
        t = jnp.where(mask, t, _MASKED)
    return t, inner[:, 0:1] - sp[:, 0:1]


def _sb_apply(t, d, v, carry, acc):
    w = jnp.exp2(t + carry)
    return acc + jnp.dot(w.astype(bf16), v, preferred_element_type=f32), carry + d


def _sb_block(q, k, v, ntri, carry, acc, mask):
    z = lax.dot_general(q, k, _NT, preferred_element_type=f32)
    t, d = _sb_softplus_tri(z, ntri, mask)
    return _sb_apply(t, d, v, carry, acc)


def _tri_and_mask(n):
    r = lax.broadcasted_iota(jnp.int32, (n, n), 0)
    c = lax.broadcasted_iota(jnp.int32, (n, n), 1)
    return jnp.where(r > c, -1.0, 0.0).astype(bf16), c < r


def _sb_attn_kernel(q_ref, k_ref, v_ref, o_ref, acc_ref, car_ref, t_ref, d_ref, *, nq):
    tq = ATTN_BLOCK
    i0 = pl.program_id(2) * nq
    ntri, causal = _tri_and_mask(tq)

    def rows(ref, kb):
        off = kb * tq if isinstance(kb, int) else pl.multiple_of(kb * tq, tq)
        return ref[pl.ds(off, tq), :]

    def pieces(r_lo, diag):
        out = []
        r = r_lo
        if diag:
            out.append((r * tq, (r + 1) * tq, causal))
            r += 1
        while r < nq:
            n = min(2, nq - r)
            out.append((r * tq, (r + n) * tq, None))
            r += n
        return out

    def step(prev, cur):
        zs = []
        if cur is not None:
            k = rows(k_ref, cur[0])
            for a, b, mask in pieces(cur[1], cur[2]):
                zs.append((a, b, mask, lax.dot_general(q_ref[a:b, :], k, _NT, preferred_element_type=f32)))
        if prev is not None:
            v = rows(v_ref, prev[0])
            for a, b, _ in pieces(prev[1], False):
                acc, carry = _sb_apply(t_ref[a:b, :], d_ref[a:b, :], v, car_ref[a:b, :], acc_ref[a:b, :])
                acc_ref[a:b, :] = acc
                car_ref[a:b, :] = carry
        for a, b, mask, z in zs:
            t, d = _sb_softplus_tri(z, ntri, mask)
            t_ref[a:b, :] = t
            d_ref[a:b, :] = d

    acc_ref[...] = jnp.zeros(acc_ref.shape, f32)
    car_ref[...] = jnp.zeros(car_ref.shape, f32)
    prev = None
    for p in range(nq):
        cur = (i0 + nq - 1 - p, nq - 1 - p, True)
        step(prev, cur)
        prev = cur

    def body(j, c):
        kb = i0 - 1 - 2 * j
        step((kb + 1, 0, False), (kb, 0, False))
        step((kb, 0, False), (kb - 1, 0, False))
        return c

    lax.fori_loop(0, i0 // 2, body, 0)
    step((0, 0, False), None)
    o_ref[...] = acc_ref[...].astype(bf16)


def _sb_attn(q, kvb, bsz, seq, heads):
    tq = ATTN_BLOCK
    hd = SB_HEAD_DIM
    nq = 4
    assert seq % (tq * nq) == 0
    steps = seq // (tq * nq)
    return pl.pallas_call(
        functools.partial(_sb_attn_kernel, nq=nq),
        grid=(bsz, heads, steps),
        in_specs=[pl.BlockSpec((nq * tq, hd), lambda b, h, i: (b * steps + i, h)),
                  pl.BlockSpec((seq, hd), lambda b, h, i: (b, h)),
                  pl.BlockSpec((seq, hd), lambda b, h, i: (b, heads + h))],
        out_specs=pl.BlockSpec((nq * tq, hd), lambda b, h, i: (b * steps + i, h)),
        out_shape=jax.ShapeDtypeStruct(q.shape, bf16),
        scratch_shapes=[pltpu.VMEM((nq * tq, hd), f32), pltpu.VMEM((nq * tq, 1), f32),
                        pltpu.VMEM((nq * tq, tq), f32), pltpu.VMEM((nq * tq, 1), f32)],
        compiler_params=_params(40, ("parallel", "parallel", "arbitrary")),
        name="sb_attn",
    )(q, kvb, kvb)


def _sb_attn_sample_kernel(q_ref, kn_ref, vn_ref, kc_ref, vc_ref, o_ref):
    tq = q_ref.shape[0]
    past = kc_ref.shape[0]
    blk = min(ATTN_BLOCK, past)
    q = q_ref[...]
    tri_n, causal = _tri_and_mask(tq)
    acc = jnp.zeros((tq, q_ref.shape[1]), f32)
    carry = jnp.zeros((tq, 1), f32)
    acc, carry = _sb_block(q, kn_ref[...], vn_ref[...], tri_n, carry, acc, causal)
    tri_p, _ = _tri_and_mask(blk)
    for j in range(past // blk - 1, -1, -1):
        kb = kc_ref[j * blk:(j + 1) * blk, :].astype(bf16)
        vb = vc_ref[j * blk:(j + 1) * blk, :].astype(bf16)
        acc, carry = _sb_block(q, kb, vb, tri_p, carry, acc, None)
    o_ref[...] = acc.astype(bf16)


def _sb_attn_sample(q, kvb, cache_k, cache_v, tq):
    bsz, heads, past, hd = cache_k.shape
    cache = pl.BlockSpec((None, None, past, hd), lambda b, h: (b, h, 0, 0))
    return pl.pallas_call(
        _sb_attn_sample_kernel,
        grid=(bsz, heads),
        in_specs=[pl.BlockSpec((tq, hd), lambda b, h: (b, h)),
                  pl.BlockSpec((tq, hd), lambda b, h: (b, h)),
                  pl.BlockSpec((tq, hd), lambda b, h: (b, heads + h)),
                  cache, cache],
        out_specs=pl.BlockSpec((tq, hd), lambda b, h: (b, h)),
        out_shape=jax.ShapeDtypeStruct(q.shape, bf16),
        compiler_params=_params(40, ("parallel", "parallel")),
        name="sb_attn_sample",
    )(q, kvb, kvb, cache_k, cache_v)


def _ssm_param_tables(lam_re, lam_im, log_dt, b_re, b_im, c_re, c_im, dvec):
    dt = jnp.exp(log_dt.astype(f32))[:, None]
    lr = lam_re.astype(f32)
    li = lam_im.astype(f32)
    lre = lr * dt
    lim = li * dt
    mag = jnp.exp(lre)
    nr = mag * jnp.cos(lim) - 1.0
    ni = mag * jnp.sin(lim)
    den = lr * lr + li * li
    fr = ((nr * lr + ni * li) / den)[..., None]
    fi = ((ni * lr - nr * li) / den)[..., None]
    b_r = b_re.astype(f32)
    b_i = b_im.astype(f32)
    br = jnp.swapaxes(fr * b_r - fi * b_i, 1, 2)
    bi = jnp.swapaxes(fr * b_i + fi * b_r, 1, 2)
    cr = c_re.astype(f32)
    ci = c_im.astype(f32)
    dup = lambda v: jnp.concatenate([v, v], axis=-1)
    lre2 = dup(lre)[:, None, :]
    lim2 = dup(lim)[:, None, :]
    caa = jnp.concatenate([cr, -ci], axis=-1)
    cab = jnp.concatenate([-ci, -cr], axis=-1)
    ba = jnp.concatenate([br, bi], axis=-1)
    bb = jnp.concatenate([-bi, br], axis=-1)
    dd = dvec.astype(f32)[:, :, None]
    return lre, lim, (lre2, lim2, caa, cab, ba, bb, dd)


def _chunk_powers(lre, lim, t_len, nc):
    cols = []
    exps = [t_len]
    j = 0
    while (1 << j) < nc:
        exps.append(t_len * (1 << j))
        j += 1
    for e in exps:
        mag = jnp.exp(lre * e)
        cols += [mag * jnp.cos(lim * e), mag * jnp.sin(lim * e)]
    return jnp.stack(cols, axis=-1)


def _layer_a(x, t_len, nc, h0_lanes, mk, mv, wa, tables, lre, lim, prompt):
    n, d = x.shape
    mix = wa["w_glu"].shape[0]
    groups = mix // SSM_GROUP
    tt, ca, wb = tables
    apow = _chunk_powers(lre, lim, t_len, nc)
    if prompt:
        bc = n // t_len
        ns = 2
        x3 = x.reshape(bc, t_len, d)
        ut4, p = _inproj_a(x3, wa["g_pre"], wa["w_ut"], wa["w_rest"], ns)
        y, hfin = _s5(ut4, tt, ca, wb, apow, h0_lanes, nc)
        streams = bc // nc
        rest = p.shape[-1]
        segs = tuple(tuple((s * bc + b * nc, nc) for s in range(ns)) for b in range(streams))
        blk = lambda w: pl.BlockSpec((ns, bc, w), lambda i: (i, 0, 0))
        x1 = _post(y, p, x3, mk, mv, wa["w_glu"], wa["b_glu"], wa["w_out"], wa["g_post"],
                   glu=True, segs=segs, grid=(t_len // ns,), rows=ns * bc, offsets_per_step=ns,
                   y_spec=blk(mix), p_spec=blk(rest), x_spec=None, mem_spec=_resident(mk.shape))
        return x1.reshape(n, d), hfin
    streams = n // t_len
    lanes = h0_lanes.shape[-1]
    u, p = _inproj_plain(x, wa["g_pre"], wa["w_ut"], wa["w_rest"])
    ut4 = jnp.transpose(u.reshape(streams, t_len, groups, SSM_GROUP), (1, 2, 3, 0))
    ut4 = jnp.pad(ut4, ((0, 0), (0, 0), (0, 0), (0, lanes - streams)))
    y3, hfin = _s5(ut4, tt, ca, wb, apow, h0_lanes, nc)
    y = jnp.transpose(y3[:, :streams, :], (1, 0, 2)).reshape(n, mix)
    rest = p.shape[-1]
    whole = lambda w: pl.BlockSpec((n, w), lambda i: (0, 0))
    x1 = _post(y, p, x, mk, mv, wa["w_glu"], wa["b_glu"], wa["w_out"], wa["g_post"],
               glu=True, segs=tuple(((b * t_len, t_len),) for b in range(streams)), grid=(1,), rows=n,
               y_spec=whole(mix), p_spec=whole(rest), x_spec=whole(d), mem_spec=_resident(mk.shape))
    return x1, hfin


def kernel(x_prompt, x_sample, cache_k, cache_v, cache_mem_k, cache_mem_v, state_ssm, mem_prompt, w_in_a, w_out_a, g_pre_a, g_post_a, ssm_lam_re, ssm_lam_im, ssm_log_dt, ssm_b_re, ssm_b_im, ssm_c_re, ssm_c_im, ssm_d, w_glu, b_glu, g_kv, w_kv, w_in_b, w_out_b, g_pre_b, g_post_b, w_mem_k, w_mem_v):
    bsz, seq, d = x_prompt.shape
    dbsz, dseq, _ = x_sample.shape
    mix = w_glu.shape[-1]
    memw = w_mem_k.shape[-1]
    heads = mix // SB_HEAD_DIM
    groups = mix // SSM_GROUP
    n_mem = mem_prompt.shape[1]
    depth = w_mem_k.shape[0]
    assert depth == 2 and w_in_a.shape[0] == 1 and w_in_b.shape[0] == 1
    assert seq % CHUNK == 0 and (bsz * seq // CHUNK) % LANES == 0 and seq % ATTN_BLOCK == 0
    assert dseq % (2 * SUB) == 0 and dseq <= TABLE_T and dbsz <= LANES
    nc = seq // CHUNK
    assert nc & (nc - 1) == 0

    row = lambda v: v.astype(f32).reshape(1, -1)
    wa = dict(
        w_ut=w_in_a[0][:, :mix].T.astype(bf16),
        w_rest=w_in_a[0][:, mix:].astype(bf16),
        w_glu=w_glu[0].astype(bf16), b_glu=row(b_glu[0]),
        w_out=w_out_a[0].astype(bf16), g_pre=row(g_pre_a[0]), g_post=row(g_post_a[0]))
    qscale = math.log2(math.e) / math.sqrt(SB_HEAD_DIM)
    w_b = jnp.concatenate([w_in_b[0][:, :mix] * qscale, w_in_b[0][:, mix:]], axis=1).astype(bf16)
    w_kv_b = w_kv.astype(bf16)
    w_out_bb = w_out_b[0].astype(bf16)
    w_mem = jnp.concatenate([w_mem_k[0], w_mem_k[1], w_mem_v[0], w_mem_v[1]], axis=1).astype(bf16)

    memf, memb = _memkv(mem_prompt.reshape(bsz * n_mem, d), w_mem, memw)
    mem_k_prompt = memf[:depth].reshape(depth, bsz, n_mem, MEM_HEADS, memw // MEM_HEADS)
    mem_v_prompt = memf[depth:].reshape(depth, bsz, n_mem, MEM_HEADS, memw // MEM_HEADS)
    mkp = memb[:depth].reshape(depth, bsz, n_mem, memw)
    mvp = memb[depth:].reshape(depth, bsz, n_mem, memw)
    mks = cache_mem_k.reshape(depth, dbsz, n_mem, memw).astype(bf16)
    mvs = cache_mem_v.reshape(depth, dbsz, n_mem, memw).astype(bf16)

    lre, lim, tab_in = _ssm_param_tables(ssm_lam_re[0], ssm_lam_im[0], ssm_log_dt[0], ssm_b_re[0], ssm_b_im[0],
                                         ssm_c_re[0], ssm_c_im[0], ssm_d[0])
    tables = _s5_tables(*tab_in)

    n_p = bsz * seq
    bc = n_p // CHUNK
    h0_p = jnp.zeros((groups, 2 * SSM_STATE, bc), f32)
    x1_p, hfin_p = _layer_a(x_prompt.reshape(n_p, d), CHUNK, nc, h0_p, mkp[0], mvp[0], wa, tables, lre, lim, True)
    k_p, v_p, kvb_p, q_p, pr_p = _kvb(x1_p, row(g_kv), row(g_pre_b[0]), w_kv_b, w_b, mix, bsz)
    o_p = _sb_attn(q_p, kvb_p, bsz, seq, heads)
    rows_b = 2 * ATTN_BLOCK if seq % (2 * ATTN_BLOCK) == 0 else ATTN_BLOCK
    per_b = seq // rows_b
    tile = lambda w: pl.BlockSpec((rows_b, w), lambda i: (i, 0))
    y_p = _post(o_p, pr_p, x1_p, mkp[1], mvp[1], wa["w_glu"], wa["b_glu"], w_out_bb, row(g_post_b[0]),
                glu=False, segs=(((0, rows_b),),), grid=(n_p // rows_b,), rows=rows_b,
                y_spec=tile(mix), p_spec=tile(pr_p.shape[-1]), x_spec=tile(d),
                mem_spec=pl.BlockSpec((1, n_mem, memw), lambda i: (i // per_b, 0, 0)))

    n_s = dbsz * dseq
    st = state_ssm[0].astype(f32)
    h0_s = jnp.transpose(jnp.concatenate([st[..., 0], st[..., 1]], axis=-1), (1, 2, 0))
    h0_s = jnp.pad(h0_s, ((0, 0), (0, 0), (0, LANES - dbsz)))
    x1_s, hfin_s = _layer_a(x_sample.reshape(n_s, d), dseq, 1, h0_s, mks[0], mvs[0], wa, tables, lre, lim, False)
    k_s, v_s, kvb_s, q_s, pr_s = _kvb(x1_s, row(g_kv), row(g_pre_b[0]), w_kv_b, w_b, mix, dbsz)
    head_major = lambda a: jnp.transpose(a, (0, 2, 1, 3))
    o_s = _sb_attn_sample(q_s, kvb_s, head_major(cache_k), head_major(cache_v), dseq)
    whole = lambda w: pl.BlockSpec((n_s, w), lambda i: (0, 0))
    y_s = _post(o_s, pr_s, x1_s, mks[1], mvs[1], wa["w_glu"], wa["b_glu"], w_out_bb, row(g_post_b[0]),
                glu=False, segs=tuple(((b * dseq, dseq),) for b in range(dbsz)), grid=(1,), rows=n_s,
                y_spec=whole(mix), p_spec=whole(pr_s.shape[-1]), x_spec=whole(d),
                mem_spec=_resident(mks[1].shape))

    def ssm_out(hfin, lanes_idx):
        h = hfin[:, :, lanes_idx]
        h = jnp.transpose(h, (2, 0, 1))
        return jnp.stack([h[..., :SSM_STATE], h[..., SSM_STATE:]], axis=-1)[None]

    ssm_prompt = ssm_out(hfin_p, jnp.arange(bsz) * nc + (nc - 1)).astype(x_prompt.dtype)
    ssm_sample = ssm_out(hfin_s, jnp.arange(dbsz)).astype(state_ssm.dtype)
    return (y_p.reshape(bsz, seq, d), y_s.reshape(dbsz, dseq, d),
            head_major(k_p), head_major(v_p), head_major(k_s), head_major(v_s),
            ssm_prompt, ssm_sample, mem_k_prompt, mem_v_prompt)
```

```python
import functools
import math

import jax
import jax.numpy as jnp
from jax import lax
from jax.experimental import pallas as pl
from jax.experimental.pallas import tpu as pltpu

EPS = 1e-6
CHUNK = 64
SSM_GROUP = 16
SSM_STATE = 64
SB_HEAD_DIM = 128
MEM_HEADS = 4
SUB = 8
TABLE_T = 64
LANES = 128
ATTN_BLOCK = 256
MIB = 1024 * 1024

bf16 = jnp.bfloat16
f32 = jnp.float32

_NT = (((1,), (1,)), ((), ()))


def _params(vmem_mib, semantics):
    return pltpu.CompilerParams(vmem_limit_bytes=vmem_mib * MIB, dimension_semantics=semantics)


def _resident(shape):
    zeros = (0,) * len(shape)
    return pl.BlockSpec(shape, lambda *_: zeros, pipeline_mode=pl.Buffered(1))


def _rms_scale(x):
    return x * lax.rsqrt(jnp.mean(x * x, axis=-1, keepdims=True) + EPS)


def _sigmoid(x):
    return 1.0 / (1.0 + jnp.exp(-x))


def _gelu_tanh(x):
    c = math.sqrt(2.0 / math.pi)
    return 0.5 * x * (1.0 + jnp.tanh(c * (x + 0.044715 * (x * x * x))))


def _memkv_kernel(x_ref, w_ref, of_ref, ob_ref):
    acc = jnp.dot(x_ref[...].astype(bf16), w_ref[...], preferred_element_type=f32)
    width = of_ref.shape[-1]
    for j in range(of_ref.shape[0]):
        blk = acc[:, j * width:(j + 1) * width]
        of_ref[j] = blk
        ob_ref[j] = blk.astype(bf16)


def _memkv(mem, w_cat, width):
    rows, d = mem.shape
    nout = w_cat.shape[1] // width
    tm = 256
    return pl.pallas_call(
        _memkv_kernel,
        grid=(rows // tm,),
        in_specs=[pl.BlockSpec((tm, d), lambda i: (i, 0)), _resident(w_cat.shape)],
        out_specs=[pl.BlockSpec((nout, tm, width), lambda i: (0, i, 0)),
                   pl.BlockSpec((nout, tm, width), lambda i: (0, i, 0))],
        out_shape=[jax.ShapeDtypeStruct((nout, rows, width), f32),
                   jax.ShapeDtypeStruct((nout, rows, width), bf16)],
        compiler_params=_params(40, ("parallel",)),
        name="memkv",
    )(mem, w_cat)


def _offset_row_copies(x_hbm, buf, sem, step, slot, to_hbm=False):
    ns = buf.shape[1]
    out = []
    for s in range(ns):
        hbm = x_hbm.at[:, step * ns + s, :]
        vmem = buf.at[slot, s]
        out.append(pltpu.make_async_copy(vmem, hbm, sem.at[slot]) if to_hbm
                   else pltpu.make_async_copy(hbm, vmem, sem.at[slot]))
    return out


def _fetch_offset_rows(x_hbm, buf, sem):
    i = pl.program_id(0)

    @pl.when(i == 0)
    def _():
        for c in _offset_row_copies(x_hbm, buf, sem, 0, 0):
            c.start()

    @pl.when(i + 1 < pl.num_programs(0))
    def _():
        for c in _offset_row_copies(x_hbm, buf, sem, i + 1, (i + 1) % 2):
            c.start()

    slot = i % 2
    for c in _offset_row_copies(x_hbm, buf, sem, i, slot):
        c.wait()
    return slot


def _inproj_a_kernel(x_hbm, g_ref, wut_ref, wr_ref, ut_ref, p_ref, hn_ref, xbuf, xsem):
    _, ns, bc, _ = xbuf.shape
    slot = _fetch_offset_rows(x_hbm, xbuf, xsem)
    for s in range(ns):
        hn = (_rms_scale(xbuf[slot, s]) * g_ref[...]).astype(bf16)
        hn_ref[s * bc:(s + 1) * bc, :] = hn
        ut = lax.dot_general(wut_ref[...], hn, _NT, preferred_element_type=f32)
        ut_ref[s] = ut.reshape(ut_ref.shape[1:]).astype(bf16)
    step = 512
    for c in range(0, wr_ref.shape[1], step):
        r = jnp.dot(hn_ref[...], wr_ref[:, c:c + step], preferred_element_type=f32)
        p_ref[:, :, c:c + step] = r.astype(bf16).reshape(ns, bc, step)


def _inproj_a(x3, g, w_ut, w_rest, ns):
    bc, t, d = x3.shape
    mix = w_ut.shape[0]
    rest = w_rest.shape[1]
    groups = mix // SSM_GROUP
    return pl.pallas_call(
        _inproj_a_kernel,
        grid=(t // ns,),
        in_specs=[pl.BlockSpec(memory_space=pl.ANY),
                  _resident(g.shape), _resident(w_ut.shape), _resident(w_rest.shape)],
        out_specs=[pl.BlockSpec((ns, groups, SSM_GROUP, bc), lambda i: (i, 0, 0, 0)),
                   pl.BlockSpec((ns, bc, rest), lambda i: (i, 0, 0))],
        out_shape=[jax.ShapeDtypeStruct((t, groups, SSM_GROUP, bc), bf16),
                   jax.ShapeDtypeStruct((t, bc, rest), bf16)],
        scratch_shapes=[pltpu.VMEM((ns * bc, d), bf16), pltpu.VMEM((2, ns, bc, d), f32),
                        pltpu.SemaphoreType.DMA((2,))],
        compiler_params=_params(52, ("arbitrary",)),
        name="inproj_a",
    )(x3, g, w_ut, w_rest)


def _inproj_plain_kernel(x_ref, g_ref, wut_ref, wr_ref, u_ref, p_ref):
    hn = (_rms_scale(x_ref[...]) * g_ref[...]).astype(bf16)
    u_ref[...] = lax.dot_general(hn, wut_ref[...], _NT, preferred_element_type=f32).astype(bf16)
    p_ref[...] = jnp.dot(hn, wr_ref[...], preferred_element_type=f32).astype(bf16)


def _inproj_plain(x, g, w_ut, w_rest):
    rows, d = x.shape
    mix = w_ut.shape[0]
    rest = w_rest.shape[1]
    return pl.pallas_call(
        _inproj_plain_kernel,
        grid=(1,),
        in_specs=[pl.BlockSpec((rows, d), lambda i: (0, 0)), _resident(g.shape), _resident(w_ut.shape),
                  _resident(w_rest.shape)],
        out_specs=[pl.BlockSpec((rows, mix), lambda i: (0, 0)), pl.BlockSpec((rows, rest), lambda i: (0, 0))],
        out_shape=[jax.ShapeDtypeStruct((rows, mix), bf16), jax.ShapeDtypeStruct((rows, rest), bf16)],
        compiler_params=_params(40, ("arbitrary",)),
        name="inproj_plain",
    )(x, g, w_ut, w_rest)


def _s5_tables_kernel(lre_ref, lim_ref, caa_ref, cab_ref, ba_ref, bb_ref, dd_ref,
                      tt_ref, ca_ref, wb_ref, pw_ref, cas_ref, wbs_ref):
    t_len = TABLE_T
    rows = t_len * SSM_GROUP
    lre = lre_ref[...]
    lim = lim_ref[...]
    kk = lax.broadcasted_iota(jnp.int32, (t_len, LANES), 0).astype(f32)

    def powers(k):
        mag = jnp.exp(lre * k)
        th = lim * k
        return mag * jnp.cos(th), mag * jnp.sin(th)

    pr1, pi1 = powers(kk + 1.0)
    pr0, pi0 = powers((t_len - 1.0) - kk)
    pw_ref[...] = jnp.concatenate([pr1, pi1, pr0, pi0], axis=1)

    def tile_rows(v, n):
        return jnp.broadcast_to(v[None], (n,) + v.shape).reshape(n * v.shape[0], v.shape[1])

    caa = caa_ref[...]
    cab = cab_ref[...]
    ba = ba_ref[...]
    bb = bb_ref[...]
    for t in range(t_len):
        pw = jnp.broadcast_to(pw_ref[t:t + 1, :], (SSM_GROUP, 4 * LANES))
        r0 = t * SSM_GROUP
        cas_ref[r0:r0 + SSM_GROUP, :] = pw[:, 0:128] * caa + pw[:, 128:256] * cab
        wbs_ref[r0:r0 + SSM_GROUP, :] = pw[:, 256:384] * ba + pw[:, 384:512] * bb
    ca = cas_ref[...]
    wbt = wbs_ref[...]
    ca_ref[...] = ca.astype(bf16)
    wb_ref[...] = wbt.T.astype(bf16)

    blk = SUB * SSM_GROUP
    nblk = rows // blk
    rt = wbt[rows - blk:, :]
    m = [None] * nblk
    for d in range(1, nblk):
        m[d] = lax.dot_general(ca[(d - 1) * blk:d * blk, :], rt, _NT, preferred_element_type=f32,
                               precision=lax.Precision.HIGHEST)
    ca0 = jnp.concatenate([caa, ca[:blk - SSM_GROUP, :]], axis=0)
    kj = lax.dot_general(ca0, tile_rows(ba, SUB), _NT, preferred_element_type=f32,
                         precision=lax.Precision.HIGHEST)
    lane = lax.broadcasted_iota(jnp.int32, (SSM_GROUP, LANES), 1)
    hrow = lax.broadcasted_iota(jnp.int32, (SSM_GROUP, LANES), 0)
    s0_lane = lane // SSM_GROUP
    skip = jnp.where(lane % SSM_GROUP == hrow, dd_ref[...], 0.0)
    kjs = [kj[j * SSM_GROUP:(j + 1) * SSM_GROUP, :] for j in range(SUB)]
    kjs[0] = kjs[0] + skip
    drows = []
    for t0 in range(SUB):
        acc = jnp.zeros((SSM_GROUP, LANES), f32)
        for j in range(t0 + 1):
            acc = acc + jnp.where(s0_lane == t0 - j, kjs[j], 0.0)
        drows.append(acc)
    m[0] = jnp.concatenate(drows, axis=0)
    zero = jnp.zeros((blk, blk), f32)
    r1 = jnp.concatenate([m[d] for d in range(nblk - 1, -1, -1)], axis=1)
    r0 = jnp.concatenate([m[d] for d in range(nblk - 2, -1, -1)] + [zero], axis=1)
    tt_ref[...] = jnp.concatenate([r0, r1], axis=0).astype(bf16)


def _s5_tables(lre2, lim2, caa, cab, ba, bb, dd):
    groups = lre2.shape[0]
    rows = TABLE_T * SSM_GROUP

    def gspec(shape):
        return pl.BlockSpec((None,) + shape, lambda g: (g,) + (0,) * len(shape))

    return pl.pallas_call(
        _s5_tables_kernel,
        grid=(groups,),
        in_specs=[gspec((1, LANES)), gspec((1, LANES)), gspec((SSM_GROUP, LANES)), gspec((SSM_GROUP, LANES)),
                  gspec((SSM_GROUP, LANES)), gspec((SSM_GROUP, LANES)), gspec((SSM_GROUP, 1))],
        out_specs=[gspec((2 * SUB * SSM_GROUP, rows)), gspec((rows, LANES)), gspec((LANES, rows))],
        out_shape=[jax.ShapeDtypeStruct((groups, 2 * SUB * SSM_GROUP, rows), bf16),
                   jax.ShapeDtypeStruct((groups, rows, LANES), bf16),
                   jax.ShapeDtypeStruct((groups, LANES, rows), bf16)],
        scratch_shapes=[pltpu.VMEM((TABLE_T, 4 * LANES), f32), pltpu.VMEM((rows, LANES), f32),
                        pltpu.VMEM((rows, LANES), f32)],
        compiler_params=_params(40, ("parallel",)),
        name="s5_tables",
    )(lre2, lim2, caa, cab, ba, bb, dd)


def _cmul(ar, ai, x):
    half = x.shape[0] // 2
    xr = x[:half]
    xi = x[half:]
    return jnp.concatenate([ar * xr - ai * xi, ar * xi + ai * xr], axis=0)


def _s5_group(z, tt, ca, wb, ap, h0, nc, rows):
    pair = 2 * SUB * SSM_GROUP
    table_rows = tt.shape[1]
    ys = []
    for t2 in range(rows // pair):
        kk = pair * (t2 + 1)
        ys.append(jnp.dot(tt[:, table_rows - kk:], z[:kk], preferred_element_type=f32))
    y = jnp.concatenate(ys, axis=0) if len(ys) > 1 else ys[0]
    state = jnp.dot(wb[:, table_rows - rows:], z, preferred_element_type=f32)
    state = state + _cmul(ap[:, 0:1], ap[:, 1:2], h0)
    lane = lax.broadcasted_iota(jnp.int32, state.shape, 1) % nc
    step = 0
    while (1 << step) < nc:
        sh = 1 << step
        shifted = jnp.where(lane >= sh, pltpu.roll(state, sh, axis=1), 0.0)
        state = state + _cmul(ap[:, 2 + 2 * step:3 + 2 * step], ap[:, 3 + 2 * step:4 + 2 * step], shifted)
        step += 1
    if nc > 1:
        h_in = jnp.where(lane >= 1, pltpu.roll(state, 1, axis=1), 0.0) + h0
    else:
        h_in = h0
    y = y + jnp.dot(ca[:rows, :], h_in.astype(bf16), preferred_element_type=f32)
    return y, state


def _s5_kernel(ut_ref, tt_ref, ca_ref, wb_ref, ap_ref, h0_ref, y_ref, hfin_ref, yt_ref, *, nc):
    t_len, per, _, bc = ut_ref.shape
    rows = t_len * SSM_GROUP
    for g in range(per):
        z = ut_ref[:, g].reshape(rows, bc)
        y, state = _s5_group(z, tt_ref.at[g], ca_ref.at[g], wb_ref.at[g], ap_ref[g], h0_ref[g], nc, rows)
        hfin_ref[g] = state
        yt_ref[:, g * SSM_GROUP:(g + 1) * SSM_GROUP, :] = y.reshape(t_len, SSM_GROUP, bc)
    for t in range(t_len):
        y_ref[t] = yt_ref[t].T.astype(bf16)


def _s5(ut4, tt, ca, wb, apow, h0, nc):
    t_len, groups, _, bc = ut4.shape
    per = LANES // SSM_GROUP
    rows = t_len * SSM_GROUP
    last = TABLE_T * SSM_GROUP // rows - 1
    assert (last + 1) * rows == TABLE_T * SSM_GROUP
    mix = groups * SSM_GROUP
    ncol = apow.shape[-1]
    return pl.pallas_call(
        functools.partial(_s5_kernel, nc=nc),
        grid=(groups // per,),
        in_specs=[pl.BlockSpec((t_len, per, SSM_GROUP, bc), lambda G: (0, G, 0, 0)),
                  pl.BlockSpec((per, 2 * SUB * SSM_GROUP, rows), lambda G: (G, 0, last)),
                  pl.BlockSpec((per, rows, LANES), lambda G: (G, 0, 0)),
                  pl.BlockSpec((per, LANES, rows), lambda G: (G, 0, last)),
                  pl.BlockSpec((per, SSM_STATE, ncol), lambda G: (G, 0, 0)),
                  pl.BlockSpec((per, 2 * SSM_STATE, bc), lambda G: (G, 0, 0))],
        out_specs=[pl.BlockSpec((t_len, bc, LANES), lambda G: (0, 0, G)),
                   pl.BlockSpec((per, 2 * SSM_STATE, bc), lambda G: (G, 0, 0))],
        out_shape=[jax.ShapeDtypeStruct((t_len, bc, mix), bf16),
                   jax.ShapeDtypeStruct((groups, 2 * SSM_STATE, bc), f32)],
        scratch_shapes=[pltpu.VMEM((t_len, LANES, bc), f32)],
        compiler_params=_params(52, ("parallel",)),
        name="s5",
    )(ut4, tt, ca, wb, apow, h0)


def _ld(ref, start, n, c0, c1):
    if len(ref.shape) == 3:
        s, r = divmod(start, ref.shape[1])
        return ref[s, r:r + n, c0:c1]
    return ref[start:start + n, c0:c1]


def _post_kernel(y_ref, p_ref, x_ref, mk_ref, mv_ref, wglu_ref, bglu_ref, wout_ref, gpost_ref,
                 o_ref, cat_ref, *dma, glu, segs, mem_scale):
    mix = y_ref.shape[-1]
    memw = mk_ref.shape[-1]
    hd = memw // MEM_HEADS
    rows = cat_ref.shape[0]
    if dma:
        xbuf, xsem, obuf, osem = dma
        slot = _fetch_offset_rows(x_ref, xbuf, xsem)
    y = y_ref[...].reshape(rows, mix).astype(f32)
    gate = p_ref[:, :mix] if len(p_ref.shape) == 2 else p_ref[:, :, :mix].reshape(rows, mix)
    gate = gate.astype(f32)
    heads = [(b, h, pieces) for b, pieces in enumerate(segs) for h in range(MEM_HEADS)]
    scores = []
    for b, h, pieces in heads:
        cq = mix + h * hd
        q = jnp.concatenate([_ld(p_ref, st, n, cq, cq + hd) for st, n in pieces], axis=0)
        scores.append(lax.dot_general(q, mk_ref[b, :, h * hd:(h + 1) * hd], _NT, preferred_element_type=f32))
    if glu:
        y = _gelu_tanh(y)
        zz = jnp.dot(y.astype(bf16), wglu_ref[...], preferred_element_type=f32) + bglu_ref[...]
        y = y * _sigmoid(zz)
    main = (y * (gate * _sigmoid(gate))).astype(bf16)
    out = jnp.dot(main, wout_ref[:mix, :], preferred_element_type=f32)
    probs = []
    for s in scores:
        s = s * mem_scale
        e = jnp.exp(s - jnp.max(s, axis=-1, keepdims=True))
        probs.append((e / jnp.sum(e, axis=-1, keepdims=True)).astype(bf16))
    for (b, h, pieces), p in zip(heads, probs):
        cg = mix + memw + h * hd
        mg = jnp.concatenate([_ld(p_ref, st, n, cg, cg + hd) for st, n in pieces], axis=0).astype(f32)
        o = jnp.dot(p, mv_ref[b, :, h * hd:(h + 1) * hd], preferred_element_type=f32)
        om = (o * (mg * _sigmoid(mg))).astype(bf16)
        off = 0
        for st, n in pieces:
            cat_ref[st:st + n, h * hd:(h + 1) * hd] = om[off:off + n]
            off += n
    out = out + jnp.dot(cat_ref[...], wout_ref[mix:, :], preferred_element_type=f32)
    d = out.shape[-1]
    branch = _rms_scale(out) * gpost_ref[...]
    if not dma:
        o_ref[...] = (x_ref[...].reshape(rows, d) + branch).reshape(o_ref.shape)
        return
    i = pl.program_id(0)
    last = pl.num_programs(0) - 1

    @pl.when(i >= 2)
    def _():
        for c in _offset_row_copies(o_ref, obuf, osem, i - 2, slot, to_hbm=True):
            c.wait()

    obuf[slot] = (xbuf[slot].reshape(rows, d) + branch).reshape(obuf.shape[1:])
    for c in _offset_row_copies(o_ref, obuf, osem, i, slot, to_hbm=True):
        c.start()

    @pl.when(i == last)
    def _():
        for c in _offset_row_copies(o_ref, obuf, osem, i, slot, to_hbm=True):
            c.wait()

    @pl.when((i == last) & (i >= 1))
    def _():
        for c in _offset_row_copies(o_ref, obuf, osem, i - 1, 1 - slot, to_hbm=True):
            c.wait()


def _post(y, p, x, mk, mv, wglu, bglu, wout, gpost, *, glu, segs, grid, y_spec, p_spec, x_spec, mem_spec, rows,
          offsets_per_step=None):
    mem_scale = 1.0 / math.sqrt(mk.shape[-1] // MEM_HEADS)
    scratch = [pltpu.VMEM((rows, mk.shape[-1]), bf16)]
    semantics = ("parallel",)
    if x_spec is None:
        x_spec = pl.BlockSpec(memory_space=pl.ANY)
        buf = pltpu.VMEM((2, offsets_per_step, x.shape[0], x.shape[2]), f32)
        scratch += [buf, pltpu.SemaphoreType.DMA((2,)), buf, pltpu.SemaphoreType.DMA((2,))]
        semantics = ("arbitrary",)
    return pl.pallas_call(
        functools.partial(_post_kernel, glu=glu, segs=segs, mem_scale=mem_scale),
        grid=grid,
        in_specs=[y_spec, p_spec, x_spec, mem_spec, mem_spec,
                  _resident(wglu.shape), _resident(bglu.shape), _resident(wout.shape), _resident(gpost.shape)],
        out_specs=x_spec,
        out_shape=jax.ShapeDtypeStruct(x.shape, f32),
        scratch_shapes=scratch,
        compiler_params=_params(52, semantics),
        name="post_glu" if glu else "post",
    )(y, p, x, mk, mv, wglu, bglu, wout, gpost)


def _kvb_kernel(x_ref, gkv_ref, gb_ref, wkv_ref, wb_ref, k_ref, v_ref, kvb_ref, q_ref, pr_ref):
    xs = _rms_scale(x_ref[...])
    hkv = (xs * gkv_ref[...]).astype(bf16)
    hb = (xs * gb_ref[...]).astype(bf16)
    nst, heads, rps, hd = k_ref.shape
    mix = heads * hd
    step = 512
    for c in range(0, wkv_ref.shape[1], step):
        r = jnp.dot(hkv, wkv_ref[:, c:c + step], preferred_element_type=f32)
        kvb_ref[:, c:c + step] = r.astype(bf16)
        for j in range(step // hd):
            h = (c % mix) // hd + j
            (k_ref if c < mix else v_ref)[:, h] = r[:, j * hd:(j + 1) * hd].reshape(nst, rps, hd)
    for c in range(0, wb_ref.shape[1], step):
        r = jnp.dot(hb, wb_ref[:, c:c + step], preferred_element_type=f32).astype(bf16)
        if c < mix:
            q_ref[:, c:c + step] = r
        else:
            pr_ref[:, c - mix:c - mix + step] = r


def _kvb(x, gkv, gb, wkv, wb, mix, streams):
    n, d = x.shape
    seq = n // streams
    tm = min(256, n)
    rps = min(tm, seq)
    assert tm % rps == 0 and seq % rps == 0 and rps % 8 == 0
    per_stream = seq // rps
    rest = wb.shape[1] - mix
    heads = mix // SB_HEAD_DIM
    row = lambda w: pl.BlockSpec((tm, w), lambda i: (i, 0))
    head_major = pl.BlockSpec((tm // rps, heads, rps, SB_HEAD_DIM), lambda i: (i // per_stream, 0, i % per_stream, 0))
    kv_shape = jax.ShapeDtypeStruct((streams, heads, seq, SB_HEAD_DIM), f32)
    return pl.pallas_call(
        _kvb_kernel,
        grid=(n // tm,),
        in_specs=[row(d), _resident(gkv.shape), _resident(gb.shape), _resident(wkv.shape), _resident(wb.shape)],
        out_specs=[head_major, head_major, row(2 * mix), row(mix), row(rest)],
        out_shape=[kv_shape, kv_shape,
                   jax.ShapeDtypeStruct((n, 2 * mix), bf16), jax.ShapeDtypeStruct((n, mix), bf16),
                   jax.ShapeDtypeStruct((n, rest), bf16)],
        compiler_params=_params(56, ("parallel",)),
        name="kvb",
    )(x, gkv, gb, wkv, wb)


_MASKED = -1e30


def _sb_softplus_tri(z, ntri, mask):
    sp = jnp.maximum(z, 0.0) + jnp.log2(1.0 + jnp.exp2(-jnp.abs(z)))
    if mask is not None:
        sp = jnp.where(mask, sp, 0.0)
    inner = jnp.dot(sp.astype(bf16), ntri, preferred_element_type=f32)
    t = (z - sp) + inner
    if mask is not None:
        t = jnp.where(mask, t, _MASKED)
    return t, inner[:, 0:1] - sp[:, 0:1]


def _sb_apply(t, d, v, carry, acc):
    w = jnp.exp2(t + carry)
    return acc + jnp.dot(w.astype(bf16), v, preferred_element_type=f32), carry + d


def _tri_and_mask(n):
    r = lax.broadcasted_iota(jnp.int32, (n, n), 0)
    c = lax.broadcasted_iota(jnp.int32, (n, n), 1)
    return jnp.where(r > c, -1.0, 0.0).astype(bf16), c < r


def _sb_attn_kernel(q_ref, k_ref, v_ref, o_ref, acc_ref, car_ref, t_ref, d_ref, *, nq):
    tq = ATTN_BLOCK
    i0 = pl.program_id(2) * nq
    ntri, causal = _tri_and_mask(tq)

    def rows(ref, kb):
        off = kb * tq if isinstance(kb, int) else pl.multiple_of(kb * tq, tq)
        return ref[pl.ds(off, tq), :]

    def pieces(r_lo, diag):
        out = []
        r = r_lo
        if diag:
            out.append((r * tq, (r + 1) * tq, causal))
            r += 1
        while r < nq:
            n = min(2, nq - r)
            out.append((r * tq, (r + n) * tq, None))
            r += n
        return out

    def step(prev, cur):
        zs = []
        if cur is not None:
            k = rows(k_ref, cur[0])
            for a, b, mask in pieces(cur[1], cur[2]):
                zs.append((a, b, mask, lax.dot_general(q_ref[a:b, :], k, _NT, preferred_element_type=f32)))
        if prev is not None:
            v = rows(v_ref, prev[0])
            for a, b, _ in pieces(prev[1], False):
                acc, carry = _sb_apply(t_ref[a:b, :], d_ref[a:b, :], v, car_ref[a:b, :], acc_ref[a:b, :])
                acc_ref[a:b, :] = acc
                car_ref[a:b, :] = carry
        for a, b, mask, z in zs:
            t, d = _sb_softplus_tri(z, ntri, mask)
            t_ref[a:b, :] = t
            d_ref[a:b, :] = d

    acc_ref[...] = jnp.zeros(acc_ref.shape, f32)
    car_ref[...] = jnp.zeros(car_ref.shape, f32)
    prev = None
    for p in range(nq):
        cur = (i0 + nq - 1 - p, nq - 1 - p, True)
        step(prev, cur)
        prev = cur

    def body(j, c):
        kb = i0 - 1 - 2 * j
        step((kb + 1, 0, False), (kb, 0, False))
        step((kb, 0, False), (kb - 1, 0, False))
        return c

    lax.fori_loop(0, i0 // 2, body, 0)
    step((0, 0, False), None)
    o_ref[...] = acc_ref[...].astype(bf16)


def _sb_attn(q, kvb, bsz, seq, heads):
    tq = ATTN_BLOCK
    hd = SB_HEAD_DIM
    nq = 4
    assert seq % (tq * nq) == 0
    steps = seq // (tq * nq)
    return pl.pallas_call(
        functools.partial(_sb_attn_kernel, nq=nq),
        grid=(bsz, heads, steps),
        in_specs=[pl.BlockSpec((nq * tq, hd), lambda b, h, i: (b * steps + i, h)),
                  pl.BlockSpec((seq, hd), lambda b, h, i: (b, h)),
                  pl.BlockSpec((seq, hd), lambda b, h, i: (b, heads + h))],
        out_specs=pl.BlockSpec((nq * tq, hd), lambda b, h, i: (b * steps + i, h)),
        out_shape=jax.ShapeDtypeStruct(q.shape, bf16),
        scratch_shapes=[pltpu.VMEM((nq * tq, hd), f32), pltpu.VMEM((nq * tq, 1), f32),
                        pltpu.VMEM((nq * tq, tq), f32), pltpu.VMEM((nq * tq, 1), f32)],
        compiler_params=_params(40, ("parallel", "parallel", "arbitrary")),
        name="sb_attn",
    )(q, kvb, kvb)


def _sb_attn_sample_kernel(q_ref, kn_ref, vn_ref, kc_ref, vc_ref, o_ref):
    tq = q_ref.shape[0]
    past = kc_ref.shape[0]
    blk = min(ATTN_BLOCK, past)
    q = q_ref[...]
    tri_n, causal = _tri_and_mask(tq)
    tri_p, _ = _tri_and_mask(blk)
    blocks = [(kn_ref[...], vn_ref[...], tri_n, causal)]
    for j in range(past // blk - 1, -1, -1):
        blocks.append((kc_ref[j * blk:(j + 1) * blk, :].astype(bf16), vc_ref[j * blk:(j + 1) * blk, :].astype(bf16),
                       tri_p, None))
    zs = [lax.dot_general(q, k, _NT, preferred_element_type=f32) for k, _, _, _ in blocks]
    tds = [_sb_softplus_tri(z, tri, mask) for z, (_, _, tri, mask) in zip(zs, blocks)]
    acc = jnp.zeros((tq, q_ref.shape[1]), f32)
    carry = jnp.zeros((tq, 1), f32)
    for (t, d), (_, v, _, _) in zip(tds, blocks):
        acc, carry = _sb_apply(t, d, v, carry, acc)
    o_ref[...] = acc.astype(bf16)


def _sb_attn_sample(q, kvb, cache_k, cache_v, tq):
    bsz, heads, past, hd = cache_k.shape
    cache = pl.BlockSpec((None, None, past, hd), lambda b, h: (b, h, 0, 0))
    return pl.pallas_call(
        _sb_attn_sample_kernel,
        grid=(bsz, heads),
        in_specs=[pl.BlockSpec((tq, hd), lambda b, h: (b, h)),
                  pl.BlockSpec((tq, hd), lambda b, h: (b, h)),
                  pl.BlockSpec((tq, hd), lambda b, h: (b, heads + h)),
                  cache, cache],
        out_specs=pl.BlockSpec((tq, hd), lambda b, h: (b, h)),
        out_shape=jax.ShapeDtypeStruct(q.shape, bf16),
        compiler_params=_params(40, ("parallel", "parallel")),
        name="sb_attn_sample",
    )(q, kvb, kvb, cache_k, cache_v)


def _ssm_param_tables(lam_re, lam_im, log_dt, b_re, b_im, c_re, c_im, dvec):
    dt = jnp.exp(log_dt.astype(f32))[:, None]
    lr = lam_re.astype(f32)
    li = lam_im.astype(f32)
    lre = lr * dt
    lim = li * dt
    mag = jnp.exp(lre)
    nr = mag * jnp.cos(lim) - 1.0
    ni = mag * jnp.sin(lim)
    den = lr * lr + li * li
    fr = ((nr * lr + ni * li) / den)[..., None]
    fi = ((ni * lr - nr * li) / den)[..., None]
    b_r = b_re.astype(f32)
    b_i = b_im.astype(f32)
    br = jnp.swapaxes(fr * b_r - fi * b_i, 1, 2)
    bi = jnp.swapaxes(fr * b_i + fi * b_r, 1, 2)
    cr = c_re.astype(f32)
    ci = c_im.astype(f32)
    dup = lambda v: jnp.concatenate([v, v], axis=-1)
    lre2 = dup(lre)[:, None, :]
    lim2 = dup(lim)[:, None, :]
    caa = jnp.concatenate([cr, -ci], axis=-1)
    cab = jnp.concatenate([-ci, -cr], axis=-1)
    ba = jnp.concatenate([br, bi], axis=-1)
    bb = jnp.concatenate([-bi, br], axis=-1)
    dd = dvec.astype(f32)[:, :, None]
    return lre, lim, (lre2, lim2, caa, cab, ba, bb, dd)


def _chunk_powers(lre, lim, t_len, nc):
    cols = []
    exps = [t_len]
    j = 0
    while (1 << j) < nc:
        exps.append(t_len * (1 << j))
        j += 1
    for e in exps:
        mag = jnp.exp(lre * e)
        cols += [mag * jnp.cos(lim * e), mag * jnp.sin(lim * e)]
    return jnp.stack(cols, axis=-1)


def _layer_a(x, t_len, nc, h0_lanes, mk, mv, wa, tables, lre, lim, prompt):
    n, d = x.shape
    mix = wa["w_glu"].shape[0]
    groups = mix // SSM_GROUP
    tt, ca, wb = tables
    apow = _chunk_powers(lre, lim, t_len, nc)
    if prompt:
        bc = n // t_len
        ns = 2
        x3 = x.reshape(bc, t_len, d)
        ut4, p = _inproj_a(x3, wa["g_pre"], wa["w_ut"], wa["w_rest"], ns)
        y, hfin = _s5(ut4, tt, ca, wb, apow, h0_lanes, nc)
        streams = bc // nc
        rest = p.shape[-1]
        segs = tuple(tuple((s * bc + b * nc, nc) for s in range(ns)) for b in range(streams))
        blk = lambda w: pl.BlockSpec((ns, bc, w), lambda i: (i, 0, 0))
        x1 = _post(y, p, x3, mk, mv, wa["w_glu"], wa["b_glu"], wa["w_out"], wa["g_post"],
                   glu=True, segs=segs, grid=(t_len // ns,), rows=ns * bc, offsets_per_step=ns,
                   y_spec=blk(mix), p_spec=blk(rest), x_spec=None, mem_spec=_resident(mk.shape))
        return x1.reshape(n, d), hfin
    streams = n // t_len
    lanes = h0_lanes.shape[-1]
    u, p = _inproj_plain(x, wa["g_pre"], wa["w_ut"], wa["w_rest"])
    ut4 = jnp.transpose(u.reshape(streams, t_len, groups, SSM_GROUP), (1, 2, 3, 0))
    ut4 = jnp.pad(ut4, ((0, 0), (0, 0), (0, 0), (0, lanes - streams)))
    y3, hfin = _s5(ut4, tt, ca, wb, apow, h0_lanes, nc)
    y = jnp.transpose(y3[:, :streams, :], (1, 0, 2)).reshape(n, mix)
    rest = p.shape[-1]
    whole = lambda w: pl.BlockSpec((n, w), lambda i: (0, 0))
    x1 = _post(y, p, x, mk, mv, wa["w_glu"], wa["b_glu"], wa["w_out"], wa["g_post"],
               glu=True, segs=tuple(((b * t_len, t_len),) for b in range(streams)), grid=(1,), rows=n,
               y_spec=whole(mix), p_spec=whole(rest), x_spec=whole(d), mem_spec=_resident(mk.shape))
    return x1, hfin


def kernel(x_prompt, x_sample, cache_k, cache_v, cache_mem_k, cache_mem_v, state_ssm, mem_prompt, w_in_a, w_out_a, g_pre_a, g_post_a, ssm_lam_re, ssm_lam_im, ssm_log_dt, ssm_b_re, ssm_b_im, ssm_c_re, ssm_c_im, ssm_d, w_glu, b_glu, g_kv, w_kv, w_in_b, w_out_b, g_pre_b, g_post_b, w_mem_k, w_mem_v):
    bsz, seq, d = x_prompt.shape
    dbsz, dseq, _ = x_sample.shape
    mix = w_glu.shape[-1]
    memw = w_mem_k.shape[-1]
    heads = mix // SB_HEAD_DIM
    groups = mix // SSM_GROUP
    n_mem = mem_prompt.shape[1]
    depth = w_mem_k.shape[0]
    assert depth == 2 and w_in_a.shape[0] == 1 and w_in_b.shape[0] == 1
    assert seq % CHUNK == 0 and (bsz * seq // CHUNK) % LANES == 0 and seq % ATTN_BLOCK == 0
    assert dseq % (2 * SUB) == 0 and dseq <= TABLE_T and dbsz <= LANES
    nc = seq // CHUNK
    assert nc & (nc - 1) == 0

    row = lambda v: v.astype(f32).reshape(1, -1)
    wa = dict(
        w_ut=w_in_a[0][:, :mix].T.astype(bf16),
        w_rest=w_in_a[0][:, mix:].astype(bf16),
        w_glu=w_glu[0].astype(bf16), b_glu=row(b_glu[0]),
        w_out=w_out_a[0].astype(bf16), g_pre=row(g_pre_a[0]), g_post=row(g_post_a[0]))
    qscale = math.log2(math.e) / math.sqrt(SB_HEAD_DIM)
    w_b = jnp.concatenate([w_in_b[0][:, :mix] * qscale, w_in_b[0][:, mix:]], axis=1).astype(bf16)
    w_kv_b = w_kv.astype(bf16)
    w_out_bb = w_out_b[0].astype(bf16)
    w_mem = jnp.concatenate([w_mem_k[0], w_mem_k[1], w_mem_v[0], w_mem_v[1]], axis=1).astype(bf16)

    memf, memb = _memkv(mem_prompt.reshape(bsz * n_mem, d), w_mem, memw)
    mem_k_prompt = memf[:depth].reshape(depth, bsz, n_mem, MEM_HEADS, memw // MEM_HEADS)
    mem_v_prompt = memf[depth:].reshape(depth, bsz, n_mem, MEM_HEADS, memw // MEM_HEADS)
    mkp = memb[:depth].reshape(depth, bsz, n_mem, memw)
    mvp = memb[depth:].reshape(depth, bsz, n_mem, memw)
    mks = cache_mem_k.reshape(depth, dbsz, n_mem, memw).astype(bf16)
    mvs = cache_mem_v.reshape(depth, dbsz, n_mem, memw).astype(bf16)

    lre, lim, tab_in = _ssm_param_tables(ssm_lam_re[0], ssm_lam_im[0], ssm_log_dt[0], ssm_b_re[0], ssm_b_im[0],
                                         ssm_c_re[0], ssm_c_im[0], ssm_d[0])
    tables = _s5_tables(*tab_in)

    n_p = bsz * seq
    bc = n_p // CHUNK
    h0_p = jnp.zeros((groups, 2 * SSM_STATE, bc), f32)
    x1_p, hfin_p = _layer_a(x_prompt.reshape(n_p, d), CHUNK, nc, h0_p, mkp[0], mvp[0], wa, tables, lre, lim, True)
    k_p, v_p, kvb_p, q_p, pr_p = _kvb(x1_p, row(g_kv), row(g_pre_b[0]), w_kv_b, w_b, mix, bsz)
    o_p = _sb_attn(q_p, kvb_p, bsz, seq, heads)
    rows_b = 2 * ATTN_BLOCK if seq % (2 * ATTN_BLOCK) == 0 else ATTN_BLOCK
    per_b = seq // rows_b
    tile = lambda w: pl.BlockSpec((rows_b, w), lambda i: (i, 0))
    y_p = _post(o_p, pr_p, x1_p, mkp[1], mvp[1], wa["w_glu"], wa["b_glu"], w_out_bb, row(g_post_b[0]),
                glu=False, segs=(((0, rows_b),),), grid=(n_p // rows_b,), rows=rows_b,
                y_spec=tile(mix), p_spec=tile(pr_p.shape[-1]), x_spec=tile(d),
                mem_spec=pl.BlockSpec((1, n_mem, memw), lambda i: (i // per_b, 0, 0)))

    n_s = dbsz * dseq
    st = state_ssm[0].astype(f32)
    h0_s = jnp.transpose(jnp.concatenate([st[..., 0], st[..., 1]], axis=-1), (1, 2, 0))
    h0_s = jnp.pad(h0_s, ((0, 0), (0, 0), (0, LANES - dbsz)))
    x1_s, hfin_s = _layer_a(x_sample.reshape(n_s, d), dseq, 1, h0_s, mks[0], mvs[0], wa, tables, lre, lim, False)
    k_s, v_s, kvb_s, q_s, pr_s = _kvb(x1_s, row(g_kv), row(g_pre_b[0]), w_kv_b, w_b, mix, dbsz)
    head_major = lambda a: jnp.transpose(a, (0, 2, 1, 3))
    o_s = _sb_attn_sample(q_s, kvb_s, head_major(cache_k), head_major(cache_v), dseq)
    whole = lambda w: pl.BlockSpec((n_s, w), lambda i: (0, 0))
    y_s = _post(o_s, pr_s, x1_s, mks[1], mvs[1], wa["w_glu"], wa["b_glu"], w_out_bb, row(g_post_b[0]),
                glu=False, segs=tuple(((b * dseq, dseq),) for b in range(dbsz)), grid=(1,), rows=n_s,
                y_spec=whole(mix), p_spec=whole(pr_s.shape[-1]), x_spec=whole(d),
                mem_spec=_resident(mks[1].shape))

    def ssm_out(hfin, lanes_idx):
        h = hfin[:, :, lanes_idx]
        h = jnp.transpose(h, (2, 0, 1))
        return jnp.stack([h[..., :SSM_STATE], h[..., SSM_STATE:]], axis=-1)[None]

    ssm_prompt = ssm_out(hfin_p, jnp.arange(bsz) * nc + (nc - 1)).astype(x_prompt.dtype)
    ssm_sample = ssm_out(hfin_s, jnp.arange(dbsz)).astype(state_ssm.dtype)
    return (y_p.reshape(bsz, seq, d), y_s.reshape(dbsz, dseq, d),
            head_major(k_p), head_major(v_p), head_major(k_s), head_major(v_s),
            ssm_prompt, ssm_sample, mem_k_prompt, mem_v_prompt)
```

```python
import functools
import math

import jax
import jax.numpy as jnp
from jax import lax
from jax.experimental import pallas as pl
from jax.experimental.pallas import tpu as pltpu

EPS = 1e-6
CHUNK = 64
SSM_GROUP = 16
SSM_STATE = 64
SB_HEAD_DIM = 128
MEM_HEADS = 4
SUB = 8
TABLE_T = 64
LANES = 128
ATTN_BLOCK = 256
MIB = 1024 * 1024

bf16 = jnp.bfloat16
f32 = jnp.float32

_NT = (((1,), (1,)), ((), ()))


def _params(vmem_mib, semantics):
    return pltpu.CompilerParams(vmem_limit_bytes=vmem_mib * MIB, dimension_semantics=semantics)


def _resident(shape):
    zeros = (0,) * len(shape)
    return pl.BlockSpec(shape, lambda *_: zeros, pipeline_mode=pl.Buffered(1))


def _rms_scale(x):
    return x * lax.rsqrt(jnp.mean(x * x, axis=-1, keepdims=True) + EPS)


def _sigmoid(x):
    return 1.0 / (1.0 + jnp.exp(-x))


def _gelu_tanh(x):
    c = math.sqrt(2.0 / math.pi)
    return 0.5 * x * (1.0 + jnp.tanh(c * (x + 0.044715 * (x * x * x))))


def _memkv_kernel(x_ref, w_ref, of_ref, ob_ref):
    acc = jnp.dot(x_ref[...].astype(bf16), w_ref[...], preferred_element_type=f32)
    width = of_ref.shape[-1]
    for j in range(of_ref.shape[0]):
        blk = acc[:, j * width:(j + 1) * width]
        of_ref[j] = blk
        ob_ref[j] = blk.astype(bf16)


def _memkv(mem, w_cat, width):
    rows, d = mem.shape
    nout = w_cat.shape[1] // width
    tm = 256
    return pl.pallas_call(
        _memkv_kernel,
        grid=(rows // tm,),
        in_specs=[pl.BlockSpec((tm, d), lambda i: (i, 0)), _resident(w_cat.shape)],
        out_specs=[pl.BlockSpec((nout, tm, width), lambda i: (0, i, 0)),
                   pl.BlockSpec((nout, tm, width), lambda i: (0, i, 0))],
        out_shape=[jax.ShapeDtypeStruct((nout, rows, width), f32),
                   jax.ShapeDtypeStruct((nout, rows, width), bf16)],
        compiler_params=_params(40, ("parallel",)),
        name="memkv",
    )(mem, w_cat)


def _offset_row_copies(x_hbm, buf, sem, step, slot, to_hbm=False):
    ns = buf.shape[1]
    out = []
    for s in range(ns):
        hbm = x_hbm.at[:, step * ns + s, :]
        vmem = buf.at[slot, s]
        out.append(pltpu.make_async_copy(vmem, hbm, sem.at[slot]) if to_hbm
                   else pltpu.make_async_copy(hbm, vmem, sem.at[slot]))
    return out


def _fetch_offset_rows(x_hbm, buf, sem):
    i = pl.program_id(0)

    @pl.when(i == 0)
    def _():
        for c in _offset_row_copies(x_hbm, buf, sem, 0, 0):
            c.start()

    @pl.when(i + 1 < pl.num_programs(0))
    def _():
        for c in _offset_row_copies(x_hbm, buf, sem, i + 1, (i + 1) % 2):
            c.start()

    slot = i % 2
    for c in _offset_row_copies(x_hbm, buf, sem, i, slot):
        c.wait()
    return slot


def _inproj_a_kernel(x_hbm, g_ref, wut_ref, wr_ref, ut_ref, p_ref, hn_ref, xbuf, xsem):
    _, ns, bc, _ = xbuf.shape
    slot = _fetch_offset_rows(x_hbm, xbuf, xsem)
    for s in range(ns):
        hn = (_rms_scale(xbuf[slot, s]) * g_ref[...]).astype(bf16)
        hn_ref[s * bc:(s + 1) * bc, :] = hn
        ut = lax.dot_general(wut_ref[...], hn, _NT, preferred_element_type=f32)
        ut_ref[s] = ut.reshape(ut_ref.shape[1:]).astype(bf16)
    step = 512
    for c in range(0, wr_ref.shape[1], step):
        r = jnp.dot(hn_ref[...], wr_ref[:, c:c + step], preferred_element_type=f32)
        p_ref[:, :, c:c + step] = r.astype(bf16).reshape(ns, bc, step)


def _inproj_a(x3, g, w_ut, w_rest, ns):
    bc, t, d = x3.shape
    mix = w_ut.shape[0]
    rest = w_rest.shape[1]
    groups = mix // SSM_GROUP
    return pl.pallas_call(
        _inproj_a_kernel,
        grid=(t // ns,),
        in_specs=[pl.BlockSpec(memory_space=pl.ANY),
                  _resident(g.shape), _resident(w_ut.shape), _resident(w_rest.shape)],
        out_specs=[pl.BlockSpec((ns, groups, SSM_GROUP, bc), lambda i: (i, 0, 0, 0)),
                   pl.BlockSpec((ns, bc, rest), lambda i: (i, 0, 0))],
        out_shape=[jax.ShapeDtypeStruct((t, groups, SSM_GROUP, bc), bf16),
                   jax.ShapeDtypeStruct((t, bc, rest), bf16)],
        scratch_shapes=[pltpu.VMEM((ns * bc, d), bf16), pltpu.VMEM((2, ns, bc, d), f32),
                        pltpu.SemaphoreType.DMA((2,))],
        compiler_params=_params(52, ("arbitrary",)),
        name="inproj_a",
    )(x3, g, w_ut, w_rest)


def _inproj_plain_kernel(x_ref, g_ref, wut_ref, wr_ref, u_ref, p_ref):
    hn = (_rms_scale(x_ref[...]) * g_ref[...]).astype(bf16)
    u_ref[...] = lax.dot_general(hn, wut_ref[...], _NT, preferred_element_type=f32).astype(bf16)
    p_ref[...] = jnp.dot(hn, wr_ref[...], preferred_element_type=f32).astype(bf16)


def _inproj_plain(x, g, w_ut, w_rest):
    rows, d = x.shape
    mix = w_ut.shape[0]
    rest = w_rest.shape[1]
    return pl.pallas_call(
        _inproj_plain_kernel,
        grid=(1,),
        in_specs=[pl.BlockSpec((rows, d), lambda i: (0, 0)), _resident(g.shape), _resident(w_ut.shape),
                  _resident(w_rest.shape)],
        out_specs=[pl.BlockSpec((rows, mix), lambda i: (0, 0)), pl.BlockSpec((rows, rest), lambda i: (0, 0))],
        out_shape=[jax.ShapeDtypeStruct((rows, mix), bf16), jax.ShapeDtypeStruct((rows, rest), bf16)],
        compiler_params=_params(40, ("arbitrary",)),
        name="inproj_plain",
    )(x, g, w_ut, w_rest)


def _s5_tables_kernel(lre_ref, lim_ref, caa_ref, cab_ref, ba_ref, bb_ref, dd_ref,
                      tt_ref, ca_ref, wb_ref, pw_ref, cas_ref, wbs_ref):
    t_len = TABLE_T
    rows = t_len * SSM_GROUP
    lre = lre_ref[...]
    lim = lim_ref[...]
    kk = lax.broadcasted_iota(jnp.int32, (t_len, LANES), 0).astype(f32)

    def powers(k):
        mag = jnp.exp(lre * k)
        th = lim * k
        return mag * jnp.cos(th), mag * jnp.sin(th)

    pr1, pi1 = powers(kk + 1.0)
    pr0, pi0 = powers((t_len - 1.0) - kk)
    pw_ref[...] = jnp.concatenate([pr1, pi1, pr0, pi0], axis=1)

    def tile_rows(v, n):
        return jnp.broadcast_to(v[None], (n,) + v.shape).reshape(n * v.shape[0], v.shape[1])

    caa = caa_ref[...]
    cab = cab_ref[...]
    ba = ba_ref[...]
    bb = bb_ref[...]
    for t in range(t_len):
        pw = jnp.broadcast_to(pw_ref[t:t + 1, :], (SSM_GROUP, 4 * LANES))
        r0 = t * SSM_GROUP
        cas_ref[r0:r0 + SSM_GROUP, :] = pw[:, 0:128] * caa + pw[:, 128:256] * cab
        wbs_ref[r0:r0 + SSM_GROUP, :] = pw[:, 256:384] * ba + pw[:, 384:512] * bb
    ca = cas_ref[...]
    wbt = wbs_ref[...]
    ca_ref[...] = ca.astype(bf16)
    wb_ref[...] = wbt.T.astype(bf16)

    blk = SUB * SSM_GROUP
    nblk = rows // blk
    rt = wbt[rows - blk:, :]
    lagged = lax.dot_general(ca[:(nblk - 1) * blk, :], rt, _NT, preferred_element_type=f32,
                             precision=lax.Precision.HIGHEST)
    m = [None] + [lagged[(d - 1) * blk:d * blk, :] for d in range(1, nblk)]
    ca0 = jnp.concatenate([caa, ca[:blk - SSM_GROUP, :]], axis=0)
    kj = lax.dot_general(ca0, tile_rows(ba, SUB), _NT, preferred_element_type=f32,
                         precision=lax.Precision.HIGHEST)
    lane = lax.broadcasted_iota(jnp.int32, (SSM_GROUP, LANES), 1)
    hrow = lax.broadcasted_iota(jnp.int32, (SSM_GROUP, LANES), 0)
    s0_lane = lane // SSM_GROUP
    skip = jnp.where(lane % SSM_GROUP == hrow, dd_ref[...], 0.0)
    kjs = [kj[j * SSM_GROUP:(j + 1) * SSM_GROUP, :] for j in range(SUB)]
    kjs[0] = kjs[0] + skip
    drows = []
    for t0 in range(SUB):
        acc = jnp.zeros((SSM_GROUP, LANES), f32)
        for j in range(t0 + 1):
            acc = acc + jnp.where(s0_lane == t0 - j, kjs[j], 0.0)
        drows.append(acc)
    m[0] = jnp.concatenate(drows, axis=0)
    zero = jnp.zeros((blk, blk), f32)
    r1 = jnp.concatenate([m[d] for d in range(nblk - 1, -1, -1)], axis=1)
    r0 = jnp.concatenate([m[d] for d in range(nblk - 2, -1, -1)] + [zero], axis=1)
    tt_ref[...] = jnp.concatenate([r0, r1], axis=0).astype(bf16)


def _s5_tables(lre2, lim2, caa, cab, ba, bb, dd):
    groups = lre2.shape[0]
    rows = TABLE_T * SSM_GROUP

    def gspec(shape):
        return pl.BlockSpec((None,) + shape, lambda g: (g,) + (0,) * len(shape))

    return pl.pallas_call(
        _s5_tables_kernel,
        grid=(groups,),
        in_specs=[gspec((1, LANES)), gspec((1, LANES)), gspec((SSM_GROUP, LANES)), gspec((SSM_GROUP, LANES)),
                  gspec((SSM_GROUP, LANES)), gspec((SSM_GROUP, LANES)), gspec((SSM_GROUP, 1))],
        out_specs=[gspec((2 * SUB * SSM_GROUP, rows)), gspec((rows, LANES)), gspec((LANES, rows))],
        out_shape=[jax.ShapeDtypeStruct((groups, 2 * SUB * SSM_GROUP, rows), bf16),
                   jax.ShapeDtypeStruct((groups, rows, LANES), bf16),
                   jax.ShapeDtypeStruct((groups, LANES, rows), bf16)],
        scratch_shapes=[pltpu.VMEM((TABLE_T, 4 * LANES), f32), pltpu.VMEM((rows, LANES), f32),
                        pltpu.VMEM((rows, LANES), f32)],
        compiler_params=_params(40, ("parallel",)),
        name="s5_tables",
    )(lre2, lim2, caa, cab, ba, bb, dd)


def _cmul(ar, ai, x):
    half = x.shape[0] // 2
    xr = x[:half]
    xi = x[half:]
    return jnp.concatenate([ar * xr - ai * xi, ar * xi + ai * xr], axis=0)


def _s5_group(z, tt, ca, wb, ap, h0, nc, rows):
    pair = 2 * SUB * SSM_GROUP
    table_rows = tt.shape[1]
    ys = []
    for t2 in range(rows // pair):
        kk = pair * (t2 + 1)
        ys.append(jnp.dot(tt[:, table_rows - kk:], z[:kk], preferred_element_type=f32))
    y = jnp.concatenate(ys, axis=0) if len(ys) > 1 else ys[0]
    state = jnp.dot(wb[:, table_rows - rows:], z, preferred_element_type=f32)
    if h0 is not None:
        state = state + _cmul(ap[:, 0:1], ap[:, 1:2], h0)
    lane = lax.broadcasted_iota(jnp.int32, state.shape, 1) % nc
    step = 0
    while (1 << step) < nc:
        sh = 1 << step
        shifted = jnp.where(lane >= sh, pltpu.roll(state, sh, axis=1), 0.0)
        state = state + _cmul(ap[:, 2 + 2 * step:3 + 2 * step], ap[:, 3 + 2 * step:4 + 2 * step], shifted)
        step += 1
    if nc > 1:
        h_in = jnp.where(lane >= 1, pltpu.roll(state, 1, axis=1), 0.0)
        if h0 is not None:
            h_in = h_in + h0
    else:
        h_in = h0 if h0 is not None else jnp.zeros_like(state)
    y = y + jnp.dot(ca[:rows, :], h_in.astype(bf16), preferred_element_type=f32)
    return y, state


def _s5_kernel(ut_ref, tt_ref, ca_ref, wb_ref, ap_ref, *rest, nc, has_h0):
    h0_ref = rest[0] if has_h0 else None
    y_ref, hfin_ref, yt_ref = rest[-3:]
    t_len, per, _, bc = ut_ref.shape
    rows = t_len * SSM_GROUP
    for g in range(per):
        z = ut_ref[:, g].reshape(rows, bc)
        y, state = _s5_group(z, tt_ref.at[g], ca_ref.at[g], wb_ref.at[g], ap_ref[g],
                             h0_ref[g] if has_h0 else None, nc, rows)
        hfin_ref[g] = state
        yt_ref[:, g * SSM_GROUP:(g + 1) * SSM_GROUP, :] = y.reshape(t_len, SSM_GROUP, bc)
    for t in range(t_len):
        y_ref[t] = yt_ref[t].T.astype(bf16)


def _s5(ut4, tt, ca, wb, apow, h0, nc):
    t_len, groups, _, bc = ut4.shape
    per = LANES // SSM_GROUP
    rows = t_len * SSM_GROUP
    last = TABLE_T * SSM_GROUP // rows - 1
    assert (last + 1) * rows == TABLE_T * SSM_GROUP
    mix = groups * SSM_GROUP
    ncol = apow.shape[-1]
    state_spec = pl.BlockSpec((per, 2 * SSM_STATE, bc), lambda G: (G, 0, 0))
    operands = (ut4, tt, ca, wb, apow) + (() if h0 is None else (h0,))
    return pl.pallas_call(
        functools.partial(_s5_kernel, nc=nc, has_h0=h0 is not None),
        grid=(groups // per,),
        in_specs=[pl.BlockSpec((t_len, per, SSM_GROUP, bc), lambda G: (0, G, 0, 0)),
                  pl.BlockSpec((per, 2 * SUB * SSM_GROUP, rows), lambda G: (G, 0, last)),
                  pl.BlockSpec((per, rows, LANES), lambda G: (G, 0, 0)),
                  pl.BlockSpec((per, LANES, rows), lambda G: (G, 0, last)),
                  pl.BlockSpec((per, SSM_STATE, ncol), lambda G: (G, 0, 0))] + ([] if h0 is None else [state_spec]),
        out_specs=[pl.BlockSpec((t_len, bc, LANES), lambda G: (0, 0, G)),
                   pl.BlockSpec((per, 2 * SSM_STATE, bc), lambda G: (G, 0, 0))],
        out_shape=[jax.ShapeDtypeStruct((t_len, bc, mix), bf16),
                   jax.ShapeDtypeStruct((groups, 2 * SSM_STATE, bc), f32)],
        scratch_shapes=[pltpu.VMEM((t_len, LANES, bc), f32)],
        compiler_params=_params(52, ("parallel",)),
        name="s5",
    )(*operands)


def _ld(ref, start, n, c0, c1):
    if len(ref.shape) == 3:
        s, r = divmod(start, ref.shape[1])
        return ref[s, r:r + n, c0:c1]
    return ref[start:start + n, c0:c1]


def _post_kernel(y_ref, p_ref, x_ref, mk_ref, mv_ref, wglu_ref, bglu_ref, wout_ref, gpost_ref,
                 o_ref, cat_ref, *dma, glu, segs, mem_scale):
    mix = y_ref.shape[-1]
    memw = mk_ref.shape[-1]
    hd = memw // MEM_HEADS
    rows = cat_ref.shape[0]
    if dma:
        xbuf, xsem, obuf, osem = dma
        slot = _fetch_offset_rows(x_ref, xbuf, xsem)
    y = y_ref[...].reshape(rows, mix).astype(f32)
    gate = p_ref[:, :mix] if len(p_ref.shape) == 2 else p_ref[:, :, :mix].reshape(rows, mix)
    gate = gate.astype(f32)
    heads = [(b, h, pieces) for b, pieces in enumerate(segs) for h in range(MEM_HEADS)]
    scores = []
    for b, h, pieces in heads:
        cq = mix + h * hd
        q = jnp.concatenate([_ld(p_ref, st, n, cq, cq + hd) for st, n in pieces], axis=0)
        scores.append(lax.dot_general(q, mk_ref[b, :, h * hd:(h + 1) * hd], _NT, preferred_element_type=f32))
    nchunk = max(rows // 256, 1)
    cr = rows // nchunk
    ys = [y[c * cr:(c + 1) * cr] for c in range(nchunk)]
    if glu:
        ys = [_gelu_tanh(yc) for yc in ys]
        zzs = [jnp.dot(yc.astype(bf16), wglu_ref[...], preferred_element_type=f32) for yc in ys]
        ys = [yc * _sigmoid(zz + bglu_ref[...]) for yc, zz in zip(ys, zzs)]
    outs = []
    for c, yc in enumerate(ys):
        gc = gate[c * cr:(c + 1) * cr]
        main = (yc * (gc * _sigmoid(gc))).astype(bf16)
        outs.append(jnp.dot(main, wout_ref[:mix, :], preferred_element_type=f32))
    out = jnp.concatenate(outs, axis=0) if nchunk > 1 else outs[0]
    probs = []
    for s in scores:
        s = s * mem_scale
        e = jnp.exp(s - jnp.max(s, axis=-1, keepdims=True))
        probs.append((e / jnp.sum(e, axis=-1, keepdims=True)).astype(bf16))
    for (b, h, pieces), p in zip(heads, probs):
        cg = mix + memw + h * hd
        mg = jnp.concatenate([_ld(p_ref, st, n, cg, cg + hd) for st, n in pieces], axis=0).astype(f32)
        o = jnp.dot(p, mv_ref[b, :, h * hd:(h + 1) * hd], preferred_element_type=f32)
        om = (o * (mg * _sigmoid(mg))).astype(bf16)
        off = 0
        for st, n in pieces:
            cat_ref[st:st + n, h * hd:(h + 1) * hd] = om[off:off + n]
            off += n
    out = out + jnp.dot(cat_ref[...], wout_ref[mix:, :], preferred_element_type=f32)
    d = out.shape[-1]
    branch = _rms_scale(out) * gpost_ref[...]
    if not dma:
        o_ref[...] = (x_ref[...].reshape(rows, d) + branch).reshape(o_ref.shape)
        return
    i = pl.program_id(0)
    last = pl.num_programs(0) - 1

    @pl.when(i >= 2)
    def _():
        for c in _offset_row_copies(o_ref, obuf, osem, i - 2, slot, to_hbm=True):
            c.wait()

    obuf[slot] = (xbuf[slot].reshape(rows, d) + branch).reshape(obuf.shape[1:])
    for c in _offset_row_copies(o_ref, obuf, osem, i, slot, to_hbm=True):
        c.start()

    @pl.when(i == last)
    def _():
        for c in _offset_row_copies(o_ref, obuf, osem, i, slot, to_hbm=True):
            c.wait()

    @pl.when((i == last) & (i >= 1))
    def _():
        for c in _offset_row_copies(o_ref, obuf, osem, i - 1, 1 - slot, to_hbm=True):
            c.wait()


def _post(y, p, x, mk, mv, wglu, bglu, wout, gpost, *, glu, segs, grid, y_spec, p_spec, x_spec, mem_spec, rows,
          offsets_per_step=None):
    mem_scale = 1.0 / math.sqrt(mk.shape[-1] // MEM_HEADS)
    scratch = [pltpu.VMEM((rows, mk.shape[-1]), bf16)]
    semantics = ("parallel",)
    if x_spec is None:
        x_spec = pl.BlockSpec(memory_space=pl.ANY)
        buf = pltpu.VMEM((2, offsets_per_step, x.shape[0], x.shape[2]), f32)
        scratch += [buf, pltpu.SemaphoreType.DMA((2,)), buf, pltpu.SemaphoreType.DMA((2,))]
        semantics = ("arbitrary",)
    return pl.pallas_call(
        functools.partial(_post_kernel, glu=glu, segs=segs, mem_scale=mem_scale),
        grid=grid,
        in_specs=[y_spec, p_spec, x_spec, mem_spec, mem_spec,
                  _resident(wglu.shape), _resident(bglu.shape), _resident(wout.shape), _resident(gpost.shape)],
        out_specs=x_spec,
        out_shape=jax.ShapeDtypeStruct(x.shape, f32),
        scratch_shapes=scratch,
        compiler_params=_params(52, semantics),
        name="post_glu" if glu else "post",
    )(y, p, x, mk, mv, wglu, bglu, wout, gpost)


def _kvb_kernel(x_ref, gkv_ref, gb_ref, wkv_ref, wb_ref, k_ref, v_ref, kvb_ref, q_ref, pr_ref):
    xs = _rms_scale(x_ref[...])
    hkv = (xs * gkv_ref[...]).astype(bf16)
    hb = (xs * gb_ref[...]).astype(bf16)
    nst, heads, rps, hd = k_ref.shape
    mix = heads * hd
    step = 512
    for c in range(0, wkv_ref.shape[1], step):
        r = jnp.dot(hkv, wkv_ref[:, c:c + step], preferred_element_type=f32)
        kvb_ref[:, c:c + step] = r.astype(bf16)
        for j in range(step // hd):
            h = (c % mix) // hd + j
            (k_ref if c < mix else v_ref)[:, h] = r[:, j * hd:(j + 1) * hd].reshape(nst, rps, hd)
    for c in range(0, wb_ref.shape[1], step):
        r = jnp.dot(hb, wb_ref[:, c:c + step], preferred_element_type=f32).astype(bf16)
        if c < mix:
            q_ref[:, c:c + step] = r
        else:
            pr_ref[:, c - mix:c - mix + step] = r


def _kvb(x, gkv, gb, wkv, wb, mix, streams):
    n, d = x.shape
    seq = n // streams
    tm = min(256, n)
    rps = min(tm, seq)
    assert tm % rps == 0 and seq % rps == 0 and rps % 8 == 0
    per_stream = seq // rps
    rest = wb.shape[1] - mix
    heads = mix // SB_HEAD_DIM
    row = lambda w: pl.BlockSpec((tm, w), lambda i: (i, 0))
    head_major = pl.BlockSpec((tm // rps, heads, rps, SB_HEAD_DIM), lambda i: (i // per_stream, 0, i % per_stream, 0))
    kv_shape = jax.ShapeDtypeStruct((streams, heads, seq, SB_HEAD_DIM), f32)
    return pl.pallas_call(
        _kvb_kernel,
        grid=(n // tm,),
        in_specs=[row(d), _resident(gkv.shape), _resident(gb.shape), _resident(wkv.shape), _resident(wb.shape)],
        out_specs=[head_major, head_major, row(2 * mix), row(mix), row(rest)],
        out_shape=[kv_shape, kv_shape,
                   jax.ShapeDtypeStruct((n, 2 * mix), bf16), jax.ShapeDtypeStruct((n, mix), bf16),
                   jax.ShapeDtypeStruct((n, rest), bf16)],
        compiler_params=_params(56, ("parallel",)),
        name="kvb",
    )(x, gkv, gb, wkv, wb)


_MASKED = -1e30


def _sb_softplus_tri(z, ntri, mask):
    sp = jnp.maximum(z, 0.0) + jnp.log2(1.0 + jnp.exp2(-jnp.abs(z)))
    if mask is not None:
        sp = jnp.where(mask, sp, 0.0)
    inner = jnp.dot(sp.astype(bf16), ntri, preferred_element_type=f32)
    t = (z - sp) + inner
    if mask is not None:
        t = jnp.where(mask, t, _MASKED)
    return t, inner[:, 0:1] - sp[:, 0:1]


def _sb_apply(t, d, v, carry, acc):
    w = jnp.exp2(t + carry)
    return acc + jnp.dot(w.astype(bf16), v, preferred_element_type=f32), carry + d


def _tri_and_mask(n):
    r = lax.broadcasted_iota(jnp.int32, (n, n), 0)
    c = lax.broadcasted_iota(jnp.int32, (n, n), 1)
    return jnp.where(r > c, -1.0, 0.0).astype(bf16), c < r


def _sb_attn_kernel(q_ref, k_ref, v_ref, o_ref, acc_ref, car_ref, t_ref, d_ref, *, nq):
    tq = ATTN_BLOCK
    i0 = pl.program_id(2) * nq
    ntri, causal = _tri_and_mask(tq)

    def rows(ref, kb):
        off = kb * tq if isinstance(kb, int) else pl.multiple_of(kb * tq, tq)
        return ref[pl.ds(off, tq), :]

    def pieces(r_lo, diag):
        out = []
        r = r_lo
        if diag:
            out.append((r * tq, (r + 1) * tq, causal))
            r += 1
        while r < nq:
            n = min(2, nq - r)
            out.append((r * tq, (r + n) * tq, None))
            r += n
        return out

    def step(prev, cur):
        zs = []
        if cur is not None:
            k = rows(k_ref, cur[0])
            for a, b, mask in pieces(cur[1], cur[2]):
                zs.append((a, b, mask, lax.dot_general(q_ref[a:b, :], k, _NT, preferred_element_type=f32)))
        if prev is not None:
            v = rows(v_ref, prev[0])
            for a, b, _ in pieces(prev[1], False):
                acc, carry = _sb_apply(t_ref[a:b, :], d_ref[a:b, :], v, car_ref[a:b, :], acc_ref[a:b, :])
                acc_ref[a:b, :] = acc
                car_ref[a:b, :] = carry
        for a, b, mask, z in zs:
            t, d = _sb_softplus_tri(z, ntri, mask)
            t_ref[a:b, :] = t
            d_ref[a:b, :] = d

    acc_ref[...] = jnp.zeros(acc_ref.shape, f32)
    car_ref[...] = jnp.zeros(car_ref.shape, f32)
    prev = None
    for p in range(nq):
        cur = (i0 + nq - 1 - p, nq - 1 - p, True)
        step(prev, cur)
        prev = cur

    def body(j, c):
        kb = i0 - 1 - 2 * j
        step((kb + 1, 0, False), (kb, 0, False))
        step((kb, 0, False), (kb - 1, 0, False))
        return c

    lax.fori_loop(0, i0 // 2, body, 0)
    step((0, 0, False), None)
    o_ref[...] = acc_ref[...].astype(bf16)


def _sb_attn(q, kvb, bsz, seq, heads):
    tq = ATTN_BLOCK
    hd = SB_HEAD_DIM
    nq = 4
    assert seq % (tq * nq) == 0
    steps = seq // (tq * nq)
    return pl.pallas_call(
        functools.partial(_sb_attn_kernel, nq=nq),
        grid=(bsz, heads, steps),
        in_specs=[pl.BlockSpec((nq * tq, hd), lambda b, h, i: (b * steps + i, h)),
                  pl.BlockSpec((seq, hd), lambda b, h, i: (b, h)),
                  pl.BlockSpec((seq, hd), lambda b, h, i: (b, heads + h))],
        out_specs=pl.BlockSpec((nq * tq, hd), lambda b, h, i: (b * steps + i, h)),
        out_shape=jax.ShapeDtypeStruct(q.shape, bf16),
        scratch_shapes=[pltpu.VMEM((nq * tq, hd), f32), pltpu.VMEM((nq * tq, 1), f32),
                        pltpu.VMEM((nq * tq, tq), f32), pltpu.VMEM((nq * tq, 1), f32)],
        compiler_params=_params(40, ("parallel", "parallel", "arbitrary")),
        name="sb_attn",
    )(q, kvb, kvb)


def _sb_attn_sample_kernel(q_ref, kn_ref, vn_ref, kc_ref, vc_ref, o_ref):
    tq = q_ref.shape[0]
    past = kc_ref.shape[0]
    blk = min(ATTN_BLOCK, past)
    q = q_ref[...]
    tri_n, causal = _tri_and_mask(tq)
    tri_p, _ = _tri_and_mask(blk)
    blocks = [(kn_ref[...], vn_ref[...], tri_n, causal)]
    for j in range(past // blk - 1, -1, -1):
        blocks.append((kc_ref[j * blk:(j + 1) * blk, :].astype(bf16), vc_ref[j * blk:(j + 1) * blk, :].astype(bf16),
                       tri_p, None))
    zs = [lax.dot_general(q, k, _NT, preferred_element_type=f32) for k, _, _, _ in blocks]
    tds = [_sb_softplus_tri(z, tri, mask) for z, (_, _, tri, mask) in zip(zs, blocks)]
    acc = jnp.zeros((tq, q_ref.shape[1]), f32)
    carry = jnp.zeros((tq, 1), f32)
    for (t, d), (_, v, _, _) in zip(tds, blocks):
        acc, carry = _sb_apply(t, d, v, carry, acc)
    o_ref[...] = acc.astype(bf16)


def _sb_attn_sample(q, kvb, cache_k, cache_v, tq):
    bsz, heads, past, hd = cache_k.shape
    cache = pl.BlockSpec((None, None, past, hd), lambda b, h: (b, h, 0, 0))
    return pl.pallas_call(
        _sb_attn_sample_kernel,
        grid=(bsz, heads),
        in_specs=[pl.BlockSpec((tq, hd), lambda b, h: (b, h)),
                  pl.BlockSpec((tq, hd), lambda b, h: (b, h)),
                  pl.BlockSpec((tq, hd), lambda b, h: (b, heads + h)),
                  cache, cache],
        out_specs=pl.BlockSpec((tq, hd), lambda b, h: (b, h)),
        out_shape=jax.ShapeDtypeStruct(q.shape, bf16),
        compiler_params=_params(40, ("parallel", "parallel")),
        name="sb_attn_sample",
    )(q, kvb, kvb, cache_k, cache_v)


def _ssm_param_tables(lam_re, lam_im, log_dt, b_re, b_im, c_re, c_im, dvec):
    dt = jnp.exp(log_dt.astype(f32))[:, None]
    lr = lam_re.astype(f32)
    li = lam_im.astype(f32)
    lre = lr * dt
    lim = li * dt
    mag = jnp.exp(lre)
    nr = mag * jnp.cos(lim) - 1.0
    ni = mag * jnp.sin(lim)
    den = lr * lr + li * li
    fr = ((nr * lr + ni * li) / den)[..., None]
    fi = ((ni * lr - nr * li) / den)[..., None]
    b_r = b_re.astype(f32)
    b_i = b_im.astype(f32)
    br = jnp.swapaxes(fr * b_r - fi * b_i, 1, 2)
    bi = jnp.swapaxes(fr * b_i + fi * b_r, 1, 2)
    cr = c_re.astype(f32)
    ci = c_im.astype(f32)
    dup = lambda v: jnp.concatenate([v, v], axis=-1)
    lre2 = dup(lre)[:, None, :]
    lim2 = dup(lim)[:, None, :]
    caa = jnp.concatenate([cr, -ci], axis=-1)
    cab = jnp.concatenate([-ci, -cr], axis=-1)
    ba = jnp.concatenate([br, bi], axis=-1)
    bb = jnp.concatenate([-bi, br], axis=-1)
    dd = dvec.astype(f32)[:, :, None]
    return lre, lim, (lre2, lim2, caa, cab, ba, bb, dd)


def _chunk_powers(lre, lim, t_len, nc):
    cols = []
    exps = [t_len]
    j = 0
    while (1 << j) < nc:
        exps.append(t_len * (1 << j))
        j += 1
    for e in exps:
        mag = jnp.exp(lre * e)
        cols += [mag * jnp.cos(lim * e), mag * jnp.sin(lim * e)]
    return jnp.stack(cols, axis=-1)


def _layer_a(x, t_len, nc, h0_lanes, mk, mv, wa, tables, lre, lim, prompt):
    n, d = x.shape
    mix = wa["w_glu"].shape[0]
    groups = mix // SSM_GROUP
    tt, ca, wb = tables
    apow = _chunk_powers(lre, lim, t_len, nc)
    if prompt:
        bc = n // t_len
        ns = 2
        x3 = x.reshape(bc, t_len, d)
        ut4, p = _inproj_a(x3, wa["g_pre"], wa["w_ut"], wa["w_rest"], ns)
        y, hfin = _s5(ut4, tt, ca, wb, apow, h0_lanes, nc)
        streams = bc // nc
        rest = p.shape[-1]
        segs = tuple(tuple((s * bc + b * nc, nc) for s in range(ns)) for b in range(streams))
        blk = lambda w: pl.BlockSpec((ns, bc, w), lambda i: (i, 0, 0))
        x1 = _post(y, p, x3, mk, mv, wa["w_glu"], wa["b_glu"], wa["w_out"], wa["g_post"],
                   glu=True, segs=segs, grid=(t_len // ns,), rows=ns * bc, offsets_per_step=ns,
                   y_spec=blk(mix), p_spec=blk(rest), x_spec=None, mem_spec=_resident(mk.shape))
        return x1.reshape(n, d), hfin
    streams = n // t_len
    lanes = h0_lanes.shape[-1]
    u, p = _inproj_plain(x, wa["g_pre"], wa["w_ut"], wa["w_rest"])
    ut4 = jnp.transpose(u.reshape(streams, t_len, groups, SSM_GROUP), (1, 2, 3, 0))
    ut4 = jnp.pad(ut4, ((0, 0), (0, 0), (0, 0), (0, lanes - streams)))
    y3, hfin = _s5(ut4, tt, ca, wb, apow, h0_lanes, nc)
    y = jnp.transpose(y3[:, :streams, :], (1, 0, 2)).reshape(n, mix)
    rest = p.shape[-1]
    whole = lambda w: pl.BlockSpec((n, w), lambda i: (0, 0))
    x1 = _post(y, p, x, mk, mv, wa["w_glu"], wa["b_glu"], wa["w_out"], wa["g_post"],
               glu=True, segs=tuple(((b * t_len, t_len),) for b in range(streams)), grid=(1,), rows=n,
               y_spec=whole(mix), p_spec=whole(rest), x_spec=whole(d), mem_spec=_resident(mk.shape))
    return x1, hfin


def kernel(x_prompt, x_sample, cache_k, cache_v, cache_mem_k, cache_mem_v, state_ssm, mem_prompt, w_in_a, w_out_a, g_pre_a, g_post_a, ssm_lam_re, ssm_lam_im, ssm_log_dt, ssm_b_re, ssm_b_im, ssm_c_re, ssm_c_im, ssm_d, w_glu, b_glu, g_kv, w_kv, w_in_b, w_out_b, g_pre_b, g_post_b, w_mem_k, w_mem_v):
    bsz, seq, d = x_prompt.shape
    dbsz, dseq, _ = x_sample.shape
    mix = w_glu.shape[-1]
    memw = w_mem_k.shape[-1]
    heads = mix // SB_HEAD_DIM
    groups = mix // SSM_GROUP
    n_mem = mem_prompt.shape[1]
    depth = w_mem_k.shape[0]
    assert depth == 2 and w_in_a.shape[0] == 1 and w_in_b.shape[0] == 1
    assert seq % CHUNK == 0 and (bsz * seq // CHUNK) % LANES == 0 and seq % ATTN_BLOCK == 0
    assert dseq % (2 * SUB) == 0 and dseq <= TABLE_T and dbsz <= LANES
    nc = seq // CHUNK
    assert nc & (nc - 1) == 0

    row = lambda v: v.astype(f32).reshape(1, -1)
    wa = dict(
        w_ut=w_in_a[0][:, :mix].T.astype(bf16),
        w_rest=w_in_a[0][:, mix:].astype(bf16),
        w_glu=w_glu[0].astype(bf16), b_glu=row(b_glu[0]),
        w_out=w_out_a[0].astype(bf16), g_pre=row(g_pre_a[0]), g_post=row(g_post_a[0]))
    qscale = math.log2(math.e) / math.sqrt(SB_HEAD_DIM)
    col_scale = jnp.where(jnp.arange(w_in_b.shape[-1]) < mix, qscale, 1.0).astype(f32)
    w_b = (w_in_b[0] * col_scale).astype(bf16)
    w_kv_b = w_kv.astype(bf16)
    w_out_bb = w_out_b[0].astype(bf16)
    w_mem = jnp.concatenate([w_mem_k[0], w_mem_k[1], w_mem_v[0], w_mem_v[1]], axis=1).astype(bf16)

    memf, memb = _memkv(mem_prompt.reshape(bsz * n_mem, d), w_mem, memw)
    mem_k_prompt = memf[:depth].reshape(depth, bsz, n_mem, MEM_HEADS, memw // MEM_HEADS)
    mem_v_prompt = memf[depth:].reshape(depth, bsz, n_mem, MEM_HEADS, memw // MEM_HEADS)
    mkp = memb[:depth].reshape(depth, bsz, n_mem, memw)
    mvp = memb[depth:].reshape(depth, bsz, n_mem, memw)
    mks = cache_mem_k.reshape(depth, dbsz, n_mem, memw).astype(bf16)
    mvs = cache_mem_v.reshape(depth, dbsz, n_mem, memw).astype(bf16)

    lre, lim, tab_in = _ssm_param_tables(ssm_lam_re[0], ssm_lam_im[0], ssm_log_dt[0], ssm_b_re[0], ssm_b_im[0],
                                         ssm_c_re[0], ssm_c_im[0], ssm_d[0])
    tables = _s5_tables(*tab_in)

    n_p = bsz * seq
    x1_p, hfin_p = _layer_a(x_prompt.reshape(n_p, d), CHUNK, nc, None, mkp[0], mvp[0], wa, tables, lre, lim, True)
    k_p, v_p, kvb_p, q_p, pr_p = _kvb(x1_p, row(g_kv), row(g_pre_b[0]), w_kv_b, w_b, mix, bsz)
    o_p = _sb_attn(q_p, kvb_p, bsz, seq, heads)
    rows_b = 2 * ATTN_BLOCK if seq % (2 * ATTN_BLOCK) == 0 else ATTN_BLOCK
    per_b = seq // rows_b
    tile = lambda w: pl.BlockSpec((rows_b, w), lambda i: (i, 0))
    y_p = _post(o_p, pr_p, x1_p, mkp[1], mvp[1], wa["w_glu"], wa["b_glu"], w_out_bb, row(g_post_b[0]),
                glu=False, segs=(((0, rows_b),),), grid=(n_p // rows_b,), rows=rows_b,
                y_spec=tile(mix), p_spec=tile(pr_p.shape[-1]), x_spec=tile(d),
                mem_spec=pl.BlockSpec((1, n_mem, memw), lambda i: (i // per_b, 0, 0)))

    n_s = dbsz * dseq
    st = state_ssm[0].astype(f32)
    h0_s = jnp.transpose(jnp.concatenate([st[..., 0], st[..., 1]], axis=-1), (1, 2, 0))
    h0_s = jnp.pad(h0_s, ((0, 0), (0, 0), (0, LANES - dbsz)))
    x1_s, hfin_s = _layer_a(x_sample.reshape(n_s, d), dseq, 1, h0_s, mks[0], mvs[0], wa, tables, lre, lim, False)
    k_s, v_s, kvb_s, q_s, pr_s = _kvb(x1_s, row(g_kv), row(g_pre_b[0]), w_kv_b, w_b, mix, dbsz)
    head_major = lambda a: jnp.transpose(a, (0, 2, 1, 3))
    o_s = _sb_attn_sample(q_s, kvb_s, head_major(cache_k), head_major(cache_v), dseq)
    whole = lambda w: pl.BlockSpec((n_s, w), lambda i: (0, 0))
    y_s = _post(o_s, pr_s, x1_s, mks[1], mvs[1], wa["w_glu"], wa["b_glu"], w_out_bb, row(g_post_b[0]),
                glu=False, segs=tuple(((b * dseq, dseq),) for b in range(dbsz)), grid=(1,), rows=n_s,
                y_spec=whole(mix), p_spec=whole(pr_s.shape[-1]), x_spec=whole(d),
                mem_spec=_resident(mks[1].shape))

    def ssm_out(hfin, lanes_idx):
        h = hfin[:, :, lanes_idx]
        h = jnp.transpose(h, (2, 0, 1))
        return jnp.stack([h[..., :SSM_STATE], h[..., SSM_STATE:]], axis=-1)[None]

    ssm_prompt = ssm_out(hfin_p, jnp.arange(bsz) * nc + (nc - 1)).astype(x_prompt.dtype)
    ssm_sample = ssm_out(hfin_s, jnp.arange(dbsz)).astype(state_ssm.dtype)
    return (y_p.reshape(bsz, seq, d), y_s.reshape(dbsz, dseq, d),
            head_major(k_p), head_major(v_p), head_major(k_s), head_major(v_s),
            ssm_prompt, ssm_sample, mem_k_prompt, mem_v_prompt)
```

```python
import functools
import math

import jax
import jax.numpy as jnp
from jax import lax
from jax.experimental import pallas as pl
from jax.experimental.pallas import tpu as pltpu

EPS = 1e-6
CHUNK = 64
SSM_GROUP = 16
SSM_STATE = 64
SB_HEAD_DIM = 128
MEM_HEADS = 4
SUB = 8
TABLE_T = 64
LANES = 128
ATTN_BLOCK = 256
MIB = 1024 * 1024

bf16 = jnp.bfloat16
f32 = jnp.float32

_NT = (((1,), (1,)), ((), ()))


def _params(vmem_mib, semantics):
    return pltpu.CompilerParams(vmem_limit_bytes=vmem_mib * MIB, dimension_semantics=semantics)


def _resident(shape):
    zeros = (0,) * len(shape)
    return pl.BlockSpec(shape, lambda *_: zeros, pipeline_mode=pl.Buffered(1))


def _rms_scale(x):
    return x * lax.rsqrt(jnp.mean(x * x, axis=-1, keepdims=True) + EPS)


def _sigmoid(x):
    return 1.0 / (1.0 + jnp.exp(-x))


def _gelu_tanh(x):
    c = math.sqrt(2.0 / math.pi)
    return 0.5 * x * (1.0 + jnp.tanh(c * (x + 0.044715 * (x * x * x))))


def _memkv_kernel(x_ref, w_ref, of_ref, ob_ref):
    acc = jnp.dot(x_ref[...].astype(bf16), w_ref[...], preferred_element_type=f32)
    width = of_ref.shape[-1]
    for j in range(of_ref.shape[0]):
        blk = acc[:, j * width:(j + 1) * width]
        of_ref[j] = blk
        ob_ref[j] = blk.astype(bf16)


def _memkv(mem, w_cat, width):
    rows, d = mem.shape
    nout = w_cat.shape[1] // width
    tm = 256
    return pl.pallas_call(
        _memkv_kernel,
        grid=(rows // tm,),
        in_specs=[pl.BlockSpec((tm, d), lambda i: (i, 0)), _resident(w_cat.shape)],
        out_specs=[pl.BlockSpec((nout, tm, width), lambda i: (0, i, 0)),
                   pl.BlockSpec((nout, tm, width), lambda i: (0, i, 0))],
        out_shape=[jax.ShapeDtypeStruct((nout, rows, width), f32),
                   jax.ShapeDtypeStruct((nout, rows, width), bf16)],
        compiler_params=_params(40, ("parallel",)),
        name="memkv",
    )(mem, w_cat)


def _offset_row_copies(x_hbm, buf, sem, step, slot, to_hbm=False):
    ns = buf.shape[1]
    out = []
    for s in range(ns):
        hbm = x_hbm.at[:, step * ns + s, :]
        vmem = buf.at[slot, s]
        out.append(pltpu.make_async_copy(vmem, hbm, sem.at[slot]) if to_hbm
                   else pltpu.make_async_copy(hbm, vmem, sem.at[slot]))
    return out


def _fetch_offset_rows(x_hbm, buf, sem):
    i = pl.program_id(0)

    @pl.when(i == 0)
    def _():
        for c in _offset_row_copies(x_hbm, buf, sem, 0, 0):
            c.start()

    @pl.when(i + 1 < pl.num_programs(0))
    def _():
        for c in _offset_row_copies(x_hbm, buf, sem, i + 1, (i + 1) % 2):
            c.start()

    slot = i % 2
    for c in _offset_row_copies(x_hbm, buf, sem, i, slot):
        c.wait()
    return slot


def _inproj_a_kernel(x_hbm, g_ref, wut_ref, wr_ref, ut_ref, p_ref, hn_ref, xbuf, xsem):
    _, ns, bc, _ = xbuf.shape
    slot = _fetch_offset_rows(x_hbm, xbuf, xsem)
    for s in range(ns):
        hn = (_rms_scale(xbuf[slot, s]) * g_ref[...]).astype(bf16)
        hn_ref[s * bc:(s + 1) * bc, :] = hn
        ut = lax.dot_general(wut_ref[...], hn, _NT, preferred_element_type=f32)
        ut_ref[s] = ut.reshape(ut_ref.shape[1:]).astype(bf16)
    step = 512
    for c in range(0, wr_ref.shape[1], step):
        r = jnp.dot(hn_ref[...], wr_ref[:, c:c + step], preferred_element_type=f32)
        p_ref[:, :, c:c + step] = r.astype(bf16).reshape(ns, bc, step)


def _inproj_a(x3, g, w_ut, w_rest, ns):
    bc, t, d = x3.shape
    mix = w_ut.shape[0]
    rest = w_rest.shape[1]
    groups = mix // SSM_GROUP
    return pl.pallas_call(
        _inproj_a_kernel,
        grid=(t // ns,),
        in_specs=[pl.BlockSpec(memory_space=pl.ANY),
                  _resident(g.shape), _resident(w_ut.shape), _resident(w_rest.shape)],
        out_specs=[pl.BlockSpec((ns, groups, SSM_GROUP, bc), lambda i: (i, 0, 0, 0)),
                   pl.BlockSpec((ns, bc, rest), lambda i: (i, 0, 0))],
        out_shape=[jax.ShapeDtypeStruct((t, groups, SSM_GROUP, bc), bf16),
                   jax.ShapeDtypeStruct((t, bc, rest), bf16)],
        scratch_shapes=[pltpu.VMEM((ns * bc, d), bf16), pltpu.VMEM((2, ns, bc, d), f32),
                        pltpu.SemaphoreType.DMA((2,))],
        compiler_params=_params(52, ("arbitrary",)),
        name="inproj_a",
    )(x3, g, w_ut, w_rest)


def _inproj_plain_kernel(x_ref, g_ref, wut_ref, wr_ref, u_ref, p_ref):
    hn = (_rms_scale(x_ref[...]) * g_ref[...]).astype(bf16)
    u_ref[...] = lax.dot_general(hn, wut_ref[...], _NT, preferred_element_type=f32).astype(bf16)
    p_ref[...] = jnp.dot(hn, wr_ref[...], preferred_element_type=f32).astype(bf16)


def _inproj_plain(x, g, w_ut, w_rest):
    rows, d = x.shape
    mix = w_ut.shape[0]
    rest = w_rest.shape[1]
    return pl.pallas_call(
        _inproj_plain_kernel,
        grid=(1,),
        in_specs=[pl.BlockSpec((rows, d), lambda i: (0, 0)), _resident(g.shape), _resident(w_ut.shape),
                  _resident(w_rest.shape)],
        out_specs=[pl.BlockSpec((rows, mix), lambda i: (0, 0)), pl.BlockSpec((rows, rest), lambda i: (0, 0))],
        out_shape=[jax.ShapeDtypeStruct((rows, mix), bf16), jax.ShapeDtypeStruct((rows, rest), bf16)],
        compiler_params=_params(40, ("arbitrary",)),
        name="inproj_plain",
    )(x, g, w_ut, w_rest)


def _s5_tables_kernel(*refs):
    for g in range(refs[0].shape[0]):
        _s5_tables_group(*[r.at[g] for r in refs])


def _s5_tables_group(lre_ref, lim_ref, caa_ref, cab_ref, ba_ref, bb_ref, dd_ref,
                     tt_ref, ca_ref, wb_ref, pw_ref, cas_ref, wbs_ref):
    t_len = TABLE_T
    rows = t_len * SSM_GROUP
    lre = lre_ref[...]
    lim = lim_ref[...]
    assert t_len == SUB * SUB
    k8 = lax.broadcasted_iota(jnp.int32, (SUB, LANES), 0).astype(f32)

    def powers(k):
        mag = jnp.exp(lre * k)
        th = lim * k
        return mag * jnp.cos(th), mag * jnp.sin(th)

    def outer(hi, lo):
        hr = jnp.concatenate([jnp.broadcast_to(hi[0][m:m + 1, :], (SUB, LANES)) for m in range(SUB)], axis=0)
        hi_ = jnp.concatenate([jnp.broadcast_to(hi[1][m:m + 1, :], (SUB, LANES)) for m in range(SUB)], axis=0)
        lr = jnp.concatenate([lo[0]] * SUB, axis=0)
        li = jnp.concatenate([lo[1]] * SUB, axis=0)
        return hr * lr - hi_ * li, hr * li + hi_ * lr

    low = powers(k8)
    kr, ki = outer(powers(SUB * k8), low)
    a1r, a1i = low[0][1:2, :], low[1][1:2, :]
    pr1, pi1 = kr * a1r - ki * a1i, kr * a1i + ki * a1r
    pr0, pi0 = outer(powers(SUB * (SUB - 1.0 - k8)), powers(SUB - 1.0 - k8))
    pw_ref[...] = jnp.concatenate([pr1, pi1, pr0, pi0], axis=1)

    def tile_rows(v, n):
        return jnp.broadcast_to(v[None], (n,) + v.shape).reshape(n * v.shape[0], v.shape[1])

    caa = caa_ref[...]
    cab = cab_ref[...]
    ba = ba_ref[...]
    bb = bb_ref[...]
    for t in range(t_len):
        pw = jnp.broadcast_to(pw_ref[t:t + 1, :], (SSM_GROUP, 4 * LANES))
        r0 = t * SSM_GROUP
        cas_ref[r0:r0 + SSM_GROUP, :] = pw[:, 0:128] * caa + pw[:, 128:256] * cab
        wbs_ref[r0:r0 + SSM_GROUP, :] = pw[:, 256:384] * ba + pw[:, 384:512] * bb
    ca = cas_ref[...]
    wbt = wbs_ref[...]
    ca_ref[...] = ca.astype(bf16)
    wb_ref[...] = wbt.T.astype(bf16)

    blk = SUB * SSM_GROUP
    nblk = rows // blk
    rt = wbt[rows - blk:, :]
    lagged = lax.dot_general(ca[:(nblk - 1) * blk, :], rt, _NT, preferred_element_type=f32,
                             precision=lax.Precision.HIGHEST)
    m = [None] + [lagged[(d - 1) * blk:d * blk, :] for d in range(1, nblk)]
    ca0 = jnp.concatenate([caa, ca[:blk - SSM_GROUP, :]], axis=0)
    kj = lax.dot_general(ca0, tile_rows(ba, SUB), _NT, preferred_element_type=f32,
                         precision=lax.Precision.HIGHEST)
    lane = lax.broadcasted_iota(jnp.int32, (SSM_GROUP, LANES), 1)
    hrow = lax.broadcasted_iota(jnp.int32, (SSM_GROUP, LANES), 0)
    s0_lane = lane // SSM_GROUP
    skip = jnp.where(lane % SSM_GROUP == hrow, dd_ref[...], 0.0)
    kjs = [kj[j * SSM_GROUP:(j + 1) * SSM_GROUP, :] for j in range(SUB)]
    kjs[0] = kjs[0] + skip
    drows = []
    for t0 in range(SUB):
        acc = jnp.zeros((SSM_GROUP, LANES), f32)
        for j in range(t0 + 1):
            acc = acc + jnp.where(s0_lane == t0 - j, kjs[j], 0.0)
        drows.append(acc)
    m[0] = jnp.concatenate(drows, axis=0)
    zero = jnp.zeros((blk, blk), f32)
    r1 = jnp.concatenate([m[d] for d in range(nblk - 1, -1, -1)], axis=1)
    r0 = jnp.concatenate([m[d] for d in range(nblk - 2, -1, -1)] + [zero], axis=1)
    tt_ref[...] = jnp.concatenate([r0, r1], axis=0).astype(bf16)


def _s5_tables(lre2, lim2, caa, cab, ba, bb, dd):
    groups = lre2.shape[0]
    rows = TABLE_T * SSM_GROUP
    per = 4 if groups % 4 == 0 else 1

    def gspec(shape):
        return pl.BlockSpec((per,) + shape, lambda g: (g,) + (0,) * len(shape))

    return pl.pallas_call(
        _s5_tables_kernel,
        grid=(groups // per,),
        in_specs=[gspec((1, LANES)), gspec((1, LANES)), gspec((SSM_GROUP, LANES)), gspec((SSM_GROUP, LANES)),
                  gspec((SSM_GROUP, LANES)), gspec((SSM_GROUP, LANES)), gspec((SSM_GROUP, 1))],
        out_specs=[gspec((2 * SUB * SSM_GROUP, rows)), gspec((rows, LANES)), gspec((LANES, rows))],
        out_shape=[jax.ShapeDtypeStruct((groups, 2 * SUB * SSM_GROUP, rows), bf16),
                   jax.ShapeDtypeStruct((groups, rows, LANES), bf16),
                   jax.ShapeDtypeStruct((groups, LANES, rows), bf16)],
        scratch_shapes=[pltpu.VMEM((per, TABLE_T, 4 * LANES), f32), pltpu.VMEM((per, rows, LANES), f32),
                        pltpu.VMEM((per, rows, LANES), f32)],
        compiler_params=_params(40, ("parallel",)),
        name="s5_tables",
    )(lre2, lim2, caa, cab, ba, bb, dd)


def _cmul(ar, ai, x):
    half = x.shape[0] // 2
    xr = x[:half]
    xi = x[half:]
    return jnp.concatenate([ar * xr - ai * xi, ar * xi + ai * xr], axis=0)


def _s5_group(z, tt, ca, wb, ap, h0, nc, rows):
    pair = 2 * SUB * SSM_GROUP
    table_rows = tt.shape[1]
    ys = []
    for t2 in range(rows // pair):
        kk = pair * (t2 + 1)
        ys.append(jnp.dot(tt[:, table_rows - kk:], z[:kk], preferred_element_type=f32))
    y = jnp.concatenate(ys, axis=0) if len(ys) > 1 else ys[0]
    state = jnp.dot(wb[:, table_rows - rows:], z, preferred_element_type=f32)
    if h0 is not None:
        state = state + _cmul(ap[:, 0:1], ap[:, 1:2], h0)
    lane = lax.broadcasted_iota(jnp.int32, state.shape, 1) % nc
    step = 0
    while (1 << step) < nc:
        sh = 1 << step
        shifted = jnp.where(lane >= sh, pltpu.roll(state, sh, axis=1), 0.0)
        state = state + _cmul(ap[:, 2 + 2 * step:3 + 2 * step], ap[:, 3 + 2 * step:4 + 2 * step], shifted)
        step += 1
    if nc > 1:
        h_in = jnp.where(lane >= 1, pltpu.roll(state, 1, axis=1), 0.0)
        if h0 is not None:
            h_in = h_in + h0
    else:
        h_in = h0 if h0 is not None else jnp.zeros_like(state)
    y = y + jnp.dot(ca[:rows, :], h_in.astype(bf16), preferred_element_type=f32)
    return y, state


def _s5_kernel(ut_ref, tt_ref, ca_ref, wb_ref, ap_ref, *rest, nc, has_h0):
    h0_ref = rest[0] if has_h0 else None
    y_ref, hfin_ref, yt_ref = rest[-3:]
    t_len, per, _, bc = ut_ref.shape
    rows = t_len * SSM_GROUP
    for g in range(per):
        z = ut_ref[:, g].reshape(rows, bc)
        y, state = _s5_group(z, tt_ref.at[g], ca_ref.at[g], wb_ref.at[g], ap_ref[g],
                             h0_ref[g] if has_h0 else None, nc, rows)
        hfin_ref[g] = state
        yt_ref[:, g * SSM_GROUP:(g + 1) * SSM_GROUP, :] = y.reshape(t_len, SSM_GROUP, bc)
    for t in range(t_len):
        y_ref[t] = yt_ref[t].T.astype(bf16)


def _s5(ut4, tt, ca, wb, apow, h0, nc):
    t_len, groups, _, bc = ut4.shape
    per = LANES // SSM_GROUP
    rows = t_len * SSM_GROUP
    last = TABLE_T * SSM_GROUP // rows - 1
    assert (last + 1) * rows == TABLE_T * SSM_GROUP
    mix = groups * SSM_GROUP
    ncol = apow.shape[-1]
    state_spec = pl.BlockSpec((per, 2 * SSM_STATE, bc), lambda G: (G, 0, 0))
    operands = (ut4, tt, ca, wb, apow) + (() if h0 is None else (h0,))
    return pl.pallas_call(
        functools.partial(_s5_kernel, nc=nc, has_h0=h0 is not None),
        grid=(groups // per,),
        in_specs=[pl.BlockSpec((t_len, per, SSM_GROUP, bc), lambda G: (0, G, 0, 0)),
                  pl.BlockSpec((per, 2 * SUB * SSM_GROUP, rows), lambda G: (G, 0, last)),
                  pl.BlockSpec((per, rows, LANES), lambda G: (G, 0, 0)),
                  pl.BlockSpec((per, LANES, rows), lambda G: (G, 0, last)),
                  pl.BlockSpec((per, SSM_STATE, ncol), lambda G: (G, 0, 0))] + ([] if h0 is None else [state_spec]),
        out_specs=[pl.BlockSpec((t_len, bc, LANES), lambda G: (0, 0, G)),
                   pl.BlockSpec((per, 2 * SSM_STATE, bc), lambda G: (G, 0, 0))],
        out_shape=[jax.ShapeDtypeStruct((t_len, bc, mix), bf16),
                   jax.ShapeDtypeStruct((groups, 2 * SSM_STATE, bc), f32)],
        scratch_shapes=[pltpu.VMEM((t_len, LANES, bc), f32)],
        compiler_params=_params(52, ("parallel",)),
        name="s5",
    )(*operands)


def _ld(ref, start, n, c0, c1):
    if len(ref.shape) == 3:
        s, r = divmod(start, ref.shape[1])
        return ref[s, r:r + n, c0:c1]
    return ref[start:start + n, c0:c1]


def _post_kernel(y_ref, p_ref, x_ref, mk_ref, mv_ref, wglu_ref, bglu_ref, wout_ref, gpost_ref,
                 o_ref, cat_ref, *dma, glu, segs, mem_scale):
    mix = y_ref.shape[-1]
    memw = mk_ref.shape[-1]
    hd = memw // MEM_HEADS
    rows = cat_ref.shape[0]
    if dma:
        xbuf, xsem, obuf, osem = dma
        slot = _fetch_offset_rows(x_ref, xbuf, xsem)
    y = y_ref[...].reshape(rows, mix).astype(f32)
    gate = p_ref[:, :mix] if len(p_ref.shape) == 2 else p_ref[:, :, :mix].reshape(rows, mix)
    gate = gate.astype(f32)
    heads = [(b, h, pieces) for b, pieces in enumerate(segs) for h in range(MEM_HEADS)]
    scores = []
    for b, h, pieces in heads:
        cq = mix + h * hd
        q = jnp.concatenate([_ld(p_ref, st, n, cq, cq + hd) for st, n in pieces], axis=0)
        scores.append(lax.dot_general(q, mk_ref[b, :, h * hd:(h + 1) * hd], _NT, preferred_element_type=f32))
    nchunk = max(rows // 256, 1)
    cr = rows // nchunk
    ys = [y[c * cr:(c + 1) * cr] for c in range(nchunk)]
    if glu:
        ys = [_gelu_tanh(yc) for yc in ys]
        zzs = [jnp.dot(yc.astype(bf16), wglu_ref[...], preferred_element_type=f32) for yc in ys]
        ys = [yc * _sigmoid(zz + bglu_ref[...]) for yc, zz in zip(ys, zzs)]
    outs = []
    for c, yc in enumerate(ys):
        gc = gate[c * cr:(c + 1) * cr]
        main = (yc * (gc * _sigmoid(gc))).astype(bf16)
        outs.append(jnp.dot(main, wout_ref[:mix, :], preferred_element_type=f32))
    out = jnp.concatenate(outs, axis=0) if nchunk > 1 else outs[0]
    probs = []
    for s in scores:
        s = s * mem_scale
        e = jnp.exp(s - jnp.max(s, axis=-1, keepdims=True))
        probs.append((e / jnp.sum(e, axis=-1, keepdims=True)).astype(bf16))
    for (b, h, pieces), p in zip(heads, probs):
        cg = mix + memw + h * hd
        mg = jnp.concatenate([_ld(p_ref, st, n, cg, cg + hd) for st, n in pieces], axis=0).astype(f32)
        o = jnp.dot(p, mv_ref[b, :, h * hd:(h + 1) * hd], preferred_element_type=f32)
        om = (o * (mg * _sigmoid(mg))).astype(bf16)
        off = 0
        for st, n in pieces:
            cat_ref[st:st + n, h * hd:(h + 1) * hd] = om[off:off + n]
            off += n
    out = out + jnp.dot(cat_ref[...], wout_ref[mix:, :], preferred_element_type=f32)
    d = out.shape[-1]
    branch = _rms_scale(out) * gpost_ref[...]
    if not dma:
        o_ref[...] = (x_ref[...].reshape(rows, d) + branch).reshape(o_ref.shape)
        return
    i = pl.program_id(0)
    last = pl.num_programs(0) - 1

    @pl.when(i >= 2)
    def _():
        for c in _offset_row_copies(o_ref, obuf, osem, i - 2, slot, to_hbm=True):
            c.wait()

    obuf[slot] = (xbuf[slot].reshape(rows, d) + branch).reshape(obuf.shape[1:])
    for c in _offset_row_copies(o_ref, obuf, osem, i, slot, to_hbm=True):
        c.start()

    @pl.when(i == last)
    def _():
        for c in _offset_row_copies(o_ref, obuf, osem, i, slot, to_hbm=True):
            c.wait()

    @pl.when((i == last) & (i >= 1))
    def _():
        for c in _offset_row_copies(o_ref, obuf, osem, i - 1, 1 - slot, to_hbm=True):
            c.wait()


def _post(y, p, x, mk, mv, wglu, bglu, wout, gpost, *, glu, segs, grid, y_spec, p_spec, x_spec, mem_spec, rows,
          offsets_per_step=None):
    mem_scale = 1.0 / math.sqrt(mk.shape[-1] // MEM_HEADS)
    scratch = [pltpu.VMEM((rows, mk.shape[-1]), bf16)]
    semantics = ("parallel",)
    if x_spec is None:
        x_spec = pl.BlockSpec(memory_space=pl.ANY)
        buf = pltpu.VMEM((2, offsets_per_step, x.shape[0], x.shape[2]), f32)
        scratch += [buf, pltpu.SemaphoreType.DMA((2,)), buf, pltpu.SemaphoreType.DMA((2,))]
        semantics = ("arbitrary",)
    return pl.pallas_call(
        functools.partial(_post_kernel, glu=glu, segs=segs, mem_scale=mem_scale),
        grid=grid,
        in_specs=[y_spec, p_spec, x_spec, mem_spec, mem_spec,
                  _resident(wglu.shape), _resident(bglu.shape), _resident(wout.shape), _resident(gpost.shape)],
        out_specs=x_spec,
        out_shape=jax.ShapeDtypeStruct(x.shape, f32),
        scratch_shapes=scratch,
        compiler_params=_params(52, semantics),
        name="post_glu" if glu else "post",
    )(y, p, x, mk, mv, wglu, bglu, wout, gpost)


def _kvb_kernel(x_ref, gkv_ref, gb_ref, wkv_ref, wb_ref, k_ref, v_ref, kvb_ref, q_ref, pr_ref):
    xs = _rms_scale(x_ref[...])
    hkv = (xs * gkv_ref[...]).astype(bf16)
    hb = (xs * gb_ref[...]).astype(bf16)
    nst, heads, rps, hd = k_ref.shape
    mix = heads * hd
    step = 512
    for c in range(0, wkv_ref.shape[1], step):
        r = jnp.dot(hkv, wkv_ref[:, c:c + step], preferred_element_type=f32)
        kvb_ref[:, c:c + step] = r.astype(bf16)
        for j in range(step // hd):
            h = (c % mix) // hd + j
            (k_ref if c < mix else v_ref)[:, h] = r[:, j * hd:(j + 1) * hd].reshape(nst, rps, hd)
    for c in range(0, wb_ref.shape[1], step):
        r = jnp.dot(hb, wb_ref[:, c:c + step], preferred_element_type=f32).astype(bf16)
        if c < mix:
            q_ref[:, c:c + step] = r
        else:
            pr_ref[:, c - mix:c - mix + step] = r


def _kvb(x, gkv, gb, wkv, wb, mix, streams):
    n, d = x.shape
    seq = n // streams
    tm = min(256, n)
    rps = min(tm, seq)
    assert tm % rps == 0 and seq % rps == 0 and rps % 8 == 0
    per_stream = seq // rps
    rest = wb.shape[1] - mix
    heads = mix // SB_HEAD_DIM
    row = lambda w: pl.BlockSpec((tm, w), lambda i: (i, 0))
    head_major = pl.BlockSpec((tm // rps, heads, rps, SB_HEAD_DIM), lambda i: (i // per_stream, 0, i % per_stream, 0))
    kv_shape = jax.ShapeDtypeStruct((streams, heads, seq, SB_HEAD_DIM), f32)
    return pl.pallas_call(
        _kvb_kernel,
        grid=(n // tm,),
        in_specs=[row(d), _resident(gkv.shape), _resident(gb.shape), _resident(wkv.shape), _resident(wb.shape)],
        out_specs=[head_major, head_major, row(2 * mix), row(mix), row(rest)],
        out_shape=[kv_shape, kv_shape,
                   jax.ShapeDtypeStruct((n, 2 * mix), bf16), jax.ShapeDtypeStruct((n, mix), bf16),
                   jax.ShapeDtypeStruct((n, rest), bf16)],
        compiler_params=_params(56, ("parallel",)),
        name="kvb",
    )(x, gkv, gb, wkv, wb)


_MASKED = -1e30


def _sb_softplus_tri(z, ntri, mask):
    sp = jnp.maximum(z, 0.0) + jnp.log2(1.0 + jnp.exp2(-jnp.abs(z)))
    if mask is not None:
        sp = jnp.where(mask, sp, 0.0)
    inner = jnp.dot(sp.astype(bf16), ntri, preferred_element_type=f32)
    t = (z - sp) + inner
    if mask is not None:
        t = jnp.where(mask, t, _MASKED)
    return t, inner[:, 0:1] - sp[:, 0:1]


def _sb_apply(t, d, v, carry, acc):
    w = jnp.exp2(t + carry)
    return acc + jnp.dot(w.astype(bf16), v, preferred_element_type=f32), carry + d


def _tri_and_mask(n):
    r = lax.broadcasted_iota(jnp.int32, (n, n), 0)
    c = lax.broadcasted_iota(jnp.int32, (n, n), 1)
    return jnp.where(r > c, -1.0, 0.0).astype(bf16), c < r


def _sb_attn_kernel(q_ref, k_ref, v_ref, o_ref, acc_ref, car_ref, t_ref, d_ref, *, nq):
    tq = ATTN_BLOCK
    i0 = pl.program_id(2) * nq
    ntri, causal = _tri_and_mask(tq)

    def rows(ref, kb):
        off = kb * tq if isinstance(kb, int) else pl.multiple_of(kb * tq, tq)
        return ref[pl.ds(off, tq), :]

    def pieces(r_lo, diag):
        out = []
        r = r_lo
        if diag:
            out.append((r * tq, (r + 1) * tq, causal))
            r += 1
        while r < nq:
            n = min(2, nq - r)
            out.append((r * tq, (r + n) * tq, None))
            r += n
        return out

    def step(prev, cur):
        zs = []
        if cur is not None:
            k = rows(k_ref, cur[0])
            for a, b, mask in pieces(cur[1], cur[2]):
                zs.append((a, b, mask, lax.dot_general(q_ref[a:b, :], k, _NT, preferred_element_type=f32)))
        if prev is not None:
            v = rows(v_ref, prev[0])
            for a, b, _ in pieces(prev[1], False):
                acc, carry = _sb_apply(t_ref[a:b, :], d_ref[a:b, :], v, car_ref[a:b, :], acc_ref[a:b, :])
                acc_ref[a:b, :] = acc
                car_ref[a:b, :] = carry
        for a, b, mask, z in zs:
            t, d = _sb_softplus_tri(z, ntri, mask)
            t_ref[a:b, :] = t
            d_ref[a:b, :] = d

    acc_ref[...] = jnp.zeros(acc_ref.shape, f32)
    car_ref[...] = jnp.zeros(car_ref.shape, f32)
    prev = None
    for p in range(nq):
        cur = (i0 + nq - 1 - p, nq - 1 - p, True)
        step(prev, cur)
        prev = cur

    def body(j, c):
        kb = i0 - 1 - 2 * j
        step((kb + 1, 0, False), (kb, 0, False))
        step((kb, 0, False), (kb - 1, 0, False))
        return c

    lax.fori_loop(0, i0 // 2, body, 0)
    step((0, 0, False), None)
    o_ref[...] = acc_ref[...].astype(bf16)


def _sb_attn(q, kvb, bsz, seq, heads):
    tq = ATTN_BLOCK
    hd = SB_HEAD_DIM
    nq = 4
    assert seq % (tq * nq) == 0
    steps = seq // (tq * nq)
    return pl.pallas_call(
        functools.partial(_sb_attn_kernel, nq=nq),
        grid=(bsz, heads, steps),
        in_specs=[pl.BlockSpec((nq * tq, hd), lambda b, h, i: (b * steps + i, h)),
                  pl.BlockSpec((seq, hd), lambda b, h, i: (b, h)),
                  pl.BlockSpec((seq, hd), lambda b, h, i: (b, heads + h))],
        out_specs=pl.BlockSpec((nq * tq, hd), lambda b, h, i: (b * steps + i, h)),
        out_shape=jax.ShapeDtypeStruct(q.shape, bf16),
        scratch_shapes=[pltpu.VMEM((nq * tq, hd), f32), pltpu.VMEM((nq * tq, 1), f32),
                        pltpu.VMEM((nq * tq, tq), f32), pltpu.VMEM((nq * tq, 1), f32)],
        compiler_params=_params(40, ("parallel", "parallel", "arbitrary")),
        name="sb_attn",
    )(q, kvb, kvb)


def _sb_attn_sample_kernel(q_ref, kn_ref, vn_ref, kc_ref, vc_ref, o_ref):
    tq = q_ref.shape[0]
    nh, past, hd = kc_ref.shape
    blk = min(ATTN_BLOCK, past)
    tri_n, causal = _tri_and_mask(tq)
    tri_p, _ = _tri_and_mask(blk)
    heads = []
    for h in range(nh):
        cols = slice(h * hd, (h + 1) * hd)
        blocks = [(kn_ref[:, cols], vn_ref[:, cols], tri_n, causal)]
        for j in range(past // blk - 1, -1, -1):
            blocks.append((kc_ref[h, j * blk:(j + 1) * blk, :].astype(bf16),
                           vc_ref[h, j * blk:(j + 1) * blk, :].astype(bf16), tri_p, None))
        heads.append((cols, blocks))
    zs = [[lax.dot_general(q_ref[:, cols], k, _NT, preferred_element_type=f32) for k, _, _, _ in blocks]
          for cols, blocks in heads]
    tds = [[_sb_softplus_tri(z, tri, mask) for z, (_, _, tri, mask) in zip(zh, blocks)]
           for zh, (_, blocks) in zip(zs, heads)]
    for td, (cols, blocks) in zip(tds, heads):
        acc = jnp.zeros((tq, hd), f32)
        carry = jnp.zeros((tq, 1), f32)
        for (t, d), (_, v, _, _) in zip(td, blocks):
            acc, carry = _sb_apply(t, d, v, carry, acc)
        o_ref[:, cols] = acc.astype(bf16)


def _sb_attn_sample(q, kvb, cache_k, cache_v, tq):
    bsz, heads, past, hd = cache_k.shape
    nh = 4 if heads % 4 == 0 else 1
    cache = pl.BlockSpec((None, nh, past, hd), lambda b, h: (b, h, 0, 0))
    groups = heads // nh
    return pl.pallas_call(
        _sb_attn_sample_kernel,
        grid=(bsz, groups),
        in_specs=[pl.BlockSpec((tq, nh * hd), lambda b, h: (b, h)),
                  pl.BlockSpec((tq, nh * hd), lambda b, h: (b, h)),
                  pl.BlockSpec((tq, nh * hd), lambda b, h: (b, groups + h)),
                  cache, cache],
        out_specs=pl.BlockSpec((tq, nh * hd), lambda b, h: (b, h)),
        out_shape=jax.ShapeDtypeStruct(q.shape, bf16),
        compiler_params=_params(40, ("parallel", "parallel")),
        name="sb_attn_sample",
    )(q, kvb, kvb, cache_k, cache_v)


def _ssm_param_tables(lam_re, lam_im, log_dt, b_re, b_im, c_re, c_im, dvec):
    dt = jnp.exp(log_dt.astype(f32))[:, None]
    lr = lam_re.astype(f32)
    li = lam_im.astype(f32)
    lre = lr * dt
    lim = li * dt
    mag = jnp.exp(lre)
    nr = mag * jnp.cos(lim) - 1.0
    ni = mag * jnp.sin(lim)
    den = lr * lr + li * li
    fr = ((nr * lr + ni * li) / den)[..., None]
    fi = ((ni * lr - nr * li) / den)[..., None]
    b_r = b_re.astype(f32)
    b_i = b_im.astype(f32)
    br = jnp.swapaxes(fr * b_r - fi * b_i, 1, 2)
    bi = jnp.swapaxes(fr * b_i + fi * b_r, 1, 2)
    cr = c_re.astype(f32)
    ci = c_im.astype(f32)
    dup = lambda v: jnp.concatenate([v, v], axis=-1)
    lre2 = dup(lre)[:, None, :]
    lim2 = dup(lim)[:, None, :]
    caa = jnp.concatenate([cr, -ci], axis=-1)
    cab = jnp.concatenate([-ci, -cr], axis=-1)
    ba = jnp.concatenate([br, bi], axis=-1)
    bb = jnp.concatenate([-bi, br], axis=-1)
    dd = dvec.astype(f32)[:, :, None]
    return lre, lim, (lre2, lim2, caa, cab, ba, bb, dd)


def _chunk_powers(lre, lim, t_len, nc):
    cols = []
    exps = [t_len]
    j = 0
    while (1 << j) < nc:
        exps.append(t_len * (1 << j))
        j += 1
    for e in exps:
        mag = jnp.exp(lre * e)
        cols += [mag * jnp.cos(lim * e), mag * jnp.sin(lim * e)]
    return jnp.stack(cols, axis=-1)


def _layer_a(x, t_len, nc, h0_lanes, mk, mv, wa, tables, lre, lim, prompt):
    n, d = x.shape
    mix = wa["w_glu"].shape[0]
    groups = mix // SSM_GROUP
    tt, ca, wb = tables
    apow = _chunk_powers(lre, lim, t_len, nc)
    if prompt:
        bc = n // t_len
        ns = 2
        x3 = x.reshape(bc, t_len, d)
        ut4, p = _inproj_a(x3, wa["g_pre"], wa["w_ut"], wa["w_rest"], ns)
        y, hfin = _s5(ut4, tt, ca, wb, apow, h0_lanes, nc)
        streams = bc // nc
        rest = p.shape[-1]
        segs = tuple(tuple((s * bc + b * nc, nc) for s in range(ns)) for b in range(streams))
        blk = lambda w: pl.BlockSpec((ns, bc, w), lambda i: (i, 0, 0))
        x1 = _post(y, p, x3, mk, mv, wa["w_glu"], wa["b_glu"], wa["w_out"], wa["g_post"],
                   glu=True, segs=segs, grid=(t_len // ns,), rows=ns * bc, offsets_per_step=ns,
                   y_spec=blk(mix), p_spec=blk(rest), x_spec=None, mem_spec=_resident(mk.shape))
        return x1.reshape(n, d), hfin
    streams = n // t_len
    lanes = h0_lanes.shape[-1]
    u, p = _inproj_plain(x, wa["g_pre"], wa["w_ut"], wa["w_rest"])
    ut4 = jnp.transpose(u.reshape(streams, t_len, groups, SSM_GROUP), (1, 2, 3, 0))
    ut4 = jnp.pad(ut4, ((0, 0), (0, 0), (0, 0), (0, lanes - streams)))
    y3, hfin = _s5(ut4, tt, ca, wb, apow, h0_lanes, nc)
    y = jnp.transpose(y3[:, :streams, :], (1, 0, 2)).reshape(n, mix)
    rest = p.shape[-1]
    whole = lambda w: pl.BlockSpec((n, w), lambda i: (0, 0))
    x1 = _post(y, p, x, mk, mv, wa["w_glu"], wa["b_glu"], wa["w_out"], wa["g_post"],
               glu=True, segs=tuple(((b * t_len, t_len),) for b in range(streams)), grid=(1,), rows=n,
               y_spec=whole(mix), p_spec=whole(rest), x_spec=whole(d), mem_spec=_resident(mk.shape))
    return x1, hfin


def kernel(x_prompt, x_sample, cache_k, cache_v, cache_mem_k, cache_mem_v, state_ssm, mem_prompt, w_in_a, w_out_a, g_pre_a, g_post_a, ssm_lam_re, ssm_lam_im, ssm_log_dt, ssm_b_re, ssm_b_im, ssm_c_re, ssm_c_im, ssm_d, w_glu, b_glu, g_kv, w_kv, w_in_b, w_out_b, g_pre_b, g_post_b, w_mem_k, w_mem_v):
    bsz, seq, d = x_prompt.shape
    dbsz, dseq, _ = x_sample.shape
    mix = w_glu.shape[-1]
    memw = w_mem_k.shape[-1]
    heads = mix // SB_HEAD_DIM
    groups = mix // SSM_GROUP
    n_mem = mem_prompt.shape[1]
    depth = w_mem_k.shape[0]
    assert depth == 2 and w_in_a.shape[0] == 1 and w_in_b.shape[0] == 1
    assert seq % CHUNK == 0 and (bsz * seq // CHUNK) % LANES == 0 and seq % ATTN_BLOCK == 0
    assert dseq % (2 * SUB) == 0 and dseq <= TABLE_T and dbsz <= LANES
    nc = seq // CHUNK
    assert nc & (nc - 1) == 0

    row = lambda v: v.astype(f32).reshape(1, -1)
    wa = dict(
        w_ut=w_in_a[0][:, :mix].T.astype(bf16),
        w_rest=w_in_a[0][:, mix:].astype(bf16),
        w_glu=w_glu[0].astype(bf16), b_glu=row(b_glu[0]),
        w_out=w_out_a[0].astype(bf16), g_pre=row(g_pre_a[0]), g_post=row(g_post_a[0]))
    qscale = math.log2(math.e) / math.sqrt(SB_HEAD_DIM)
    col_scale = jnp.where(jnp.arange(w_in_b.shape[-1]) < mix, qscale, 1.0).astype(f32)
    w_b = (w_in_b[0] * col_scale).astype(bf16)
    w_kv_b = w_kv.astype(bf16)
    w_out_bb = w_out_b[0].astype(bf16)
    w_mem = jnp.concatenate([w_mem_k[0], w_mem_k[1], w_mem_v[0], w_mem_v[1]], axis=1).astype(bf16)

    memf, memb = _memkv(mem_prompt.reshape(bsz * n_mem, d), w_mem, memw)
    mem_k_prompt = memf[:depth].reshape(depth, bsz, n_mem, MEM_HEADS, memw // MEM_HEADS)
    mem_v_prompt = memf[depth:].reshape(depth, bsz, n_mem, MEM_HEADS, memw // MEM_HEADS)
    mkp = memb[:depth].reshape(depth, bsz, n_mem, memw)
    mvp = memb[depth:].reshape(depth, bsz, n_mem, memw)
    mks = cache_mem_k.reshape(depth, dbsz, n_mem, memw).astype(bf16)
    mvs = cache_mem_v.reshape(depth, dbsz, n_mem, memw).astype(bf16)

    lre, lim, tab_in = _ssm_param_tables(ssm_lam_re[0], ssm_lam_im[0], ssm_log_dt[0], ssm_b_re[0], ssm_b_im[0],
                                         ssm_c_re[0], ssm_c_im[0], ssm_d[0])
    tables = _s5_tables(*tab_in)

    n_p = bsz * seq
    x1_p, hfin_p = _layer_a(x_prompt.reshape(n_p, d), CHUNK, nc, None, mkp[0], mvp[0], wa, tables, lre, lim, True)
    k_p, v_p, kvb_p, q_p, pr_p = _kvb(x1_p, row(g_kv), row(g_pre_b[0]), w_kv_b, w_b, mix, bsz)
    o_p = _sb_attn(q_p, kvb_p, bsz, seq, heads)
    rows_b = 2 * ATTN_BLOCK if seq % (2 * ATTN_BLOCK) == 0 else ATTN_BLOCK
    per_b = seq // rows_b
    tile = lambda w: pl.BlockSpec((rows_b, w), lambda i: (i, 0))
    y_p = _post(o_p, pr_p, x1_p, mkp[1], mvp[1], wa["w_glu"], wa["b_glu"], w_out_bb, row(g_post_b[0]),
                glu=False, segs=(((0, rows_b),),), grid=(n_p // rows_b,), rows=rows_b,
                y_spec=tile(mix), p_spec=tile(pr_p.shape[-1]), x_spec=tile(d),
                mem_spec=pl.BlockSpec((1, n_mem, memw), lambda i: (i // per_b, 0, 0)))

    n_s = dbsz * dseq
    st = state_ssm[0].astype(f32)
    h0_s = jnp.transpose(jnp.concatenate([st[..., 0], st[..., 1]], axis=-1), (1, 2, 0))
    h0_s = jnp.pad(h0_s, ((0, 0), (0, 0), (0, LANES - dbsz)))
    x1_s, hfin_s = _layer_a(x_sample.reshape(n_s, d), dseq, 1, h0_s, mks[0], mvs[0], wa, tables, lre, lim, False)
    k_s, v_s, kvb_s, q_s, pr_s = _kvb(x1_s, row(g_kv), row(g_pre_b[0]), w_kv_b, w_b, mix, dbsz)
    head_major = lambda a: jnp.transpose(a, (0, 2, 1, 3))
    o_s = _sb_attn_sample(q_s, kvb_s, head_major(cache_k), head_major(cache_v), dseq)
    whole = lambda w: pl.BlockSpec((n_s, w), lambda i: (0, 0))
    y_s = _post(o_s, pr_s, x1_s, mks[1], mvs[1], wa["w_glu"], wa["b_glu"], w_out_bb, row(g_post_b[0]),
                glu=False, segs=tuple(((b * dseq, dseq),) for b in range(dbsz)), grid=(1,), rows=n_s,
                y_spec=whole(mix), p_spec=whole(pr_s.shape[-1]), x_spec=whole(d),
                mem_spec=_resident(mks[1].shape))

    def ssm_out(hfin, lanes_idx):
        h = hfin[:, :, lanes_idx]
        h = jnp.transpose(h, (2, 0, 1))
        return jnp.stack([h[..., :SSM_STATE], h[..., SSM_STATE:]], axis=-1)[None]

    ssm_prompt = ssm_out(hfin_p, jnp.arange(bsz) * nc + (nc - 1)).astype(x_prompt.dtype)
    ssm_sample = ssm_out(hfin_s, jnp.arange(dbsz)).astype(state_ssm.dtype)
    return (y_p.reshape(bsz, seq, d), y_s.reshape(dbsz, dseq, d),
            head_major(k_p), head_major(v_p), head_major(k_s), head_major(v_s),
            ssm_prompt, ssm_sample, mem_k_prompt, mem_v_prompt)
```

```python
import functools
import math

import jax
import jax.numpy as jnp
from jax import lax
from jax.experimental import pallas as pl
from jax.experimental.pallas import tpu as pltpu

EPS = 1e-6
CHUNK = 64
SSM_GROUP = 16
SSM_STATE = 64
SB_HEAD_DIM = 128
MEM_HEADS = 4
SUB = 8
TABLE_T = 64
LANES = 128
ATTN_BLOCK = 256
MIB = 1024 * 1024

bf16 = jnp.bfloat16
f32 = jnp.float32

_NT = (((1,), (1,)), ((), ()))


def _params(vmem_mib, semantics):
    return pltpu.CompilerParams(vmem_limit_bytes=vmem_mib * MIB, dimension_semantics=semantics)


def _resident(shape):
    zeros = (0,) * len(shape)
    return pl.BlockSpec(shape, lambda *_: zeros, pipeline_mode=pl.Buffered(1))


def _rms_scale(x):
    return x * lax.rsqrt(jnp.mean(x * x, axis=-1, keepdims=True) + EPS)


def _sigmoid(x):
    return 1.0 / (1.0 + jnp.exp(-x))


def _gelu_tanh(x):
    c = math.sqrt(2.0 / math.pi)
    return 0.5 * x * (1.0 + jnp.tanh(c * (x + 0.044715 * (x * x * x))))


def _memkv_kernel(x_ref, w_ref, of_ref, ob_ref):
    acc = jnp.dot(x_ref[...].astype(bf16), w_ref[...], preferred_element_type=f32)
    width = of_ref.shape[-1]
    for j in range(of_ref.shape[0]):
        blk = acc[:, j * width:(j + 1) * width]
        of_ref[j] = blk
        ob_ref[j] = blk.astype(bf16)


def _memkv(mem, w_cat, width):
    rows, d = mem.shape
    nout = w_cat.shape[1] // width
    tm = 256
    return pl.pallas_call(
        _memkv_kernel,
        grid=(rows // tm,),
        in_specs=[pl.BlockSpec((tm, d), lambda i: (i, 0)), _resident(w_cat.shape)],
        out_specs=[pl.BlockSpec((nout, tm, width), lambda i: (0, i, 0)),
                   pl.BlockSpec((nout, tm, width), lambda i: (0, i, 0))],
        out_shape=[jax.ShapeDtypeStruct((nout, rows, width), f32),
                   jax.ShapeDtypeStruct((nout, rows, width), bf16)],
        compiler_params=_params(40, ("parallel",)),
        name="memkv",
    )(mem, w_cat)


def _offset_row_copies(x_hbm, buf, sem, step, slot, to_hbm=False):
    ns = buf.shape[1]
    out = []
    for s in range(ns):
        hbm = x_hbm.at[:, step * ns + s, :]
        vmem = buf.at[slot, s]
        out.append(pltpu.make_async_copy(vmem, hbm, sem.at[slot]) if to_hbm
                   else pltpu.make_async_copy(hbm, vmem, sem.at[slot]))
    return out


def _fetch_offset_rows(x_hbm, buf, sem):
    i = pl.program_id(0)

    @pl.when(i == 0)
    def _():
        for c in _offset_row_copies(x_hbm, buf, sem, 0, 0):
            c.start()

    @pl.when(i + 1 < pl.num_programs(0))
    def _():
        for c in _offset_row_copies(x_hbm, buf, sem, i + 1, (i + 1) % 2):
            c.start()

    slot = i % 2
    for c in _offset_row_copies(x_hbm, buf, sem, i, slot):
        c.wait()
    return slot


def _inproj_a_kernel(x_hbm, g_ref, wut_ref, wr_ref, ut_ref, p_ref, hn_ref, xbuf, xsem):
    _, ns, bc, _ = xbuf.shape
    slot = _fetch_offset_rows(x_hbm, xbuf, xsem)
    for s in range(ns):
        hn = (_rms_scale(xbuf[slot, s]) * g_ref[...]).astype(bf16)
        hn_ref[s * bc:(s + 1) * bc, :] = hn
        ut = lax.dot_general(wut_ref[...], hn, _NT, preferred_element_type=f32)
        ut_ref[s] = ut.reshape(ut_ref.shape[1:]).astype(bf16)
    step = 512
    for c in range(0, wr_ref.shape[1], step):
        r = jnp.dot(hn_ref[...], wr_ref[:, c:c + step], preferred_element_type=f32)
        p_ref[:, :, c:c + step] = r.astype(bf16).reshape(ns, bc, step)


def _inproj_a(x3, g, w_ut, w_rest, ns):
    bc, t, d = x3.shape
    mix = w_ut.shape[0]
    rest = w_rest.shape[1]
    groups = mix // SSM_GROUP
    return pl.pallas_call(
        _inproj_a_kernel,
        grid=(t // ns,),
        in_specs=[pl.BlockSpec(memory_space=pl.ANY),
                  _resident(g.shape), _resident(w_ut.shape), _resident(w_rest.shape)],
        out_specs=[pl.BlockSpec((ns, groups, SSM_GROUP, bc), lambda i: (i, 0, 0, 0)),
                   pl.BlockSpec((ns, bc, rest), lambda i: (i, 0, 0))],
        out_shape=[jax.ShapeDtypeStruct((t, groups, SSM_GROUP, bc), bf16),
                   jax.ShapeDtypeStruct((t, bc, rest), bf16)],
        scratch_shapes=[pltpu.VMEM((ns * bc, d), bf16), pltpu.VMEM((2, ns, bc, d), f32),
                        pltpu.SemaphoreType.DMA((2,))],
        compiler_params=_params(52, ("arbitrary",)),
        name="inproj_a",
    )(x3, g, w_ut, w_rest)


def _inproj_plain_kernel(x_ref, g_ref, wut_ref, wr_ref, u_ref, p_ref):
    hn = (_rms_scale(x_ref[...]) * g_ref[...]).astype(bf16)
    u_ref[...] = lax.dot_general(hn, wut_ref[...], _NT, preferred_element_type=f32).astype(bf16)
    p_ref[...] = jnp.dot(hn, wr_ref[...], preferred_element_type=f32).astype(bf16)


def _inproj_plain(x, g, w_ut, w_rest):
    rows, d = x.shape
    mix = w_ut.shape[0]
    rest = w_rest.shape[1]
    return pl.pallas_call(
        _inproj_plain_kernel,
        grid=(1,),
        in_specs=[pl.BlockSpec((rows, d), lambda i: (0, 0)), _resident(g.shape), _resident(w_ut.shape),
                  _resident(w_rest.shape)],
        out_specs=[pl.BlockSpec((rows, mix), lambda i: (0, 0)), pl.BlockSpec((rows, rest), lambda i: (0, 0))],
        out_shape=[jax.ShapeDtypeStruct((rows, mix), bf16), jax.ShapeDtypeStruct((rows, rest), bf16)],
        compiler_params=_params(40, ("arbitrary",)),
        name="inproj_plain",
    )(x, g, w_ut, w_rest)


def _s5_tables_kernel(*refs):
    for g in range(refs[0].shape[0]):
        _s5_tables_group(*[r.at[g] for r in refs])


def _s5_tables_group(lre_ref, lim_ref, caa_ref, cab_ref, ba_ref, bb_ref, dd_ref,
                     tt_ref, ca_ref, wb_ref, pw_ref, cas_ref, wbs_ref):
    t_len = TABLE_T
    rows = t_len * SSM_GROUP
    lre = lre_ref[...]
    lim = lim_ref[...]
    assert t_len == SUB * SUB
    k8 = lax.broadcasted_iota(jnp.int32, (SUB, LANES), 0).astype(f32)

    def powers(k):
        mag = jnp.exp(lre * k)
        th = lim * k
        return mag * jnp.cos(th), mag * jnp.sin(th)

    def outer(hi, lo):
        hr = jnp.concatenate([jnp.broadcast_to(hi[0][m:m + 1, :], (SUB, LANES)) for m in range(SUB)], axis=0)
        hi_ = jnp.concatenate([jnp.broadcast_to(hi[1][m:m + 1, :], (SUB, LANES)) for m in range(SUB)], axis=0)
        lr = jnp.concatenate([lo[0]] * SUB, axis=0)
        li = jnp.concatenate([lo[1]] * SUB, axis=0)
        return hr * lr - hi_ * li, hr * li + hi_ * lr

    low = powers(k8)
    kr, ki = outer(powers(SUB * k8), low)
    a1r, a1i = low[0][1:2, :], low[1][1:2, :]
    pr1, pi1 = kr * a1r - ki * a1i, kr * a1i + ki * a1r
    pr0, pi0 = outer(powers(SUB * (SUB - 1.0 - k8)), powers(SUB - 1.0 - k8))
    pw_ref[...] = jnp.concatenate([pr1, pi1, pr0, pi0], axis=1)

    def tile_rows(v, n):
        return jnp.broadcast_to(v[None], (n,) + v.shape).reshape(n * v.shape[0], v.shape[1])

    caa = caa_ref[...]
    cab = cab_ref[...]
    ba = ba_ref[...]
    bb = bb_ref[...]
    for t in range(t_len):
        pw = jnp.broadcast_to(pw_ref[t:t + 1, :], (SSM_GROUP, 4 * LANES))
        r0 = t * SSM_GROUP
        cas_ref[r0:r0 + SSM_GROUP, :] = pw[:, 0:128] * caa + pw[:, 128:256] * cab
        wbs_ref[r0:r0 + SSM_GROUP, :] = pw[:, 256:384] * ba + pw[:, 384:512] * bb
    ca = cas_ref[...]
    wbt = wbs_ref[...]
    ca_ref[...] = ca.astype(bf16)
    wb_ref[...] = wbt.T.astype(bf16)

    blk = SUB * SSM_GROUP
    nblk = rows // blk
    rt = wbt[rows - blk:, :]
    lagged = lax.dot_general(ca[:(nblk - 1) * blk, :], rt, _NT, preferred_element_type=f32,
                             precision=lax.Precision.HIGHEST)
    m = [None] + [lagged[(d - 1) * blk:d * blk, :] for d in range(1, nblk)]
    ca0 = jnp.concatenate([caa, ca[:blk - SSM_GROUP, :]], axis=0)
    kj = lax.dot_general(ca0, tile_rows(ba, SUB), _NT, preferred_element_type=f32,
                         precision=lax.Precision.HIGHEST)
    lane = lax.broadcasted_iota(jnp.int32, (SSM_GROUP, LANES), 1)
    hrow = lax.broadcasted_iota(jnp.int32, (SSM_GROUP, LANES), 0)
    s0_lane = lane // SSM_GROUP
    skip = jnp.where(lane % SSM_GROUP == hrow, dd_ref[...], 0.0)
    kjs = [kj[j * SSM_GROUP:(j + 1) * SSM_GROUP, :] for j in range(SUB)]
    kjs[0] = kjs[0] + skip
    drows = []
    for t0 in range(SUB):
        acc = jnp.zeros((SSM_GROUP, LANES), f32)
        for j in range(t0 + 1):
            acc = acc + jnp.where(s0_lane == t0 - j, kjs[j], 0.0)
        drows.append(acc)
    m[0] = jnp.concatenate(drows, axis=0)
    zero = jnp.zeros((blk, blk), f32)
    r1 = jnp.concatenate([m[d] for d in range(nblk - 1, -1, -1)], axis=1)
    r0 = jnp.concatenate([m[d] for d in range(nblk - 2, -1, -1)] + [zero], axis=1)
    tt_ref[...] = jnp.concatenate([r0, r1], axis=0).astype(bf16)


def _s5_tables(lre2, lim2, caa, cab, ba, bb, dd):
    groups = lre2.shape[0]
    rows = TABLE_T * SSM_GROUP
    per = 4 if groups % 4 == 0 else 1

    def gspec(shape):
        return pl.BlockSpec((per,) + shape, lambda g: (g,) + (0,) * len(shape))

    return pl.pallas_call(
        _s5_tables_kernel,
        grid=(groups // per,),
        in_specs=[gspec((1, LANES)), gspec((1, LANES)), gspec((SSM_GROUP, LANES)), gspec((SSM_GROUP, LANES)),
                  gspec((SSM_GROUP, LANES)), gspec((SSM_GROUP, LANES)), gspec((SSM_GROUP, 1))],
        out_specs=[gspec((2 * SUB * SSM_GROUP, rows)), gspec((rows, LANES)), gspec((LANES, rows))],
        out_shape=[jax.ShapeDtypeStruct((groups, 2 * SUB * SSM_GROUP, rows), bf16),
                   jax.ShapeDtypeStruct((groups, rows, LANES), bf16),
                   jax.ShapeDtypeStruct((groups, LANES, rows), bf16)],
        scratch_shapes=[pltpu.VMEM((per, TABLE_T, 4 * LANES), f32), pltpu.VMEM((per, rows, LANES), f32),
                        pltpu.VMEM((per, rows, LANES), f32)],
        compiler_params=_params(40, ("parallel",)),
        name="s5_tables",
    )(lre2, lim2, caa, cab, ba, bb, dd)


def _cmul(ar, ai, x):
    half = x.shape[0] // 2
    xr = x[:half]
    xi = x[half:]
    return jnp.concatenate([ar * xr - ai * xi, ar * xi + ai * xr], axis=0)


def _s5_group(z, tt, ca, wb, ap, h0, nc, rows):
    pair = 2 * SUB * SSM_GROUP
    table_rows = tt.shape[1]
    ys = []
    for t2 in range(rows // pair):
        kk = pair * (t2 + 1)
        ys.append(jnp.dot(tt[:, table_rows - kk:], z[:kk], preferred_element_type=f32))
    y = jnp.concatenate(ys, axis=0) if len(ys) > 1 else ys[0]
    state = jnp.dot(wb[:, table_rows - rows:], z, preferred_element_type=f32)
    if h0 is not None:
        state = state + _cmul(ap[:, 0:1], ap[:, 1:2], h0)
    lane = lax.broadcasted_iota(jnp.int32, state.shape, 1) % nc
    step = 0
    while (1 << step) < nc:
        sh = 1 << step
        shifted = jnp.where(lane >= sh, pltpu.roll(state, sh, axis=1), 0.0)
        state = state + _cmul(ap[:, 2 + 2 * step:3 + 2 * step], ap[:, 3 + 2 * step:4 + 2 * step], shifted)
        step += 1
    if nc > 1:
        h_in = jnp.where(lane >= 1, pltpu.roll(state, 1, axis=1), 0.0)
        if h0 is not None:
            h_in = h_in + h0
    else:
        h_in = h0 if h0 is not None else jnp.zeros_like(state)
    y = y + jnp.dot(ca[:rows, :], h_in.astype(bf16), preferred_element_type=f32)
    return y, state


def _s5_kernel(ut_ref, tt_ref, ca_ref, wb_ref, ap_ref, *rest, nc, has_h0):
    h0_ref = rest[0] if has_h0 else None
    y_ref, hfin_ref, yt_ref = rest[-3:]
    t_len, per, _, bc = ut_ref.shape
    rows = t_len * SSM_GROUP
    for g in range(per):
        z = ut_ref[:, g].reshape(rows, bc)
        y, state = _s5_group(z, tt_ref.at[g], ca_ref.at[g], wb_ref.at[g], ap_ref[g],
                             h0_ref[g] if has_h0 else None, nc, rows)
        hfin_ref[g] = state
        yt_ref[:, g * SSM_GROUP:(g + 1) * SSM_GROUP, :] = y.reshape(t_len, SSM_GROUP, bc)
    for t in range(t_len):
        y_ref[t] = yt_ref[t].T.astype(bf16)


def _s5(ut4, tt, ca, wb, apow, h0, nc):
    t_len, groups, _, bc = ut4.shape
    per = LANES // SSM_GROUP
    rows = t_len * SSM_GROUP
    last = TABLE_T * SSM_GROUP // rows - 1
    assert (last + 1) * rows == TABLE_T * SSM_GROUP
    mix = groups * SSM_GROUP
    ncol = apow.shape[-1]
    state_spec = pl.BlockSpec((per, 2 * SSM_STATE, bc), lambda G: (G, 0, 0))
    operands = (ut4, tt, ca, wb, apow) + (() if h0 is None else (h0,))
    return pl.pallas_call(
        functools.partial(_s5_kernel, nc=nc, has_h0=h0 is not None),
        grid=(groups // per,),
        in_specs=[pl.BlockSpec((t_len, per, SSM_GROUP, bc), lambda G: (0, G, 0, 0)),
                  pl.BlockSpec((per, 2 * SUB * SSM_GROUP, rows), lambda G: (G, 0, last)),
                  pl.BlockSpec((per, rows, LANES), lambda G: (G, 0, 0)),
                  pl.BlockSpec((per, LANES, rows), lambda G: (G, 0, last)),
                  pl.BlockSpec((per, SSM_STATE, ncol), lambda G: (G, 0, 0))] + ([] if h0 is None else [state_spec]),
        out_specs=[pl.BlockSpec((t_len, bc, LANES), lambda G: (0, 0, G)),
                   pl.BlockSpec((per, 2 * SSM_STATE, bc), lambda G: (G, 0, 0))],
        out_shape=[jax.ShapeDtypeStruct((t_len, bc, mix), bf16),
                   jax.ShapeDtypeStruct((groups, 2 * SSM_STATE, bc), f32)],
        scratch_shapes=[pltpu.VMEM((t_len, LANES, bc), f32)],
        compiler_params=_params(52, ("parallel",)),
        name="s5",
    )(*operands)


def _ld(ref, start, n, c0, c1):
    if len(ref.shape) == 3:
        s, r = divmod(start, ref.shape[1])
        return ref[s, r:r + n, c0:c1]
    return ref[start:start + n, c0:c1]


def _post_kernel(y_ref, p_ref, x_ref, mk_ref, mv_ref, wglu_ref, bglu_ref, wout_ref, gpost_ref,
                 o_ref, cat_ref, *dma, glu, segs, mem_scale):
    mix = y_ref.shape[-1]
    memw = mk_ref.shape[-1]
    hd = memw // MEM_HEADS
    rows = cat_ref.shape[0]
    if dma:
        xbuf, xsem, obuf, osem = dma
        slot = _fetch_offset_rows(x_ref, xbuf, xsem)
    y = y_ref[...].reshape(rows, mix).astype(f32)
    gate = p_ref[:, :mix] if len(p_ref.shape) == 2 else p_ref[:, :, :mix].reshape(rows, mix)
    gate = gate.astype(f32)
    heads = [(b, h, pieces) for b, pieces in enumerate(segs) for h in range(MEM_HEADS)]
    scores = []
    for b, h, pieces in heads:
        cq = mix + h * hd
        q = jnp.concatenate([_ld(p_ref, st, n, cq, cq + hd) for st, n in pieces], axis=0)
        scores.append(lax.dot_general(q, mk_ref[b, :, h * hd:(h + 1) * hd], _NT, preferred_element_type=f32))
    nchunk = max(rows // 256, 1)
    cr = rows // nchunk
    ys = [y[c * cr:(c + 1) * cr] for c in range(nchunk)]
    if glu:
        ys = [_gelu_tanh(yc) for yc in ys]
        zzs = [jnp.dot(yc.astype(bf16), wglu_ref[...], preferred_element_type=f32) for yc in ys]
        ys = [yc * _sigmoid(zz + bglu_ref[...]) for yc, zz in zip(ys, zzs)]
    outs = []
    for c, yc in enumerate(ys):
        gc = gate[c * cr:(c + 1) * cr]
        main = (yc * (gc * _sigmoid(gc))).astype(bf16)
        outs.append(jnp.dot(main, wout_ref[:mix, :], preferred_element_type=f32))
    out = jnp.concatenate(outs, axis=0) if nchunk > 1 else outs[0]
    probs = []
    for s in scores:
        s = s * mem_scale
        e = jnp.exp(s - jnp.max(s, axis=-1, keepdims=True))
        probs.append((e / jnp.sum(e, axis=-1, keepdims=True)).astype(bf16))
    for (b, h, pieces), p in zip(heads, probs):
        cg = mix + memw + h * hd
        mg = jnp.concatenate([_ld(p_ref, st, n, cg, cg + hd) for st, n in pieces], axis=0).astype(f32)
        o = jnp.dot(p, mv_ref[b, :, h * hd:(h + 1) * hd], preferred_element_type=f32)
        om = (o * (mg * _sigmoid(mg))).astype(bf16)
        off = 0
        for st, n in pieces:
            cat_ref[st:st + n, h * hd:(h + 1) * hd] = om[off:off + n]
            off += n
    out = out + jnp.dot(cat_ref[...], wout_ref[mix:, :], preferred_element_type=f32)
    d = out.shape[-1]
    branch = _rms_scale(out) * gpost_ref[...]
    if not dma:
        o_ref[...] = (x_ref[...].reshape(rows, d) + branch).reshape(o_ref.shape)
        return
    i = pl.program_id(0)
    last = pl.num_programs(0) - 1

    @pl.when(i >= 2)
    def _():
        for c in _offset_row_copies(o_ref, obuf, osem, i - 2, slot, to_hbm=True):
            c.wait()

    obuf[slot] = (xbuf[slot].reshape(rows, d) + branch).reshape(obuf.shape[1:])
    for c in _offset_row_copies(o_ref, obuf, osem, i, slot, to_hbm=True):
        c.start()

    @pl.when(i == last)
    def _():
        for c in _offset_row_copies(o_ref, obuf, osem, i, slot, to_hbm=True):
            c.wait()

    @pl.when((i == last) & (i >= 1))
    def _():
        for c in _offset_row_copies(o_ref, obuf, osem, i - 1, 1 - slot, to_hbm=True):
            c.wait()


def _post(y, p, x, mk, mv, wglu, bglu, wout, gpost, *, glu, segs, grid, y_spec, p_spec, x_spec, mem_spec, rows,
          offsets_per_step=None):
    mem_scale = 1.0 / math.sqrt(mk.shape[-1] // MEM_HEADS)
    scratch = [pltpu.VMEM((rows, mk.shape[-1]), bf16)]
    semantics = ("parallel",)
    if x_spec is None:
        x_spec = pl.BlockSpec(memory_space=pl.ANY)
        buf = pltpu.VMEM((2, offsets_per_step, x.shape[0], x.shape[2]), f32)
        scratch += [buf, pltpu.SemaphoreType.DMA((2,)), buf, pltpu.SemaphoreType.DMA((2,))]
        semantics = ("arbitrary",)
    return pl.pallas_call(
        functools.partial(_post_kernel, glu=glu, segs=segs, mem_scale=mem_scale),
        grid=grid,
        in_specs=[y_spec, p_spec, x_spec, mem_spec, mem_spec,
                  _resident(wglu.shape), _resident(bglu.shape), _resident(wout.shape), _resident(gpost.shape)],
        out_specs=x_spec,
        out_shape=jax.ShapeDtypeStruct(x.shape, f32),
        scratch_shapes=scratch,
        compiler_params=_params(52, semantics),
        name="post_glu" if glu else "post",
    )(y, p, x, mk, mv, wglu, bglu, wout, gpost)


def _kvb_kernel(x_ref, gkv_ref, gb_ref, wkv_ref, wb_ref, k_ref, v_ref, kvb_ref, q_ref, pr_ref):
    xs = _rms_scale(x_ref[...])
    hkv = (xs * gkv_ref[...]).astype(bf16)
    hb = (xs * gb_ref[...]).astype(bf16)
    nst, heads, rps, hd = k_ref.shape
    mix = heads * hd
    step = 512
    for c in range(0, wkv_ref.shape[1], step):
        r = jnp.dot(hkv, wkv_ref[:, c:c + step], preferred_element_type=f32)
        kvb_ref[:, c:c + step] = r.astype(bf16)
        for j in range(step // hd):
            h = (c % mix) // hd + j
            (k_ref if c < mix else v_ref)[:, h] = r[:, j * hd:(j + 1) * hd].reshape(nst, rps, hd)
    for c in range(0, wb_ref.shape[1], step):
        r = jnp.dot(hb, wb_ref[:, c:c + step], preferred_element_type=f32).astype(bf16)
        if c < mix:
            q_ref[:, c:c + step] = r
        else:
            pr_ref[:, c - mix:c - mix + step] = r


def _kvb(x, gkv, gb, wkv, wb, mix, streams):
    n, d = x.shape
    seq = n // streams
    tm = min(256, n)
    rps = min(tm, seq)
    assert tm % rps == 0 and seq % rps == 0 and rps % 8 == 0
    per_stream = seq // rps
    rest = wb.shape[1] - mix
    heads = mix // SB_HEAD_DIM
    row = lambda w: pl.BlockSpec((tm, w), lambda i: (i, 0))
    head_major = pl.BlockSpec((tm // rps, heads, rps, SB_HEAD_DIM), lambda i: (i // per_stream, 0, i % per_stream, 0))
    kv_shape = jax.ShapeDtypeStruct((streams, heads, seq, SB_HEAD_DIM), f32)
    return pl.pallas_call(
        _kvb_kernel,
        grid=(n // tm,),
        in_specs=[row(d), _resident(gkv.shape), _resident(gb.shape), _resident(wkv.shape), _resident(wb.shape)],
        out_specs=[head_major, head_major, row(2 * mix), row(mix), row(rest)],
        out_shape=[kv_shape, kv_shape,
                   jax.ShapeDtypeStruct((n, 2 * mix), bf16), jax.ShapeDtypeStruct((n, mix), bf16),
                   jax.ShapeDtypeStruct((n, rest), bf16)],
        compiler_params=_params(56, ("parallel",)),
        name="kvb",
    )(x, gkv, gb, wkv, wb)


_MASKED = -1e30


_EXP2_SAFE = 120.0


def _softplus2(z):
    return jnp.where(z > _EXP2_SAFE, z, jnp.log2(1.0 + jnp.exp2(z)))


def _sb_softplus_tri(z, ntri, mask):
    sp = _softplus2(z)
    if mask is not None:
        sp = jnp.where(mask, sp, 0.0)
    inner = jnp.dot(sp.astype(bf16), ntri, preferred_element_type=f32)
    t = (z - sp) + inner
    if mask is not None:
        t = jnp.where(mask, t, _MASKED)
    return t, inner[:, 0:1] - sp[:, 0:1]


def _sb_apply(t, d, v, carry, acc):
    w = jnp.exp2(t + carry)
    return acc + jnp.dot(w.astype(bf16), v, preferred_element_type=f32), carry + d


def _tri_and_mask(n):
    r = lax.broadcasted_iota(jnp.int32, (n, n), 0)
    c = lax.broadcasted_iota(jnp.int32, (n, n), 1)
    return jnp.where(r > c, -1.0, 0.0).astype(bf16), c < r


def _sb_attn_kernel(q_ref, k_ref, v_ref, o_ref, acc_ref, car_ref, t_ref, d_ref, *, nq):
    tq = ATTN_BLOCK
    i0 = pl.program_id(2) * nq
    key = lax.broadcasted_iota(jnp.int32, (tq, tq), 0)
    qry = lax.broadcasted_iota(jnp.int32, (tq, tq), 1)
    later = jnp.where(qry > key, -1.0, 0.0).astype(bf16)
    causal = key < qry

    def rows(ref, kb):
        off = kb * tq if isinstance(kb, int) else pl.multiple_of(kb * tq, tq)
        return ref[pl.ds(off, tq), :]

    def pieces(r_lo, diag):
        out = []
        r = r_lo
        if diag:
            out.append((r * tq, (r + 1) * tq, causal))
            r += 1
        while r < nq:
            n = min(2, nq - r)
            out.append((r * tq, (r + n) * tq, None))
            r += n
        return out

    def step(prev, cur):
        zs = []
        if cur is not None:
            k = rows(k_ref, cur[0])
            for a, b, mask in pieces(cur[1], cur[2]):
                zs.append((a, b, mask, lax.dot_general(k, q_ref[a:b, :], _NT, preferred_element_type=f32)))
        if prev is not None:
            v = rows(v_ref, prev[0])
            for a, b, _ in pieces(prev[1], False):
                carry = car_ref[:, a:b]
                w = jnp.exp2(t_ref[:, a:b] + carry).astype(bf16)
                acc_ref[:, a:b] += lax.dot_general(v, w, (((0,), (0,)), ((), ())), preferred_element_type=f32)
                car_ref[:, a:b] = carry + d_ref[:, a:b]
        for a, b, mask, z in zs:
            sp = _softplus2(z)
            if mask is not None:
                sp = jnp.where(mask, sp, 0.0)
            inner = jnp.dot(later, sp.astype(bf16), preferred_element_type=f32)
            t = (z - sp) + inner
            if mask is not None:
                t = jnp.where(mask, t, _MASKED)
            t_ref[:, a:b] = t
            d_ref[:, a:b] = inner[0:1, :] - sp[0:1, :]

    acc_ref[...] = jnp.zeros(acc_ref.shape, f32)
    car_ref[...] = jnp.zeros(car_ref.shape, f32)
    prev = None
    for p in range(nq):
        cur = (i0 + nq - 1 - p, nq - 1 - p, True)
        step(prev, cur)
        prev = cur

    def body(j, c):
        kb = i0 - 1 - 2 * j
        step((kb + 1, 0, False), (kb, 0, False))
        step((kb, 0, False), (kb - 1, 0, False))
        return c

    lax.fori_loop(0, i0 // 2, body, 0)
    step((0, 0, False), None)
    o_ref[...] = acc_ref[...].T.astype(bf16)


def _sb_attn(q, kvb, bsz, seq, heads):
    tq = ATTN_BLOCK
    hd = SB_HEAD_DIM
    nq = 4
    assert seq % (tq * nq) == 0
    steps = seq // (tq * nq)
    return pl.pallas_call(
        functools.partial(_sb_attn_kernel, nq=nq),
        grid=(bsz, heads, steps),
        in_specs=[pl.BlockSpec((nq * tq, hd), lambda b, h, i: (b * steps + i, h)),
                  pl.BlockSpec((seq, hd), lambda b, h, i: (b, h)),
                  pl.BlockSpec((seq, hd), lambda b, h, i: (b, heads + h))],
        out_specs=pl.BlockSpec((nq * tq, hd), lambda b, h, i: (b * steps + i, h)),
        out_shape=jax.ShapeDtypeStruct(q.shape, bf16),
        scratch_shapes=[pltpu.VMEM((hd, nq * tq), f32), pltpu.VMEM((1, nq * tq), f32),
                        pltpu.VMEM((tq, nq * tq), f32), pltpu.VMEM((1, nq * tq), f32)],
        compiler_params=_params(40, ("parallel", "parallel", "arbitrary")),
        name="sb_attn",
    )(q, kvb, kvb)


def _sb_attn_sample_kernel(q_ref, kn_ref, vn_ref, kc_ref, vc_ref, o_ref):
    tq = q_ref.shape[0]
    nh, past, hd = kc_ref.shape
    blk = min(ATTN_BLOCK, past)
    tri_n, causal = _tri_and_mask(tq)
    tri_p, _ = _tri_and_mask(blk)
    heads = []
    for h in range(nh):
        cols = slice(h * hd, (h + 1) * hd)
        blocks = [(kn_ref[:, cols], vn_ref[:, cols], tri_n, causal)]
        for j in range(past // blk - 1, -1, -1):
            blocks.append((kc_ref[h, j * blk:(j + 1) * blk, :].astype(bf16),
                           vc_ref[h, j * blk:(j + 1) * blk, :].astype(bf16), tri_p, None))
        heads.append((cols, blocks))
    zs = [[lax.dot_general(q_ref[:, cols], k, _NT, preferred_element_type=f32) for k, _, _, _ in blocks]
          for cols, blocks in heads]
    tds = [[_sb_softplus_tri(z, tri, mask) for z, (_, _, tri, mask) in zip(zh, blocks)]
           for zh, (_, blocks) in zip(zs, heads)]
    for td, (cols, blocks) in zip(tds, heads):
        acc = jnp.zeros((tq, hd), f32)
        carry = jnp.zeros((tq, 1), f32)
        for (t, d), (_, v, _, _) in zip(td, blocks):
            acc, carry = _sb_apply(t, d, v, carry, acc)
        o_ref[:, cols] = acc.astype(bf16)


def _sb_attn_sample(q, kvb, cache_k, cache_v, tq):
    bsz, heads, past, hd = cache_k.shape
    nh = 4 if heads % 4 == 0 else 1
    cache = pl.BlockSpec((None, nh, past, hd), lambda b, h: (b, h, 0, 0))
    groups = heads // nh
    return pl.pallas_call(
        _sb_attn_sample_kernel,
        grid=(bsz, groups),
        in_specs=[pl.BlockSpec((tq, nh * hd), lambda b, h: (b, h)),
                  pl.BlockSpec((tq, nh * hd), lambda b, h: (b, h)),
                  pl.BlockSpec((tq, nh * hd), lambda b, h: (b, groups + h)),
                  cache, cache],
        out_specs=pl.BlockSpec((tq, nh * hd), lambda b, h: (b, h)),
        out_shape=jax.ShapeDtypeStruct(q.shape, bf16),
        compiler_params=_params(40, ("parallel", "parallel")),
        name="sb_attn_sample",
    )(q, kvb, kvb, cache_k, cache_v)


def _ssm_param_tables(lam_re, lam_im, log_dt, b_re, b_im, c_re, c_im, dvec):
    dt = jnp.exp(log_dt.astype(f32))[:, None]
    lr = lam_re.astype(f32)
    li = lam_im.astype(f32)
    lre = lr * dt
    lim = li * dt
    mag = jnp.exp(lre)
    nr = mag * jnp.cos(lim) - 1.0
    ni = mag * jnp.sin(lim)
    den = lr * lr + li * li
    fr = ((nr * lr + ni * li) / den)[..., None]
    fi = ((ni * lr - nr * li) / den)[..., None]
    b_r = b_re.astype(f32)
    b_i = b_im.astype(f32)
    br = jnp.swapaxes(fr * b_r - fi * b_i, 1, 2)
    bi = jnp.swapaxes(fr * b_i + fi * b_r, 1, 2)
    cr = c_re.astype(f32)
    ci = c_im.astype(f32)
    dup = lambda v: jnp.concatenate([v, v], axis=-1)
    lre2 = dup(lre)[:, None, :]
    lim2 = dup(lim)[:, None, :]
    caa = jnp.concatenate([cr, -ci], axis=-1)
    cab = jnp.concatenate([-ci, -cr], axis=-1)
    ba = jnp.concatenate([br, bi], axis=-1)
    bb = jnp.concatenate([-bi, br], axis=-1)
    dd = dvec.astype(f32)[:, :, None]
    return lre, lim, (lre2, lim2, caa, cab, ba, bb, dd)


def _chunk_powers(lre, lim, t_len, nc):
    cols = []
    exps = [t_len]
    j = 0
    while (1 << j) < nc:
        exps.append(t_len * (1 << j))
        j += 1
    for e in exps:
        mag = jnp.exp(lre * e)
        cols += [mag * jnp.cos(lim * e), mag * jnp.sin(lim * e)]
    return jnp.stack(cols, axis=-1)


def _layer_a(x, t_len, nc, h0_lanes, mk, mv, wa, tables, lre, lim, prompt):
    n, d = x.shape
    mix = wa["w_glu"].shape[0]
    groups = mix // SSM_GROUP
    tt, ca, wb = tables
    apow = _chunk_powers(lre, lim, t_len, nc)
    if prompt:
        bc = n // t_len
        ns = 2
        x3 = x.reshape(bc, t_len, d)
        ut4, p = _inproj_a(x3, wa["g_pre"], wa["w_ut"], wa["w_rest"], ns)
        y, hfin = _s5(ut4, tt, ca, wb, apow, h0_lanes, nc)
        streams = bc // nc
        rest = p.shape[-1]
        segs = tuple(tuple((s * bc + b * nc, nc) for s in range(ns)) for b in range(streams))
        blk = lambda w: pl.BlockSpec((ns, bc, w), lambda i: (i, 0, 0))
        x1 = _post(y, p, x3, mk, mv, wa["w_glu"], wa["b_glu"], wa["w_out"], wa["g_post"],
                   glu=True, segs=segs, grid=(t_len // ns,), rows=ns * bc, offsets_per_step=ns,
                   y_spec=blk(mix), p_spec=blk(rest), x_spec=None, mem_spec=_resident(mk.shape))
        return x1.reshape(n, d), hfin
    streams = n // t_len
    lanes = h0_lanes.shape[-1]
    u, p = _inproj_plain(x, wa["g_pre"], wa["w_ut"], wa["w_rest"])
    ut4 = jnp.transpose(u.reshape(streams, t_len, groups, SSM_GROUP), (1, 2, 3, 0))
    ut4 = jnp.pad(ut4, ((0, 0), (0, 0), (0, 0), (0, lanes - streams)))
    y3, hfin = _s5(ut4, tt, ca, wb, apow, h0_lanes, nc)
    y = jnp.transpose(y3[:, :streams, :], (1, 0, 2)).reshape(n, mix)
    rest = p.shape[-1]
    whole = lambda w: pl.BlockSpec((n, w), lambda i: (0, 0))
    x1 = _post(y, p, x, mk, mv, wa["w_glu"], wa["b_glu"], wa["w_out"], wa["g_post"],
               glu=True, segs=tuple(((b * t_len, t_len),) for b in range(streams)), grid=(1,), rows=n,
               y_spec=whole(mix), p_spec=whole(rest), x_spec=whole(d), mem_spec=_resident(mk.shape))
    return x1, hfin


def kernel(x_prompt, x_sample, cache_k, cache_v, cache_mem_k, cache_mem_v, state_ssm, mem_prompt, w_in_a, w_out_a, g_pre_a, g_post_a, ssm_lam_re, ssm_lam_im, ssm_log_dt, ssm_b_re, ssm_b_im, ssm_c_re, ssm_c_im, ssm_d, w_glu, b_glu, g_kv, w_kv, w_in_b, w_out_b, g_pre_b, g_post_b, w_mem_k, w_mem_v):
    bsz, seq, d = x_prompt.shape
    dbsz, dseq, _ = x_sample.shape
    mix = w_glu.shape[-1]
    memw = w_mem_k.shape[-1]
    heads = mix // SB_HEAD_DIM
    groups = mix // SSM_GROUP
    n_mem = mem_prompt.shape[1]
    depth = w_mem_k.shape[0]
    assert depth == 2 and w_in_a.shape[0] == 1 and w_in_b.shape[0] == 1
    assert seq % CHUNK == 0 and (bsz * seq // CHUNK) % LANES == 0 and seq % ATTN_BLOCK == 0
    assert dseq % (2 * SUB) == 0 and dseq <= TABLE_T and dbsz <= LANES
    nc = seq // CHUNK
    assert nc & (nc - 1) == 0

    row = lambda v: v.astype(f32).reshape(1, -1)
    wa = dict(
        w_ut=w_in_a[0][:, :mix].T.astype(bf16),
        w_rest=w_in_a[0][:, mix:].astype(bf16),
        w_glu=w_glu[0].astype(bf16), b_glu=row(b_glu[0]),
        w_out=w_out_a[0].astype(bf16), g_pre=row(g_pre_a[0]), g_post=row(g_post_a[0]))
    qscale = math.log2(math.e) / math.sqrt(SB_HEAD_DIM)
    col_scale = jnp.where(jnp.arange(w_in_b.shape[-1]) < mix, qscale, 1.0).astype(f32)
    w_b = (w_in_b[0] * col_scale).astype(bf16)
    w_kv_b = w_kv.astype(bf16)
    w_out_bb = w_out_b[0].astype(bf16)
    w_mem = jnp.concatenate([w_mem_k[0], w_mem_k[1], w_mem_v[0], w_mem_v[1]], axis=1).astype(bf16)

    memf, memb = _memkv(mem_prompt.reshape(bsz * n_mem, d), w_mem, memw)
    mem_k_prompt = memf[:depth].reshape(depth, bsz, n_mem, MEM_HEADS, memw // MEM_HEADS)
    mem_v_prompt = memf[depth:].reshape(depth, bsz, n_mem, MEM_HEADS, memw // MEM_HEADS)
    mkp = memb[:depth].reshape(depth, bsz, n_mem, memw)
    mvp = memb[depth:].reshape(depth, bsz, n_mem, memw)
    mks = cache_mem_k.reshape(depth, dbsz, n_mem, memw).astype(bf16)
    mvs = cache_mem_v.reshape(depth, dbsz, n_mem, memw).astype(bf16)

    lre, lim, tab_in = _ssm_param_tables(ssm_lam_re[0], ssm_lam_im[0], ssm_log_dt[0], ssm_b_re[0], ssm_b_im[0],
                                         ssm_c_re[0], ssm_c_im[0], ssm_d[0])
    tables = _s5_tables(*tab_in)

    n_p = bsz * seq
    x1_p, hfin_p = _layer_a(x_prompt.reshape(n_p, d), CHUNK, nc, None, mkp[0], mvp[0], wa, tables, lre, lim, True)
    k_p, v_p, kvb_p, q_p, pr_p = _kvb(x1_p, row(g_kv), row(g_pre_b[0]), w_kv_b, w_b, mix, bsz)
    o_p = _sb_attn(q_p, kvb_p, bsz, seq, heads)
    rows_b = 2 * ATTN_BLOCK if seq % (2 * ATTN_BLOCK) == 0 else ATTN_BLOCK
    per_b = seq // rows_b
    tile = lambda w: pl.BlockSpec((rows_b, w), lambda i: (i, 0))
    y_p = _post(o_p, pr_p, x1_p, mkp[1], mvp[1], wa["w_glu"], wa["b_glu"], w_out_bb, row(g_post_b[0]),
                glu=False, segs=(((0, rows_b),),), grid=(n_p // rows_b,), rows=rows_b,
                y_spec=tile(mix), p_spec=tile(pr_p.shape[-1]), x_spec=tile(d),
                mem_spec=pl.BlockSpec((1, n_mem, memw), lambda i: (i // per_b, 0, 0)))

    n_s = dbsz * dseq
    st = state_ssm[0].astype(f32)
    h0_s = jnp.transpose(jnp.concatenate([st[..., 0], st[..., 1]], axis=-1), (1, 2, 0))
    h0_s = jnp.pad(h0_s, ((0, 0), (0, 0), (0, LANES - dbsz)))
    x1_s, hfin_s = _layer_a(x_sample.reshape(n_s, d), dseq, 1, h0_s, mks[0], mvs[0], wa, tables, lre, lim, False)
    k_s, v_s, kvb_s, q_s, pr_s = _kvb(x1_s, row(g_kv), row(g_pre_b[0]), w_kv_b, w_b, mix, dbsz)
    head_major = lambda a: jnp.transpose(a, (0, 2, 1, 3))
    o_s = _sb_attn_sample(q_s, kvb_s, head_major(cache_k), head_major(cache_v), dseq)
    whole = lambda w: pl.BlockSpec((n_s, w), lambda i: (0, 0))
    y_s = _post(o_s, pr_s, x1_s, mks[1], mvs[1], wa["w_glu"], wa["b_glu"], w_out_bb, row(g_post_b[0]),
                glu=False, segs=tuple(((b * dseq, dseq),) for b in range(dbsz)), grid=(1,), rows=n_s,
                y_spec=whole(mix), p_spec=whole(pr_s.shape[-1]), x_spec=whole(d),
                mem_spec=_resident(mks[1].shape))

    def ssm_out(hfin, lanes_idx):
        h = hfin[:, :, lanes_idx]
        h = jnp.transpose(h, (2, 0, 1))
        return jnp.stack([h[..., :SSM_STATE], h[..., SSM_STATE:]], axis=-1)[None]

    ssm_prompt = ssm_out(hfin_p, jnp.arange(bsz) * nc + (nc - 1)).astype(x_prompt.dtype)
    ssm_sample = ssm_out(hfin_s, jnp.arange(dbsz)).astype(state_ssm.dtype)
    return (y_p.reshape(bsz, seq, d), y_s.reshape(dbsz, dseq, d),
            head_major(k_p), head_major(v_p), head_major(k_s), head_major(v_s),
            ssm_prompt, ssm_sample, mem_k_prompt, mem_v_prompt)
```
